```python
import math, functools
import jax, jax.numpy as jnp
from jax import lax
import numpy as np

D_MODEL = 1024
BATCH = 4
SEQ = 4096
DEPTH = 2
DEC_BATCH = 2
DEC_SEQ = 16384
PAST_LEN = 128

HEAD_DIM = 64
Q_BLOCK = 128
EPS = 1e-6
A_HEADS = 4
A_VDIM = 2 * HEAD_DIM
A_QK_COLS = A_HEADS * 2 * HEAD_DIM
A_WIDTH = A_HEADS * A_VDIM
COLS_A = 2 * A_QK_COLS + A_WIDTH
B_HEADS = 8
B_KV_HEADS = 2
B_WIDTH = B_HEADS * HEAD_DIM
COLS_B = B_HEADS * HEAD_DIM + 2 * B_KV_HEADS * HEAD_DIM
ROPE_BASE = 10000.0
GRID_W = 64
SSM_WIDTH = 512
SSM_P = 16
SSM_GROUPS = SSM_WIDTH // SSM_P
SSM_N = 64
COLS_C = SSM_WIDTH
N_BRANCH = 3
COLS_GATE = N_BRANCH * D_MODEL
IN_COLS = COLS_A + COLS_B + COLS_C + COLS_GATE
MOE_GROUPS = 4
EXPERTS_PER_GROUP = 4
N_EXPERTS = MOE_GROUPS * EXPERTS_PER_GROUP
TOP_K = 2
D_FF_EXPERT = 512

kernel_name = "hybrid_bidir_encoder_diffattn_gqa_s5_hmoe"


def rmsnorm(x, w):
    x32 = x.astype(jnp.float32)
    ms = jnp.mean(x32 * x32, axis=-1, keepdims=True)
    return (x32 * lax.rsqrt(ms + EPS)).astype(x.dtype) * w


def diff_attention(q1, q2, k1, k2, v, lam, lam_init, subln_w):
    Bsz, L, H, d = q1.shape
    vd = v.shape[-1]
    nb = L // Q_BLOCK
    scale = d ** -0.5
    slopes = 2.0 ** (-8.0 * jnp.arange(1, H + 1, dtype=jnp.float32) / H)
    key_pos = jnp.arange(L, dtype=jnp.float32)

    def to_blocks(t):
        return jnp.moveaxis(t.reshape(Bsz, nb, Q_BLOCK, H, d), 1, 0)

    def one_block(args):
        i, qb1, qb2 = args
        q_pos = (i * Q_BLOCK + jnp.arange(Q_BLOCK)).astype(jnp.float32)
        bias = -slopes[:, None, None] * jnp.abs(q_pos[:, None] - key_pos[None, :])
        s1 = jnp.einsum('bqhd,bkhd->bhqk', qb1, k1).astype(jnp.float32) * scale + bias
        s2 = jnp.einsum('bqhd,bkhd->bhqk', qb2, k2).astype(jnp.float32) * scale + bias
        p = jax.nn.softmax(s1, axis=-1) - lam * jax.nn.softmax(s2, axis=-1)
        return jnp.einsum('bhqk,bkhv->bqhv', p.astype(v.dtype), v)

    out = lax.map(one_block, (jnp.arange(nb), to_blocks(q1), to_blocks(q2)))
    out = jnp.moveaxis(out, 0, 1).reshape(Bsz, L, H, vd)
    out = rmsnorm(out, subln_w) * (1.0 - lam_init)
    return out.reshape(Bsz, L, H * vd)


def axial_rope(x, row, col):
    half = HEAD_DIM // 2
    inv = ROPE_BASE ** (-jnp.arange(0, half, 2, dtype=jnp.float32) / half)

    def rot(xa, pos):
        ang = pos[:, None] * inv[None, :]
        cos = jnp.cos(ang)[None, :, None, :]
        sin = jnp.sin(ang)[None, :, None, :]
        x1 = xa[..., : half // 2].astype(jnp.float32)
        x2 = xa[..., half // 2:].astype(jnp.float32)
        return jnp.concatenate([x1 * cos - x2 * sin, x1 * sin + x2 * cos], axis=-1)

    out = jnp.concatenate([rot(x[..., :half], row), rot(x[..., half:], col)], axis=-1)
    return out.astype(x.dtype)


def gqa_attention(q, k, v):
    Bsz, L = q.shape[:2]
    grp = B_HEADS // B_KV_HEADS
    nb = L // Q_BLOCK
    scale = HEAD_DIM ** -0.5
    qb = jnp.moveaxis(q.reshape(Bsz, nb, Q_BLOCK, B_KV_HEADS, grp, HEAD_DIM), 1, 0)

    def one_block(qblk):
        s = jnp.einsum('bqhgd,bkhd->bhgqk', qblk, k).astype(jnp.float32) * scale
        p = jax.nn.softmax(s, axis=-1)
        return jnp.einsum('bhgqk,bkhd->bqhgd', p.astype(v.dtype), v)

    out = lax.map(one_block, qb)
    return jnp.moveaxis(out, 0, 1).reshape(Bsz, L, B_WIDTH)


def _ssm_combine(c1, c2):
    a1r, a1i, b1r, b1i = c1
    a2r, a2i, b2r, b2i = c2
    return (a2r * a1r - a2i * a1i,
            a2r * a1i + a2i * a1r,
            a2r * b1r - a2i * b1i + b2r,
            a2r * b1i + a2i * b1r + b2i)


def s5_bidirectional(u, A_re, A_im, log_dt, B_re, B_im, C_re, C_im, D_skip, w_glu, b_glu):
    Bsz, L, _ = u.shape
    f32 = jnp.float32
    ug = u.reshape(Bsz, L, SSM_GROUPS, SSM_P).astype(f32)
    Br, Bi = B_re.astype(f32), B_im.astype(f32)
    Cr, Ci = C_re.astype(f32), C_im.astype(f32)

    def direction(dirn, reverse):
        dt = jnp.exp(log_dt[dirn].astype(f32))[:, None]
        ar, ai = A_re[dirn].astype(f32), A_im[dirn].astype(f32)
        mag = jnp.exp(dt * ar)
        er, ei = mag * jnp.cos(dt * ai), mag * jnp.sin(dt * ai)
        den = ar * ar + ai * ai
        fr = ((er - 1.0) * ar + ei * ai) / den
        fi = (ei * ar - (er - 1.0) * ai) / den
        bbr = fr[..., None] * Br - fi[..., None] * Bi
        bbi = fr[..., None] * Bi + fi[..., None] * Br
        bur = jnp.einsum('blgp,gnp->blgn', ug, bbr)
        bui = jnp.einsum('blgp,gnp->blgn', ug, bbi)
        abr = jnp.broadcast_to(er, bur.shape)
        abi = jnp.broadcast_to(ei, bui.shape)
        _, _, xr, xi = lax.associative_scan(_ssm_combine, (abr, abi, bur, bui), reverse=reverse, axis=1)
        return jnp.einsum('gpn,blgn->blgp', Cr, xr) - jnp.einsum('gpn,blgn->blgp', Ci, xi)

    y = direction(0, False) + direction(1, True) + D_skip.astype(f32).reshape(SSM_GROUPS, SSM_P) * ug
    y = y.reshape(Bsz, L, SSM_WIDTH).astype(u.dtype)
    g = jax.nn.gelu(y)
    return g * jax.nn.sigmoid(g @ w_glu + b_glu)


def hier_moe(x, w_rg, b_rg, w_re, b_re, w_gate, w_up, w_down):
    Bsz, L, D = x.shape
    t = x.reshape(-1, D)
    lg = (t @ w_rg + b_rg).astype(jnp.float32)
    pg = jax.nn.softmax(lg, axis=-1)
    g_idx = jnp.argmax(lg, axis=-1)
    g_w = jnp.take_along_axis(pg, g_idx[:, None], axis=-1)
    le = (t @ w_re + b_re).astype(jnp.float32).reshape(-1, MOE_GROUPS, EXPERTS_PER_GROUP)
    le_sel = jnp.take_along_axis(le, g_idx[:, None, None], axis=1)[:, 0]
    top_v, top_i = lax.top_k(le_sel, TOP_K)
    w_top = jax.nn.softmax(top_v, axis=-1) * g_w
    eid = g_idx[:, None] * EXPERTS_PER_GROUP + top_i
    combine = jnp.sum(jax.nn.one_hot(eid, N_EXPERTS, dtype=jnp.float32) * w_top[..., None], axis=1)
    out = jnp.zeros(t.shape, jnp.float32)
    for e in range(N_EXPERTS):
        h = jax.nn.silu(t @ w_gate[e]) * (t @ w_up[e])
        out = out + combine[:, e:e + 1] * (h @ w_down[e]).astype(jnp.float32)
    return out.astype(x.dtype).reshape(Bsz, L, D)


def trunk(x, norm1_w, w_in, lam_q1, lam_k1, lam_q2, lam_k2, diff_subln_w, q_norm_w, k_norm_w,
          ssm_A_re, ssm_A_im, ssm_log_dt, ssm_B_re, ssm_B_im, ssm_C_re, ssm_C_im, ssm_D, w_glu, b_glu,
          w_proj_a, w_proj_b, w_proj_c, w_out, norm2_w, w_router_group, b_router_group,
          w_router_expert, b_router_expert, w_exp_gate, w_exp_up, w_exp_down, final_norm_w):
    Bsz, L, _ = x.shape
    rows = L // GRID_W
    row = jnp.repeat(jnp.arange(rows, dtype=jnp.float32), GRID_W)
    col = jnp.tile(jnp.arange(GRID_W, dtype=jnp.float32), rows)
    f32 = jnp.float32
    for l in range(DEPTH):
        xn = rmsnorm(x, norm1_w[l])
        z = xn @ w_in[l]
        za, zb, zc, zg = jnp.split(z, [COLS_A, COLS_A + COLS_B, COLS_A + COLS_B + COLS_C], axis=-1)

        qa, ka, va = jnp.split(za, [A_QK_COLS, 2 * A_QK_COLS], axis=-1)
        qa = qa.reshape(Bsz, L, A_HEADS, 2, HEAD_DIM)
        ka = ka.reshape(Bsz, L, A_HEADS, 2, HEAD_DIM)
        va = va.reshape(Bsz, L, A_HEADS, A_VDIM)
        lam_init = 0.8 - 0.6 * math.exp(-0.3 * l)
        lam = (jnp.exp(jnp.sum(lam_q1[l].astype(f32) * lam_k1[l].astype(f32)))
               - jnp.exp(jnp.sum(lam_q2[l].astype(f32) * lam_k2[l].astype(f32))) + lam_init)
        out_a = diff_attention(qa[..., 0, :], qa[..., 1, :], ka[..., 0, :], ka[..., 1, :], va,
                               lam, lam_init, diff_subln_w[l])

        qb, kb, vb = jnp.split(zb, [B_HEADS * HEAD_DIM, B_HEADS * HEAD_DIM + B_KV_HEADS * HEAD_DIM], axis=-1)
        qb = axial_rope(rmsnorm(qb.reshape(Bsz, L, B_HEADS, HEAD_DIM), q_norm_w[l]), row, col)
        kb = axial_rope(rmsnorm(kb.reshape(Bsz, L, B_KV_HEADS, HEAD_DIM), k_norm_w[l]), row, col)
        vb = vb.reshape(Bsz, L, B_KV_HEADS, HEAD_DIM)
        out_b = gqa_attention(qb, kb, vb)

        out_c = s5_bidirectional(zc, ssm_A_re[l], ssm_A_im[l], ssm_log_dt[l], ssm_B_re[l], ssm_B_im[l],
                                 ssm_C_re[l], ssm_C_im[l], ssm_D[l], w_glu[l], b_glu[l])

        gates = jax.nn.sigmoid(zg.reshape(Bsz, L, N_BRANCH, D_MODEL))
        merged = (gates[:, :, 0] * (out_a @ w_proj_a[l])
                  + gates[:, :, 1] * (out_b @ w_proj_b[l])
                  + gates[:, :, 2] * (out_c @ w_proj_c[l]))
        x = x + merged @ w_out[l]

        x = x + hier_moe(rmsnorm(x, norm2_w[l]), w_router_group[l], b_router_group[l],
                         w_router_expert[l], b_router_expert[l],
                         w_exp_gate[l], w_exp_up[l], w_exp_down[l])
    return rmsnorm(x, final_norm_w)


def setup_inputs(seed: int = 0) -> dict:
    key = jax.random.key(seed)
    ks = jax.random.split(key, 40)
    f32 = jnp.float32

    def nrm(k, shape, scale):
        return jax.random.normal(k, shape, f32) * scale

    n_idx = jnp.arange(SSM_N, dtype=f32)
    return {
        "x_prompt": nrm(ks[0], (BATCH, SEQ, D_MODEL), 1.0),
        "x_sample": nrm(ks[1], (DEC_BATCH, DEC_SEQ, D_MODEL), 1.0),
        "norm1_w": 1.0 + nrm(ks[2], (DEPTH, D_MODEL), 0.02),
        "w_in": nrm(ks[3], (DEPTH, D_MODEL, IN_COLS), D_MODEL ** -0.5),
        "lam_q1": nrm(ks[4], (DEPTH, HEAD_DIM), 0.1),
        "lam_k1": nrm(ks[5], (DEPTH, HEAD_DIM), 0.1),
        "lam_q2": nrm(ks[6], (DEPTH, HEAD_DIM), 0.1),
        "lam_k2": nrm(ks[7], (DEPTH, HEAD_DIM), 0.1),
        "diff_subln_w": 1.0 + nrm(ks[8], (DEPTH, A_VDIM), 0.02),
        "q_norm_w": 1.0 + nrm(ks[9], (DEPTH, HEAD_DIM), 0.02),
        "k_norm_w": 1.0 + nrm(ks[10], (DEPTH, HEAD_DIM), 0.02),
        "ssm_A_re": -0.5 + nrm(ks[11], (DEPTH, 2, SSM_GROUPS, SSM_N), 0.01),
        "ssm_A_im": jnp.pi * n_idx + nrm(ks[12], (DEPTH, 2, SSM_GROUPS, SSM_N), 0.01),
        "ssm_log_dt": jax.random.uniform(ks[13], (DEPTH, 2, SSM_GROUPS), f32, math.log(1e-3), math.log(1e-1)),
        "ssm_B_re": nrm(ks[14], (DEPTH, SSM_GROUPS, SSM_N, SSM_P), (2 * SSM_P) ** -0.5),
        "ssm_B_im": nrm(ks[15], (DEPTH, SSM_GROUPS, SSM_N, SSM_P), (2 * SSM_P) ** -0.5),
        "ssm_C_re": nrm(ks[16], (DEPTH, SSM_GROUPS, SSM_P, SSM_N), SSM_N ** -0.5),
        "ssm_C_im": nrm(ks[17], (DEPTH, SSM_GROUPS, SSM_P, SSM_N), SSM_N ** -0.5),
        "ssm_D": nrm(ks[18], (DEPTH, SSM_WIDTH), 1.0),
        "w_glu": nrm(ks[19], (DEPTH, SSM_WIDTH, SSM_WIDTH), SSM_WIDTH ** -0.5),
        "b_glu": nrm(ks[20], (DEPTH, SSM_WIDTH), 0.01),
        "w_proj_a": nrm(ks[21], (DEPTH, A_WIDTH, D_MODEL), A_WIDTH ** -0.5),
        "w_proj_b": nrm(ks[22], (DEPTH, B_WIDTH, D_MODEL), B_WIDTH ** -0.5),
        "w_proj_c": nrm(ks[23], (DEPTH, SSM_WIDTH, D_MODEL), SSM_WIDTH ** -0.5),
        "w_out": nrm(ks[24], (DEPTH, D_MODEL, D_MODEL), D_MODEL ** -0.5),
        "norm2_w": 1.0 + nrm(ks[25], (DEPTH, D_MODEL), 0.02),
        "w_router_group": nrm(ks[26], (DEPTH, D_MODEL, MOE_GROUPS), D_MODEL ** -0.5),
        "b_router_group": nrm(ks[27], (DEPTH, MOE_GROUPS), 0.01),
        "w_router_expert": nrm(ks[28], (DEPTH, D_MODEL, N_EXPERTS), D_MODEL ** -0.5),
        "b_router_expert": nrm(ks[29], (DEPTH, N_EXPERTS), 0.01),
        "w_exp_gate": nrm(ks[30], (DEPTH, N_EXPERTS, D_MODEL, D_FF_EXPERT), D_MODEL ** -0.5),
        "w_exp_up": nrm(ks[31], (DEPTH, N_EXPERTS, D_MODEL, D_FF_EXPERT), D_MODEL ** -0.5),
        "w_exp_down": nrm(ks[32], (DEPTH, N_EXPERTS, D_FF_EXPERT, D_MODEL), D_FF_EXPERT ** -0.5),
        "final_norm_w": 1.0 + nrm(ks[33], (D_MODEL,), 0.02),
    }


def reference(x_prompt, x_sample, norm1_w, w_in, lam_q1, lam_k1, lam_q2, lam_k2, diff_subln_w,
              q_norm_w, k_norm_w, ssm_A_re, ssm_A_im, ssm_log_dt, ssm_B_re, ssm_B_im, ssm_C_re, ssm_C_im,
              ssm_D, w_glu, b_glu, w_proj_a, w_proj_b, w_proj_c, w_out, norm2_w, w_router_group,
              b_router_group, w_router_expert, b_router_expert, w_exp_gate, w_exp_up, w_exp_down,
              final_norm_w):
    weights = (norm1_w, w_in, lam_q1, lam_k1, lam_q2, lam_k2, diff_subln_w, q_norm_w, k_norm_w,
               ssm_A_re, ssm_A_im, ssm_log_dt, ssm_B_re, ssm_B_im, ssm_C_re, ssm_C_im, ssm_D, w_glu, b_glu,
               w_proj_a, w_proj_b, w_proj_c, w_out, norm2_w, w_router_group, b_router_group,
               w_router_expert, b_router_expert, w_exp_gate, w_exp_up, w_exp_down, final_norm_w)
    y_prompt = trunk(x_prompt, *weights)
    y_sample = trunk(x_sample, *weights)
    return (y_prompt, y_sample)
```

```python
import functools
import math

import jax
import jax.numpy as jnp
from jax import lax
from jax.experimental import pallas as pl
from jax.experimental.pallas import tpu as pltpu

F32 = jnp.float32
BF16 = jnp.bfloat16

D_MODEL = 1024
DEPTH = 2
HEAD_DIM = 64
EPS = 1e-6
A_HEADS = 4
A_VDIM = 2 * HEAD_DIM
A_QK_COLS = A_HEADS * 2 * HEAD_DIM
A_WIDTH = A_HEADS * A_VDIM
COLS_A = 2 * A_QK_COLS + A_WIDTH
B_HEADS = 8
B_KV_HEADS = 2
B_GROUP = B_HEADS // B_KV_HEADS
B_WIDTH = B_HEADS * HEAD_DIM
COLS_B = B_HEADS * HEAD_DIM + 2 * B_KV_HEADS * HEAD_DIM
ROPE_BASE = 10000.0
GRID_W = 64
SSM_WIDTH = 512
SSM_P = 16
SSM_GROUPS = SSM_WIDTH // SSM_P
SSM_N = 64
COLS_C = SSM_WIDTH
N_BRANCH = 3
MOE_GROUPS = 4
EXPERTS_PER_GROUP = 4
N_EXPERTS = MOE_GROUPS * EXPERTS_PER_GROUP
D_FF_EXPERT = 512

LANES = 128
SUBLANES = 8
VMEM_LIMIT_BYTES = 56 * 1024 * 1024

SSM_CHUNK = 64
SSM_FLAT = SSM_CHUNK * SSM_P
SSM_STATE_COLS = 4 * LANES

ROW_TILE = 512
MERGE_TILE = 256
ATT_Q_TILE = 256
ATT_KV_TILE = 1024


def _compiler_params(semantics):
    return pltpu.CompilerParams(dimension_semantics=semantics, vmem_limit_bytes=VMEM_LIMIT_BYTES)


def _const_spec(shape):
    zeros = (0,) * len(shape)
    return pl.BlockSpec(shape, lambda *_: zeros)


def _rms(x, w):
    ms = jnp.mean(x * x, axis=-1, keepdims=True)
    return x * lax.rsqrt(ms + EPS) * w


def _normed_rows(x_ref, nw_ref, xn_ref):
    @pl.when(pl.program_id(1) == 0)
    def _():
        xn_ref[...] = _rms(x_ref[...], nw_ref[...]).astype(BF16)


def _proj_a_kernel(x_ref, nw_ref, w_ref, e_ref, o_ref, xn_ref):
    _normed_rows(x_ref, nw_ref, xn_ref)
    z = jnp.dot(xn_ref[...], w_ref[...], preferred_element_type=F32)
    o_ref[...] = (z + e_ref[...]).astype(o_ref.dtype)


def _proj_c_kernel(x_ref, nw_ref, w_ref, o32_ref, o16_ref, xn_ref):
    _normed_rows(x_ref, nw_ref, xn_ref)
    z = jnp.dot(xn_ref[...], w_ref[...], preferred_element_type=F32)
    o32_ref[...] = z
    o16_ref[...] = z.astype(BF16)


def _proj_b_kernel(x_ref, nw_ref, w_ref, qw_ref, kw_ref, e_ref, cos_ref, sa_ref, sb_ref, o_ref, xn_ref,
                   *, n_q_tiles, heads_per_tile):
    _normed_rows(x_ref, nw_ref, xn_ref)
    z = jnp.dot(xn_ref[...], w_ref[...], preferred_element_type=F32)
    cos, sa, sb = cos_ref[...], sa_ref[...], sb_ref[...]
    j = pl.program_id(1)

    def norm_rope(zh, w):
        ms = jnp.sum(zh * zh, axis=-1, keepdims=True) * (1.0 / HEAD_DIM)
        y = zh * lax.rsqrt(ms + EPS) * w
        return y * cos + pltpu.roll(y, LANES - 16, 1) * sa + pltpu.roll(y, 16, 1) * sb

    @pl.when(j < n_q_tiles)
    def _():
        for h in range(heads_per_tile):
            sl = slice(h * LANES, (h + 1) * LANES)
            o_ref[:, sl] = norm_rope(z[:, sl], qw_ref[...]).astype(o_ref.dtype)

    @pl.when(j == n_q_tiles)
    def _():
        for h in range(B_KV_HEADS):
            sl = slice(h * LANES, (h + 1) * LANES)
            o_ref[:, sl] = norm_rope(z[:, sl], kw_ref[...]).astype(o_ref.dtype)
        sl = slice(B_KV_HEADS * LANES, 2 * B_KV_HEADS * LANES)
        o_ref[:, sl] = (z[:, sl] + e_ref[...]).astype(o_ref.dtype)


def _proj_call(name, kernel, x2d, nw, w, extra_inputs, extra_specs, out_shapes, out_specs, tn):
    T = x2d.shape[0]
    tm = ROW_TILE
    n = w.shape[1]
    grid = (T // tm, n // tn)
    in_specs = [pl.BlockSpec((tm, D_MODEL), lambda i, j: (i, 0)),
                _const_spec((1, D_MODEL)),
                pl.BlockSpec((D_MODEL, tn), lambda i, j: (0, j))] + extra_specs
    return pl.pallas_call(
        kernel,
        grid=grid,
        in_specs=in_specs,
        out_specs=out_specs,
        out_shape=out_shapes,
        scratch_shapes=[pltpu.VMEM((tm, D_MODEL), BF16)],
        compiler_params=_compiler_params(("parallel", "arbitrary")),
        name=name,
    )(x2d, nw, w, *extra_inputs)


def _flash(qs, k_ref, v_ref, m_ref, acc_ref, tk, bias_fn):
    L = k_ref.shape[0]
    m_ref[...] = jnp.full(m_ref.shape, -jnp.inf, F32)
    acc_ref[...] = jnp.zeros(acc_ref.shape, F32)

    def body(c, carry):
        start = pl.multiple_of(c * tk, tk)
        kc = k_ref[pl.ds(start, tk), :]
        s = lax.dot_general(qs, kc, (((1,), (1,)), ((), ())), preferred_element_type=F32)
        if bias_fn is not None:
            s = s + bias_fn(start)
        m_prev = m_ref[...]
        m_new = jnp.maximum(m_prev, jnp.max(s, axis=-1, keepdims=True))
        p = jnp.exp(s - m_new)
        alpha = jnp.exp(m_prev - m_new)
        pv = jnp.dot(p.astype(BF16), v_ref[pl.ds(start, tk), :], preferred_element_type=F32)
        acc_ref[...] = alpha * acc_ref[...] + pv
        m_ref[...] = m_new
        return carry

    lax.fori_loop(0, L // tk, body, 0)


def _diff_attn_kernel(q_ref, k_ref, v_ref, slope_ref, lam_ref, subw_ref, o_ref, m_ref, acc_ref,
                      *, tq, tk, lam_init):
    q = q_ref[...]
    lane = lax.broadcasted_iota(jnp.int32, q.shape, 1)
    zero = jnp.zeros_like(q)
    qs = jnp.concatenate([jnp.where(lane < HEAD_DIM, q, zero), jnp.where(lane >= HEAD_DIM, q, zero)], axis=0)
    slope = slope_ref[...][:, :1]
    row = lax.broadcasted_iota(jnp.int32, (2 * tq, 1), 0)
    q_pos = (pl.program_id(2) * tq + jnp.where(row >= tq, row - tq, row)).astype(F32)

    def bias_fn(start):
        k_pos = (start + lax.broadcasted_iota(jnp.int32, (1, tk), 1)).astype(F32)
        return -slope * jnp.abs(q_pos - k_pos)

    _flash(qs, k_ref, v_ref, m_ref, acc_ref, tk, bias_fn)

    acc = acc_ref[...]
    o = acc[:, :A_VDIM] / acc[:, A_VDIM:A_VDIM + 1]
    lv = lam_ref[...]
    lam = (jnp.exp(jnp.sum(lv[0:1] * lv[1:2], axis=-1, keepdims=True))
           - jnp.exp(jnp.sum(lv[2:3] * lv[3:4], axis=-1, keepdims=True)) + lam_init)
    d = o[:tq] - lam * o[tq:]
    o_ref[...] = (_rms(d, subw_ref[...]) * (1.0 - lam_init)).astype(o_ref.dtype)


def _gqa_attn_kernel(q_ref, k_ref, v_ref, o_ref, m_ref, acc_ref, *, tq, tk):
    qs = jnp.concatenate([q_ref[:, h * LANES:(h + 1) * LANES] for h in range(B_GROUP)], axis=0)
    _flash(qs, k_ref, v_ref, m_ref, acc_ref, tk, None)
    acc = acc_ref[...]
    o = acc / acc[:, HEAD_DIM:HEAD_DIM + 1]
    for h in range(B_GROUP):
        o_ref[:, h * LANES:(h + 1) * LANES] = o[h * tq:(h + 1) * tq].astype(o_ref.dtype)


def _attn_tiles(L):
    return min(ATT_Q_TILE, L), min(ATT_KV_TILE, L)


def _diff_attention(qkv, slopes, lamv, subw, lam_init, Bsz, L):
    tq, tk = _attn_tiles(L)
    nq = A_HEADS
    kernel = functools.partial(_diff_attn_kernel, tq=tq, tk=tk, lam_init=lam_init)
    return pl.pallas_call(
        kernel,
        grid=(Bsz, A_HEADS, L // tq),
        in_specs=[
            pl.BlockSpec((None, tq, LANES), lambda b, h, i: (b, i, h)),
            pl.BlockSpec((None, L, LANES), lambda b, h, i: (b, 0, nq + h)),
            pl.BlockSpec((None, L, 2 * LANES), lambda b, h, i: (b, 0, nq + h)),
            pl.BlockSpec((None, 1, LANES), lambda b, h, i: (h, 0, 0)),
            _const_spec((4, LANES)),
            _const_spec((1, A_VDIM)),
        ],
        out_specs=pl.BlockSpec((None, tq, LANES), lambda b, h, i: (b, i, h)),
        out_shape=jax.ShapeDtypeStruct((Bsz, L, A_WIDTH), BF16),
        scratch_shapes=[pltpu.VMEM((2 * tq, 1), F32), pltpu.VMEM((2 * tq, 2 * LANES), F32)],
        compiler_params=_compiler_params(("parallel", "parallel", "arbitrary")),
        name="diff_attn",
    )(qkv, qkv, qkv, slopes, lamv, subw)


def _gqa_attention(qkv, Bsz, L):
    tq, tk = _attn_tiles(L)
    q_blocks = B_HEADS // B_GROUP
    kernel = functools.partial(_gqa_attn_kernel, tq=tq, tk=tk)
    return pl.pallas_call(
        kernel,
        grid=(Bsz, B_KV_HEADS, L // tq),
        in_specs=[
            pl.BlockSpec((None, tq, B_GROUP * LANES), lambda b, g, i: (b, i, g)),
            pl.BlockSpec((None, L, LANES), lambda b, g, i: (b, 0, B_HEADS + g)),
            pl.BlockSpec((None, L, LANES), lambda b, g, i: (b, 0, B_HEADS + B_KV_HEADS + g)),
        ],
        out_specs=pl.BlockSpec((None, tq, B_GROUP * LANES), lambda b, g, i: (b, i, g)),
        out_shape=jax.ShapeDtypeStruct((Bsz, L, q_blocks * B_GROUP * LANES), BF16),
        scratch_shapes=[pltpu.VMEM((B_GROUP * tq, 1), F32), pltpu.VMEM((B_GROUP * tq, LANES), F32)],
        compiler_params=_compiler_params(("parallel", "parallel", "arbitrary")),
        name="gqa_attn",
    )(qkv, qkv, qkv)


def _ssm_kernel(u_ref, m_ref, wst_ref, wout_ref, a_ref, y_ref, s_ref, h_ref, *, n_batch, n_chunks):
    u = u_ref[...]
    y_ref[...] = jnp.dot(u, m_ref[...], preferred_element_type=F32)
    s_ref[...] = jnp.dot(u, wst_ref[...], preferred_element_type=F32)
    a = a_ref[...]
    afr, afi = a[:, 0:LANES], a[:, LANES:2 * LANES]
    abr, abi = a[:, 2 * LANES:3 * LANES], a[:, 3 * LANES:4 * LANES]
    zero = jnp.zeros((1, LANES), F32)

    def step(cg, carry):
        new = []
        for b in range(n_batch):
            hr, hi, gr, gi = carry[4 * b:4 * b + 4]
            rf = pl.multiple_of(b * n_chunks + cg * SUBLANES, SUBLANES)
            rb = pl.multiple_of(b * n_chunks + n_chunks - SUBLANES - cg * SUBLANES, SUBLANES)
            sf = s_ref[pl.ds(rf, SUBLANES), 0:2 * LANES]
            sb = s_ref[pl.ds(rb, SUBLANES), 2 * LANES:4 * LANES]
            rows_f, rows_b = [], [None] * SUBLANES
            for j in range(SUBLANES):
                rows_f.append(jnp.concatenate([hr, hi], axis=1))
                hr, hi = (afr * hr - afi * hi + sf[j:j + 1, 0:LANES],
                          afr * hi + afi * hr + sf[j:j + 1, LANES:2 * LANES])
                jb = SUBLANES - 1 - j
                rows_b[jb] = jnp.concatenate([gr, gi], axis=1)
                gr, gi = (abr * gr - abi * gi + sb[jb:jb + 1, 0:LANES],
                          abr * gi + abi * gr + sb[jb:jb + 1, LANES:2 * LANES])
            h_ref[pl.ds(rf, SUBLANES), 0:2 * LANES] = jnp.concatenate(rows_f, axis=0)
            h_ref[pl.ds(rb, SUBLANES), 2 * LANES:4 * LANES] = jnp.concatenate(rows_b, axis=0)
            new += [hr, hi, gr, gi]
        return tuple(new)

    lax.fori_loop(0, n_chunks // SUBLANES, step, (zero,) * (4 * n_batch))
    y_ref[...] += jnp.dot(h_ref[...].astype(BF16), wout_ref[...], preferred_element_type=F32)


def _ssm_apply(u_g, m, wst, wout, a64, n_batch, n_chunks):
    nc = n_batch * n_chunks
    kernel = functools.partial(_ssm_kernel, n_batch=n_batch, n_chunks=n_chunks)
    return pl.pallas_call(
        kernel,
        grid=(SSM_GROUPS,),
        in_specs=[
            pl.BlockSpec((None, nc, SSM_FLAT), lambda g: (g, 0, 0)),
            pl.BlockSpec((None, SSM_FLAT, SSM_FLAT), lambda g: (g, 0, 0)),
            pl.BlockSpec((None, SSM_FLAT, SSM_STATE_COLS), lambda g: (g, 0, 0)),
            pl.BlockSpec((None, SSM_STATE_COLS, SSM_FLAT), lambda g: (g, 0, 0)),
            pl.BlockSpec((None, 1, SSM_STATE_COLS), lambda g: (g, 0, 0)),
        ],
        out_specs=pl.BlockSpec((None, nc, SSM_FLAT), lambda g: (g, 0, 0)),
        out_shape=jax.ShapeDtypeStruct((SSM_GROUPS, nc, SSM_FLAT), F32),
        scratch_shapes=[pltpu.VMEM((nc, SSM_STATE_COLS), F32), pltpu.VMEM((nc, SSM_STATE_COLS), F32)],
        compiler_params=_compiler_params(("parallel",)),
        name="ssm_chunks",
    )(u_g, m, wst, wout, a64)


def _ssm_operators(A_re, A_im, log_dt, B_re, B_im, C_re, C_im):
    hi = lax.Precision.HIGHEST
    G, N, P, Tc = SSM_GROUPS, SSM_N, SSM_P, SSM_CHUNK
    Br, Bi, Cr, Ci = B_re.astype(F32), B_im.astype(F32), C_re.astype(F32), C_im.astype(F32)
    steps = jnp.arange(Tc + 1, dtype=F32)

    per_dir = []
    for dirn in range(2):
        dt = jnp.exp(log_dt[dirn].astype(F32))[:, None]
        ar, ai = A_re[dirn].astype(F32), A_im[dirn].astype(F32)
        mag = jnp.exp(dt * ar)
        er, ei = mag * jnp.cos(dt * ai), mag * jnp.sin(dt * ai)
        den = ar * ar + ai * ai
        fr = ((er - 1.0) * ar + ei * ai) / den
        fi = (ei * ar - (er - 1.0) * ai) / den
        bbr = fr[..., None] * Br - fi[..., None] * Bi
        bbi = fr[..., None] * Bi + fi[..., None] * Br
        pm = jnp.exp(steps[None, :, None] * (dt * ar)[:, None, :])
        ang = steps[None, :, None] * (dt * ai)[:, None, :]
        pr, pi = pm * jnp.cos(ang), pm * jnp.sin(ang)
        car = Cr[:, None] * pr[:, :, None, :] - Ci[:, None] * pi[:, :, None, :]
        cai = Cr[:, None] * pi[:, :, None, :] + Ci[:, None] * pr[:, :, None, :]
        kern = (jnp.einsum('gkpn,gnq->gkpq', car[:, :Tc], bbr, precision=hi)
                - jnp.einsum('gkpn,gnq->gkpq', cai[:, :Tc], bbi, precision=hi))
        abr = pr[..., None] * bbr[:, None] - pi[..., None] * bbi[:, None]
        abi = pr[..., None] * bbi[:, None] + pi[..., None] * bbr[:, None]
        per_dir.append(dict(kern=kern, car=car, cai=cai, abr=abr, abi=abi, pr=pr, pi=pi))

    f, b = per_dir
    s_idx = jnp.arange(Tc)[:, None]
    t_idx = jnp.arange(Tc)[None, :]
    lag = t_idx - s_idx
    kf = jnp.where((lag >= 0)[None, :, :, None, None], f['kern'][:, jnp.clip(lag, 0, Tc - 1)], 0.0)
    kb = jnp.where((lag <= 0)[None, :, :, None, None], b['kern'][:, jnp.clip(-lag, 0, Tc - 1)], 0.0)
    m = jnp.transpose(kf + kb, (0, 1, 4, 2, 3)).reshape(G, Tc * P, Tc * P)

    pad = jnp.zeros((G, Tc * P, LANES - N), F32)

    def st_cols(xr):
        return jnp.concatenate([jnp.transpose(xr, (0, 1, 3, 2)).reshape(G, Tc * P, N), pad], axis=-1)

    wst = jnp.concatenate([st_cols(f['abr'][:, Tc - 1::-1][:, :Tc]), st_cols(f['abi'][:, Tc - 1::-1][:, :Tc]),
                           st_cols(b['abr'][:, :Tc]), st_cols(b['abi'][:, :Tc])], axis=-1)

    padr = jnp.zeros((G, LANES - N, Tc * P), F32)

    def out_rows(x):
        return jnp.concatenate([jnp.transpose(x, (0, 3, 1, 2)).reshape(G, N, Tc * P), padr], axis=1)

    wout = jnp.concatenate([out_rows(f['car'][:, 1:Tc + 1]), out_rows(-f['cai'][:, 1:Tc + 1]),
                            out_rows(b['car'][:, Tc:0:-1]), out_rows(-b['cai'][:, Tc:0:-1])], axis=1)

    padc = jnp.zeros((G, LANES - N), F32)
    a64 = jnp.concatenate([f['pr'][:, Tc], padc, f['pi'][:, Tc], padc,
                           b['pr'][:, Tc], padc, b['pi'][:, Tc], padc], axis=-1)[:, None, :]
    return m.astype(BF16), wst.astype(BF16), wout.astype(BF16), a64


def _merge_kernel(x_ref, oa_ref, ob_ref, ys_ref, zc_ref, n1_ref, wg_ref, wpa_ref, wpb_ref, wpc_ref,
                  wglu_ref, bglu_ref, dskip_ref, wout_ref, n2_ref, wrg_ref, brg_ref, wre_ref, bre_ref,
                  x1_ref, xn2_ref, comb_ref):
    x = x_ref[...]
    xn = _rms(x, n1_ref[...]).astype(BF16)
    gates = jax.nn.sigmoid(jnp.dot(xn, wg_ref[...], preferred_element_type=F32))

    y = ys_ref[...] + dskip_ref[...] * zc_ref[...]
    g = jax.nn.gelu(y)
    glu = jnp.dot(g.astype(BF16), wglu_ref[...], preferred_element_type=F32) + bglu_ref[...]
    out_c = g * jax.nn.sigmoid(glu)

    merged = (gates[:, 0:D_MODEL] * jnp.dot(oa_ref[...], wpa_ref[...], preferred_element_type=F32)
              + gates[:, D_MODEL:2 * D_MODEL] * jnp.dot(ob_ref[...], wpb_ref[...], preferred_element_type=F32)
              + gates[:, 2 * D_MODEL:3 * D_MODEL]
              * jnp.dot(out_c.astype(BF16), wpc_ref[...], preferred_element_type=F32))
    x1 = x + jnp.dot(merged.astype(BF16), wout_ref[...], preferred_element_type=F32)
    x1_ref[...] = x1

    xn2 = _rms(x1, n2_ref[...])
    xn2_ref[...] = xn2.astype(BF16)

    hi = lax.Precision.HIGHEST
    lane = lax.broadcasted_iota(jnp.int32, (x.shape[0], LANES), 1).astype(F32)
    neg = jnp.float32(-jnp.inf)
    big = jnp.float32(LANES)
    lg = jnp.dot(xn2, wrg_ref[...], preferred_element_type=F32, precision=hi) + brg_ref[...]
    lg = jnp.where(lane < MOE_GROUPS, lg, neg)
    g_max = jnp.max(lg, axis=-1, keepdims=True)
    g_idx = jnp.min(jnp.where(lg == g_max, lane, big), axis=-1, keepdims=True)
    g_w = 1.0 / jnp.sum(jnp.exp(lg - g_max), axis=-1, keepdims=True)
    le = jnp.dot(xn2, wre_ref[...], preferred_element_type=F32, precision=hi) + bre_ref[...]
    in_group = (lane >= g_idx * EXPERTS_PER_GROUP) & (lane < (g_idx + 1) * EXPERTS_PER_GROUP)
    v = jnp.where(in_group, le, neg)
    top1 = jnp.max(v, axis=-1, keepdims=True)
    i1 = jnp.min(jnp.where(v == top1, lane, big), axis=-1, keepdims=True)
    v2 = jnp.where(lane == i1, neg, v)
    top2 = jnp.max(v2, axis=-1, keepdims=True)
    i2 = jnp.min(jnp.where(v2 == top2, lane, big), axis=-1, keepdims=True)
    e2 = jnp.exp(top2 - top1)
    inv = 1.0 / (1.0 + e2)
    comb_ref[...] = (jnp.where(lane == i1, inv * g_w, 0.0) + jnp.where(lane == i2, e2 * inv * g_w, 0.0))


def _merge(x2d, oa, ob, ys, zc, weights):
    T = x2d.shape[0]
    tm = MERGE_TILE

    def rows(width):
        return pl.BlockSpec((tm, width), lambda i: (i, 0))

    w_specs = [pl.BlockSpec(w.shape, lambda i, nd=w.ndim: (0,) * nd, pipeline_mode=pl.Buffered(1)) for w in weights]
    return pl.pallas_call(
        _merge_kernel,
        grid=(T // tm,),
        in_specs=[rows(D_MODEL), rows(A_WIDTH), rows(B_HEADS * LANES), rows(SSM_WIDTH), rows(SSM_WIDTH)] + w_specs,
        out_specs=[rows(D_MODEL), rows(D_MODEL), rows(LANES)],
        out_shape=[jax.ShapeDtypeStruct((T, D_MODEL), F32), jax.ShapeDtypeStruct((T, D_MODEL), BF16),
                   jax.ShapeDtypeStruct((T, LANES), F32)],
        compiler_params=_compiler_params(("parallel",)),
        name="merge_router",
    )(x2d, oa, ob, ys, zc, *weights)


def _moe_kernel(xn_ref, comb_ref, x1_ref, wg_ref, wu_ref, wd_ref, fw_ref, o_ref, acc_ref, *, final_norm):
    e = pl.program_id(1)

    @pl.when(e == 0)
    def _():
        acc_ref[...] = jnp.zeros(acc_ref.shape, F32)

    xn = xn_ref[...]
    h = jax.nn.silu(jnp.dot(xn, wg_ref[...], preferred_element_type=F32)) * jnp.dot(
        xn, wu_ref[...], preferred_element_type=F32)
    y = jnp.dot(h.astype(BF16), wd_ref[...], preferred_element_type=F32)
    comb = comb_ref[...]
    lane = lax.broadcasted_iota(jnp.int32, comb.shape, 1)
    c = jnp.sum(jnp.where(lane == e, comb, 0.0), axis=-1, keepdims=True)
    acc_ref[...] += c * y

    @pl.when(e == N_EXPERTS - 1)
    def _():
        out = x1_ref[...] + acc_ref[...]
        if final_norm:
            out = _rms(out, fw_ref[...])
        o_ref[...] = out


def _moe(xn2, comb, x1, wg, wu, wd, fw, final_norm):
    T = x1.shape[0]
    tm = ROW_TILE
    kernel = functools.partial(_moe_kernel, final_norm=final_norm)
    return pl.pallas_call(
        kernel,
        grid=(T // tm, N_EXPERTS),
        in_specs=[
            pl.BlockSpec((tm, D_MODEL), lambda i, e: (i, 0)),
            pl.BlockSpec((tm, LANES), lambda i, e: (i, 0)),
            pl.BlockSpec((tm, D_MODEL), lambda i, e: (i, 0)),
            pl.BlockSpec((None, D_MODEL, D_FF_EXPERT), lambda i, e: (e, 0, 0)),
            pl.BlockSpec((None, D_MODEL, D_FF_EXPERT), lambda i, e: (e, 0, 0)),
            pl.BlockSpec((None, D_FF_EXPERT, D_MODEL), lambda i, e: (e, 0, 0)),
            _const_spec((1, D_MODEL)),
        ],
        out_specs=pl.BlockSpec((tm, D_MODEL), lambda i, e: (i, 0)),
        out_shape=jax.ShapeDtypeStruct((T, D_MODEL), F32),
        scratch_shapes=[pltpu.VMEM((tm, D_MODEL), F32)],
        compiler_params=_compiler_params(("parallel", "arbitrary")),
        name="experts",
    )(xn2, comb, x1, wg, wu, wd, fw)


def _pad_heads(w, n_heads, width):
    w = w.reshape(D_MODEL, n_heads, -1)
    return jnp.pad(w, ((0, 0), (0, 0), (0, width - w.shape[-1]))).reshape(D_MODEL, n_heads * width)


def _pad_vec(v, width=LANES):
    return jnp.pad(v.astype(F32), (0, width - v.shape[0]))[None, :]


def _layer_params(l, p):
    scale = HEAD_DIM ** -0.5
    w_in = p['w_in'][l]
    wa, wb, wc, wg = jnp.split(w_in, [COLS_A, COLS_A + COLS_B, COLS_A + COLS_B + COLS_C], axis=-1)

    wqa, wka, wva = jnp.split(wa, [A_QK_COLS, 2 * A_QK_COLS], axis=-1)
    wva = _pad_heads(wva, A_HEADS, 2 * LANES)
    w_a = jnp.concatenate([wqa * scale, wka, wva], axis=-1).astype(BF16)
    ones_a = jnp.zeros((A_HEADS, 2 * LANES), F32).at[:, A_VDIM].set(1.0).reshape(1, -1)
    e_a = jnp.concatenate([jnp.zeros((1, 2 * A_QK_COLS), F32), ones_a], axis=-1)

    wqb, wkb, wvb = jnp.split(wb, [B_HEADS * HEAD_DIM, (B_HEADS + B_KV_HEADS) * HEAD_DIM], axis=-1)
    w_b = jnp.concatenate([_pad_heads(wqb, B_HEADS, LANES), _pad_heads(wkb, B_KV_HEADS, LANES),
                           _pad_heads(wvb, B_KV_HEADS, LANES)], axis=-1).astype(BF16)
    e_b = jnp.zeros((B_KV_HEADS, LANES), F32).at[:, HEAD_DIM].set(1.0).reshape(1, -1)

    wpb = p['w_proj_b'][l].reshape(B_HEADS, HEAD_DIM, D_MODEL)
    wpb = jnp.pad(wpb, ((0, 0), (0, LANES - HEAD_DIM), (0, 0))).reshape(B_HEADS * LANES, D_MODEL)

    merge_w = [
        p['norm1_w'][l][None, :], wg.astype(BF16), p['w_proj_a'][l].astype(BF16), wpb.astype(BF16),
        p['w_proj_c'][l].astype(BF16), p['w_glu'][l].astype(BF16), p['b_glu'][l][None, :], p['ssm_D'][l][None, :],
        p['w_out'][l].astype(BF16), p['norm2_w'][l][None, :],
        jnp.pad(p['w_router_group'][l], ((0, 0), (0, LANES - MOE_GROUPS))), _pad_vec(p['b_router_group'][l]),
        jnp.pad(p['w_router_expert'][l], ((0, 0), (0, LANES - N_EXPERTS))), _pad_vec(p['b_router_expert'][l]),
    ]
    lam_init = 0.8 - 0.6 * math.exp(-0.3 * l)
    lamv = jnp.stack([_pad_vec(v[l])[0] for v in (p['lam_q1'], p['lam_k1'], p['lam_q2'], p['lam_k2'])])
    return dict(
        norm1=p['norm1_w'][l][None, :], w_a=w_a, e_a=e_a, w_b=w_b, e_b=e_b, w_c=wc.astype(BF16),
        qw=_pad_vec(p['q_norm_w'][l] * scale), kw=_pad_vec(p['k_norm_w'][l]),
        lamv=lamv, lam_init=lam_init, subw=p['diff_subln_w'][l][None, :],
        ssm=_ssm_operators(p['ssm_A_re'][l], p['ssm_A_im'][l], p['ssm_log_dt'][l], p['ssm_B_re'][l],
                           p['ssm_B_im'][l], p['ssm_C_re'][l], p['ssm_C_im'][l]),
        merge_w=merge_w,
        wg=p['w_exp_gate'][l].astype(BF16), wu=p['w_exp_up'][l].astype(BF16), wd=p['w_exp_down'][l].astype(BF16),
    )


def _rope_tables(L):
    half = HEAD_DIM // 2
    inv = ROPE_BASE ** (-jnp.arange(0, half, 2, dtype=F32) / half)
    t = jnp.arange(L)
    row = (t // GRID_W).astype(F32)
    col = (t % GRID_W).astype(F32)
    ang = jnp.concatenate([row[:, None] * inv[None, :]] * 2 + [col[:, None] * inv[None, :]] * 2, axis=-1)
    cos, sin = jnp.cos(ang), jnp.sin(ang)
    first = (jnp.arange(HEAD_DIM) % half) < (half // 2)
    pad = ((0, 0), (0, LANES - HEAD_DIM))
    return (jnp.pad(cos, pad), jnp.pad(jnp.where(first, -sin, 0.0), pad), jnp.pad(jnp.where(first, 0.0, sin), pad))


def _alibi_slopes():
    s = 2.0 ** (-8.0 * jnp.arange(1, A_HEADS + 1, dtype=F32) / A_HEADS)
    return jnp.broadcast_to(s[:, None, None], (A_HEADS, 1, LANES))


def _trunk(x, layers, final_norm_w):
    Bsz, L, _ = x.shape
    T = Bsz * L
    tm = ROW_TILE
    n_chunks = L // SSM_CHUNK
    pos_blocks = L // tm
    cos, sa, sb = _rope_tables(L)
    slopes = _alibi_slopes()
    fw = final_norm_w[None, :]
    x2d = x.reshape(T, D_MODEL)

    for l, lp in enumerate(layers):
        na = lp['w_a'].shape[1]
        tn = 512
        qkv_a = _proj_call(
            "proj_a", _proj_a_kernel, x2d, lp['norm1'], lp['w_a'], [lp['e_a']],
            [pl.BlockSpec((1, tn), lambda i, j: (0, j))],
            jax.ShapeDtypeStruct((T, na), BF16), pl.BlockSpec((tm, tn), lambda i, j: (i, j)), tn)
        out_a = _diff_attention(qkv_a.reshape(Bsz, L, na), slopes, lp['lamv'], lp['subw'], lp['lam_init'], Bsz, L)

        nb = lp['w_b'].shape[1]
        heads_per_tile = tn // LANES
        n_q_tiles = B_HEADS // heads_per_tile
        tab = pl.BlockSpec((tm, LANES), lambda i, j: (i % pos_blocks, 0))
        qkv_b = _proj_call(
            "proj_b", functools.partial(_proj_b_kernel, n_q_tiles=n_q_tiles, heads_per_tile=heads_per_tile),
            x2d, lp['norm1'], lp['w_b'], [lp['qw'], lp['kw'], lp['e_b'], cos, sa, sb],
            [_const_spec((1, LANES)), _const_spec((1, LANES)), _const_spec((1, B_KV_HEADS * LANES)), tab, tab, tab],
            jax.ShapeDtypeStruct((T, nb), BF16), pl.BlockSpec((tm, tn), lambda i, j: (i, j)), tn)
        out_b = _gqa_attention(qkv_b.reshape(Bsz, L, nb), Bsz, L)

        zc32, zc16 = _proj_call(
            "proj_c", _proj_c_kernel, x2d, lp['norm1'], lp['w_c'], [], [],
            [jax.ShapeDtypeStruct((T, COLS_C), F32), jax.ShapeDtypeStruct((T, COLS_C), BF16)],
            [pl.BlockSpec((tm, COLS_C), lambda i, j: (i, j))] * 2, COLS_C)
        u_g = jnp.transpose(zc16.reshape(Bsz, n_chunks, SSM_CHUNK, SSM_GROUPS, SSM_P), (3, 0, 1, 2, 4))
        u_g = u_g.reshape(SSM_GROUPS, Bsz * n_chunks, SSM_FLAT)
        y_g = _ssm_apply(u_g, *lp['ssm'], Bsz, n_chunks)
        y_s = jnp.transpose(y_g.reshape(SSM_GROUPS, Bsz, n_chunks, SSM_CHUNK, SSM_P), (1, 2, 3, 0, 4))
        y_s = y_s.reshape(T, SSM_WIDTH)

        x1, xn2, comb = _merge(x2d, out_a.reshape(T, A_WIDTH), out_b.reshape(T, B_HEADS * LANES), y_s, zc32,
                               lp['merge_w'])
        x2d = _moe(xn2, comb, x1, lp['wg'], lp['wu'], lp['wd'], fw, final_norm=(l == len(layers) - 1))
    return x2d.reshape(Bsz, L, D_MODEL)


def kernel(x_prompt, x_sample, norm1_w, w_in, lam_q1, lam_k1, lam_q2, lam_k2, diff_subln_w, q_norm_w, k_norm_w,
           ssm_A_re, ssm_A_im, ssm_log_dt, ssm_B_re, ssm_B_im, ssm_C_re, ssm_C_im, ssm_D, w_glu, b_glu,
           w_proj_a, w_proj_b, w_proj_c, w_out, norm2_w, w_router_group, b_router_group, w_router_expert,
           b_router_expert, w_exp_gate, w_exp_up, w_exp_down, final_norm_w):
    p = dict(norm1_w=norm1_w, w_in=w_in, lam_q1=lam_q1, lam_k1=lam_k1, lam_q2=lam_q2, lam_k2=lam_k2,
             diff_subln_w=diff_subln_w, q_norm_w=q_norm_w, k_norm_w=k_norm_w, ssm_A_re=ssm_A_re, ssm_A_im=ssm_A_im,
             ssm_log_dt=ssm_log_dt, ssm_B_re=ssm_B_re, ssm_B_im=ssm_B_im, ssm_C_re=ssm_C_re, ssm_C_im=ssm_C_im,
             ssm_D=ssm_D, w_glu=w_glu, b_glu=b_glu, w_proj_a=w_proj_a, w_proj_b=w_proj_b, w_proj_c=w_proj_c,
             w_out=w_out, norm2_w=norm2_w, w_router_group=w_router_group, b_router_group=b_router_group,
             w_router_expert=w_router_expert, b_router_expert=b_router_expert, w_exp_gate=w_exp_gate,
             w_exp_up=w_exp_up, w_exp_down=w_exp_down)
    layers = [_layer_params(l, p) for l in range(DEPTH)]
    return (_trunk(x_prompt, layers, final_norm_w), _trunk(x_sample, layers, final_norm_w))
```

```python
import functools
import math

import jax
import jax.numpy as jnp
from jax import lax
from jax.experimental import pallas as pl
from jax.experimental.pallas import tpu as pltpu

F32 = jnp.float32
BF16 = jnp.bfloat16

D_MODEL = 1024
DEPTH = 2
HEAD_DIM = 64
EPS = 1e-6
A_HEADS = 4
A_VDIM = 2 * HEAD_DIM
A_QK_COLS = A_HEADS * 2 * HEAD_DIM
A_WIDTH = A_HEADS * A_VDIM
COLS_A = 2 * A_QK_COLS + A_WIDTH
B_HEADS = 8
B_KV_HEADS = 2
B_GROUP = B_HEADS // B_KV_HEADS
B_WIDTH = B_HEADS * HEAD_DIM
COLS_B = B_HEADS * HEAD_DIM + 2 * B_KV_HEADS * HEAD_DIM
ROPE_BASE = 10000.0
GRID_W = 64
SSM_WIDTH = 512
SSM_P = 16
SSM_GROUPS = SSM_WIDTH // SSM_P
SSM_N = 64
COLS_C = SSM_WIDTH
N_BRANCH = 3
MOE_GROUPS = 4
EXPERTS_PER_GROUP = 4
N_EXPERTS = MOE_GROUPS * EXPERTS_PER_GROUP
D_FF_EXPERT = 512

LANES = 128
SUBLANES = 8
VMEM_LIMIT_BYTES = 56 * 1024 * 1024

SSM_CHUNK = 64
SSM_FLAT = SSM_CHUNK * SSM_P
SSM_STATE_COLS = 4 * LANES

ROW_TILE = 512
MERGE_TILE = 256
ATT_ROWS = 1024
ATT_KV_TILE = 1024
MAX_STATIC_SHIFT = 40.0
SHIFT_MARGIN = 1.02


def _compiler_params(semantics):
    return pltpu.CompilerParams(dimension_semantics=semantics, vmem_limit_bytes=VMEM_LIMIT_BYTES)


def _const_spec(shape):
    zeros = (0,) * len(shape)
    return pl.BlockSpec(shape, lambda *_: zeros)


def _rms(x, w):
    ms = jnp.mean(x * x, axis=-1, keepdims=True)
    return x * lax.rsqrt(ms + EPS) * w


def _normed_rows(x_ref, nw_ref, xn_ref):
    @pl.when(pl.program_id(1) == 0)
    def _():
        xn_ref[...] = _rms(x_ref[...], nw_ref[...]).astype(BF16)


def _proj_a_kernel(x_ref, nw_ref, w_ref, e_ref, o_ref, xn_ref):
    _normed_rows(x_ref, nw_ref, xn_ref)
    z = jnp.dot(xn_ref[...], w_ref[...], preferred_element_type=F32)
    o_ref[...] = (z + e_ref[...]).astype(o_ref.dtype)


def _proj_c_kernel(x_ref, nw_ref, w_ref, o32_ref, o16_ref, xn_ref):
    _normed_rows(x_ref, nw_ref, xn_ref)
    z = jnp.dot(xn_ref[...], w_ref[...], preferred_element_type=F32)
    o32_ref[...] = z
    o16_ref[...] = z.astype(BF16)


def _proj_b_kernel(x_ref, nw_ref, w_ref, qw_ref, kw_ref, eq_ref, ek_ref, e_ref, cos_ref, sa_ref, sb_ref, o_ref,
                   xn_ref, *, n_q_tiles, heads_per_tile):
    _normed_rows(x_ref, nw_ref, xn_ref)
    z = jnp.dot(xn_ref[...], w_ref[...], preferred_element_type=F32)
    cos, sa, sb = cos_ref[...], sa_ref[...], sb_ref[...]
    j = pl.program_id(1)

    def norm_rope(zh, w):
        ms = jnp.sum(zh * zh, axis=-1, keepdims=True) * (1.0 / HEAD_DIM)
        y = zh * lax.rsqrt(ms + EPS) * w
        return y * cos + pltpu.roll(y, LANES - 16, 1) * sa + pltpu.roll(y, 16, 1) * sb

    @pl.when(j < n_q_tiles)
    def _():
        for h in range(heads_per_tile):
            sl = slice(h * LANES, (h + 1) * LANES)
            o_ref[:, sl] = (norm_rope(z[:, sl], qw_ref[...]) + eq_ref[...]).astype(o_ref.dtype)

    @pl.when(j == n_q_tiles)
    def _():
        for h in range(B_KV_HEADS):
            sl = slice(h * LANES, (h + 1) * LANES)
            o_ref[:, sl] = (norm_rope(z[:, sl], kw_ref[...]) + ek_ref[...]).astype(o_ref.dtype)
        sl = slice(B_KV_HEADS * LANES, 2 * B_KV_HEADS * LANES)
        o_ref[:, sl] = (z[:, sl] + e_ref[...]).astype(o_ref.dtype)


def _proj_call(name, kernel, x2d, nw, w, extra_inputs, extra_specs, out_shapes, out_specs, tn):
    T = x2d.shape[0]
    tm = ROW_TILE
    n = w.shape[1]
    grid = (T // tm, n // tn)
    in_specs = [pl.BlockSpec((tm, D_MODEL), lambda i, j: (i, 0)),
                _const_spec((1, D_MODEL)),
                pl.BlockSpec((D_MODEL, tn), lambda i, j: (0, j))] + extra_specs
    return pl.pallas_call(
        kernel,
        grid=grid,
        in_specs=in_specs,
        out_specs=out_specs,
        out_shape=out_shapes,
        scratch_shapes=[pltpu.VMEM((tm, D_MODEL), BF16)],
        compiler_params=_compiler_params(("parallel", "arbitrary")),
        name=name,
    )(x2d, nw, w, *extra_inputs)


_NT = (((1,), (1,)), ((), ()))


def _online_softmax(qs, k_fn, v_ref, m_ref, acc_ref, tk, n_chunks, bias_fn):
    m_ref[...] = jnp.full(m_ref.shape, -jnp.inf, F32)
    acc_ref[...] = jnp.zeros(acc_ref.shape, F32)

    def body(c, carry):
        start = pl.multiple_of(c * tk, tk)
        s = lax.dot_general(qs, k_fn(start), _NT, preferred_element_type=F32)
        if bias_fn is not None:
            s = s + bias_fn(start)
        m_prev = m_ref[...]
        m_new = jnp.maximum(m_prev, jnp.max(s, axis=-1, keepdims=True))
        p = jnp.exp(s - m_new)
        alpha = jnp.exp(m_prev - m_new)
        pv = jnp.dot(p.astype(BF16), v_ref[pl.ds(start, tk), :], preferred_element_type=F32)
        acc_ref[...] = alpha * acc_ref[...] + pv
        m_ref[...] = m_new
        return carry

    lax.fori_loop(0, n_chunks, body, 0)


def _shifted_softmax(qs, k_fn, v_ref, acc_ref, tk, lo, hi, bias_fn=None):
    def body(c, carry):
        start = pl.multiple_of(c * tk, tk)
        s = lax.dot_general(qs, k_fn(start), _NT, preferred_element_type=F32)
        if bias_fn is not None:
            s = s + bias_fn(start)
        acc_ref[...] += jnp.dot(jnp.exp(s).astype(BF16), v_ref[pl.ds(start, tk), :], preferred_element_type=F32)
        return carry

    lax.fori_loop(lo, hi, body, 0)


def _half_norms(x, lane):
    sq = x.astype(F32)
    sq = sq * sq
    n1 = jnp.sum(jnp.where(lane < HEAD_DIM, sq, 0.0), axis=-1, keepdims=True)
    n2 = jnp.sum(jnp.where(lane >= HEAD_DIM, sq, 0.0), axis=-1, keepdims=True)
    return jnp.sqrt(n1), jnp.sqrt(n2)


def _diff_attn_kernel(q_ref, k_ref, v_ref, slope_ref, lam_ref, subw_ref, o_ref, kmax_ref, kaug_ref, m_ref, acc_ref,
                      *, tq, tk, lam_init):
    i = pl.program_id(2)
    n_chunks = k_ref.shape[0] // tk
    slope = slope_ref[...][:, :1]
    lane_k = lax.broadcasted_iota(jnp.int32, (tk, LANES), 1)

    @pl.when(i == 0)
    def _():
        def body(c, carry):
            start = pl.multiple_of(c * tk, tk)
            n1, n2 = _half_norms(k_ref[pl.ds(start, tk), :], lane_k)
            k_pos = start + lax.broadcasted_iota(jnp.int32, (tk, LANES), 0)
            hi = (k_pos >> 7).astype(F32) * (slope * float(LANES))
            lo = (k_pos & (LANES - 1)).astype(F32) * slope
            aug = jnp.where(lane_k < 3, 1.0, jnp.where(lane_k == 3, hi, jnp.where(lane_k == 4, lo, 0.0)))
            kaug_ref[pl.ds(start, tk), :] = aug.astype(BF16)
            return (jnp.maximum(carry[0], jnp.max(n1, axis=0, keepdims=True)),
                    jnp.maximum(carry[1], jnp.max(n2, axis=0, keepdims=True)))

        zero11 = jnp.zeros((1, 1), F32)
        k1, k2 = lax.fori_loop(0, n_chunks, body, (zero11, zero11))
        kmax_ref[0:1, :] = jnp.broadcast_to(k1, (1, LANES))
        kmax_ref[1:2, :] = jnp.broadcast_to(k2, (1, LANES))

    q = q_ref[...]
    lane = lax.broadcasted_iota(jnp.int32, q.shape, 1)
    zero = jnp.zeros_like(q)
    qz = jnp.concatenate([jnp.where(lane < HEAD_DIM, q, zero), jnp.where(lane >= HEAD_DIM, q, zero)], axis=0)

    qn1, qn2 = _half_norms(q, lane)
    shift = jnp.concatenate([qn1 * kmax_ref[0:1, 0:1], qn2 * kmax_ref[1:2, 0:1]], axis=0) * SHIFT_MARGIN
    static_ok = jnp.max(shift) <= MAX_STATIC_SHIFT

    row = lax.broadcasted_iota(jnp.int32, (2 * tq, 1), 0)
    q_pos = i * tq + jnp.where(row >= tq, row - tq, row)
    q_pos_f = q_pos.astype(F32)
    lane2 = lax.broadcasted_iota(jnp.int32, (2 * tq, LANES), 1)
    shift_cols = jnp.where(lane2 == 0, -shift, 0.0)
    pos_cols = jnp.where(lane2 == 1, (q_pos >> 7).astype(F32) * (-slope * float(LANES)),
                         jnp.where(lane2 == 2, (q_pos & (LANES - 1)).astype(F32) * (-slope),
                                   jnp.where((lane2 == 3) | (lane2 == 4), 1.0, 0.0)))
    q_left = jnp.concatenate([qz, (shift_cols + pos_cols).astype(BF16)], axis=1)
    q_right = jnp.concatenate([qz, (shift_cols - pos_cols).astype(BF16)], axis=1)
    q_diag = jnp.concatenate([qz, shift_cols.astype(BF16)], axis=1)

    def k_fn(start):
        return jnp.concatenate([k_ref[pl.ds(start, tk), :], kaug_ref[pl.ds(start, tk), :]], axis=1)

    def bias_fn(start):
        k_pos = (start + lax.broadcasted_iota(jnp.int32, (1, tk), 1)).astype(F32)
        return -slope * jnp.abs(q_pos_f - k_pos)

    @pl.when(static_ok)
    def _():
        acc_ref[...] = jnp.zeros(acc_ref.shape, F32)
        c_diag = (i * tq) // tk
        _shifted_softmax(q_left, k_fn, v_ref, acc_ref, tk, 0, c_diag)
        _shifted_softmax(q_diag, k_fn, v_ref, acc_ref, tk, c_diag, c_diag + 1, bias_fn)
        _shifted_softmax(q_right, k_fn, v_ref, acc_ref, tk, c_diag + 1, n_chunks)

    @pl.when(jnp.logical_not(static_ok))
    def _():
        _online_softmax(q_diag, k_fn, v_ref, m_ref, acc_ref, tk, n_chunks, bias_fn)

    acc = acc_ref[...]
    o = acc[:, :A_VDIM] / acc[:, A_VDIM:A_VDIM + 1]
    lv = lam_ref[...]
    lam = (jnp.exp(jnp.sum(lv[0:1] * lv[1:2], axis=-1, keepdims=True))
           - jnp.exp(jnp.sum(lv[2:3] * lv[3:4], axis=-1, keepdims=True)) + lam_init)
    d = o[:tq] - lam * o[tq:]
    o_ref[...] = (_rms(d, subw_ref[...]) * (1.0 - lam_init)).astype(o_ref.dtype)


def _gqa_attn_kernel(shift_ref, q_ref, k_ref, v_ref, o_ref, m_ref, acc_ref, *, tq, tk):
    qs = jnp.concatenate([q_ref[:, h * LANES:(h + 1) * LANES] for h in range(B_GROUP)], axis=0)
    n_chunks = k_ref.shape[0] // tk
    static_ok = shift_ref[0] <= MAX_STATIC_SHIFT

    def k_fn(start):
        return k_ref[pl.ds(start, tk), :]

    @pl.when(static_ok)
    def _():
        acc_ref[...] = jnp.zeros(acc_ref.shape, F32)
        _shifted_softmax(qs, k_fn, v_ref, acc_ref, tk, 0, n_chunks)

    @pl.when(jnp.logical_not(static_ok))
    def _():
        _online_softmax(qs, k_fn, v_ref, m_ref, acc_ref, tk, n_chunks, None)

    acc = acc_ref[...]
    o = acc / acc[:, HEAD_DIM:HEAD_DIM + 1]
    for h in range(B_GROUP):
        o_ref[:, h * LANES:(h + 1) * LANES] = o[h * tq:(h + 1) * tq].astype(o_ref.dtype)


def _attn_tiles(L, stacked):
    return min(ATT_ROWS // stacked, L), min(ATT_KV_TILE, L)


def _resident(block_shape, index_map):
    return pl.BlockSpec(block_shape, index_map, pipeline_mode=pl.Buffered(1))


def _diff_attention(qkv, slopes, lamv, subw, lam_init, Bsz, L):
    tq, tk = _attn_tiles(L, 2)
    nq = A_HEADS
    kernel = functools.partial(_diff_attn_kernel, tq=tq, tk=tk, lam_init=lam_init)
    return pl.pallas_call(
        kernel,
        grid=(Bsz, A_HEADS, L // tq),
        in_specs=[
            pl.BlockSpec((None, tq, LANES), lambda b, h, i: (b, i, h)),
            _resident((None, L, LANES), lambda b, h, i: (b, 0, nq + h)),
            _resident((None, L, 2 * LANES), lambda b, h, i: (b, 0, nq + h)),
            pl.BlockSpec((None, 1, LANES), lambda b, h, i: (h, 0, 0)),
            _const_spec((4, LANES)),
            _const_spec((1, A_VDIM)),
        ],
        out_specs=pl.BlockSpec((None, tq, LANES), lambda b, h, i: (b, i, h)),
        out_shape=jax.ShapeDtypeStruct((Bsz, L, A_WIDTH), BF16),
        scratch_shapes=[pltpu.VMEM((SUBLANES, LANES), F32), pltpu.VMEM((L, LANES), BF16),
                        pltpu.VMEM((2 * tq, 1), F32), pltpu.VMEM((2 * tq, 2 * LANES), F32)],
        compiler_params=_compiler_params(("parallel", "parallel", "arbitrary")),
        name="diff_attn",
    )(qkv, qkv, qkv, slopes, lamv, subw)


def _gqa_attention(qkv, shift, Bsz, L):
    tq, tk = _attn_tiles(L, B_GROUP)
    q_blocks = B_HEADS // B_GROUP
    kernel = functools.partial(_gqa_attn_kernel, tq=tq, tk=tk)
    return pl.pallas_call(
        kernel,
        grid=(Bsz, B_KV_HEADS, L // tq),
        in_specs=[
            pl.BlockSpec(memory_space=pltpu.SMEM),
            pl.BlockSpec((None, tq, B_GROUP * LANES), lambda b, g, i: (b, i, g)),
            _resident((None, L, LANES), lambda b, g, i: (b, 0, B_HEADS + g)),
            _resident((None, L, LANES), lambda b, g, i: (b, 0, B_HEADS + B_KV_HEADS + g)),
        ],
        out_specs=pl.BlockSpec((None, tq, B_GROUP * LANES), lambda b, g, i: (b, i, g)),
        out_shape=jax.ShapeDtypeStruct((Bsz, L, q_blocks * B_GROUP * LANES), BF16),
        scratch_shapes=[pltpu.VMEM((B_GROUP * tq, 1), F32), pltpu.VMEM((B_GROUP * tq, LANES), F32)],
        compiler_params=_compiler_params(("parallel", "parallel", "arbitrary")),
        name="gqa_attn",
    )(shift, qkv, qkv, qkv)


def _ssm_kernel(u_ref, m_ref, wst_ref, wout_ref, a_ref, y_ref, s_ref, h_ref, *, n_batch, n_chunks):
    u = u_ref[...]
    y_ref[...] = jnp.dot(u, m_ref[...], preferred_element_type=F32)
    s_ref[...] = jnp.dot(u, wst_ref[...], preferred_element_type=F32)
    a = a_ref[...]
    afr, afi = a[:, 0:LANES], a[:, LANES:2 * LANES]
    abr, abi = a[:, 2 * LANES:3 * LANES], a[:, 3 * LANES:4 * LANES]
    zero = jnp.zeros((1, LANES), F32)

    def step(cg, carry):
        new = []
        for b in range(n_batch):
            hr, hi, gr, gi = carry[4 * b:4 * b + 4]
            rf = pl.multiple_of(b * n_chunks + cg * SUBLANES, SUBLANES)
            rb = pl.multiple_of(b * n_chunks + n_chunks - SUBLANES - cg * SUBLANES, SUBLANES)
            sf = s_ref[pl.ds(rf, SUBLANES), 0:2 * LANES]
            sb = s_ref[pl.ds(rb, SUBLANES), 2 * LANES:4 * LANES]
            rows_f, rows_b = [], [None] * SUBLANES
            for j in range(SUBLANES):
                rows_f.append(jnp.concatenate([hr, hi], axis=1))
                hr, hi = (afr * hr - afi * hi + sf[j:j + 1, 0:LANES],
                          afr * hi + afi * hr + sf[j:j + 1, LANES:2 * LANES])
                jb = SUBLANES - 1 - j
                rows_b[jb] = jnp.concatenate([gr, gi], axis=1)
                gr, gi = (abr * gr - abi * gi + sb[jb:jb + 1, 0:LANES],
                          abr * gi + abi * gr + sb[jb:jb + 1, LANES:2 * LANES])
            h_ref[pl.ds(rf, SUBLANES), 0:2 * LANES] = jnp.concatenate(rows_f, axis=0)
            h_ref[pl.ds(rb, SUBLANES), 2 * LANES:4 * LANES] = jnp.concatenate(rows_b, axis=0)
            new += [hr, hi, gr, gi]
        return tuple(new)

    lax.fori_loop(0, n_chunks // SUBLANES, step, (zero,) * (4 * n_batch))
    y_ref[...] += jnp.dot(h_ref[...].astype(BF16), wout_ref[...], preferred_element_type=F32)


def _ssm_apply(u_g, m, wst, wout, a64, n_batch, n_chunks):
    nc = n_batch * n_chunks
    kernel = functools.partial(_ssm_kernel, n_batch=n_batch, n_chunks=n_chunks)
    return pl.pallas_call(
        kernel,
        grid=(SSM_GROUPS,),
        in_specs=[
            pl.BlockSpec((None, nc, SSM_FLAT), lambda g: (g, 0, 0)),
            pl.BlockSpec((None, SSM_FLAT, SSM_FLAT), lambda g: (g, 0, 0)),
            pl.BlockSpec((None, SSM_FLAT, SSM_STATE_COLS), lambda g: (g, 0, 0)),
            pl.BlockSpec((None, SSM_STATE_COLS, SSM_FLAT), lambda g: (g, 0, 0)),
            pl.BlockSpec((None, 1, SSM_STATE_COLS), lambda g: (g, 0, 0)),
        ],
        out_specs=pl.BlockSpec((None, nc, SSM_FLAT), lambda g: (g, 0, 0)),
        out_shape=jax.ShapeDtypeStruct((SSM_GROUPS, nc, SSM_FLAT), F32),
        scratch_shapes=[pltpu.VMEM((nc, SSM_STATE_COLS), F32), pltpu.VMEM((nc, SSM_STATE_COLS), F32)],
        compiler_params=_compiler_params(("parallel",)),
        name="ssm_chunks",
    )(u_g, m, wst, wout, a64)


def _ssm_operators(A_re, A_im, log_dt, B_re, B_im, C_re, C_im):
    hi = lax.Precision.HIGHEST
    G, N, P, Tc = SSM_GROUPS, SSM_N, SSM_P, SSM_CHUNK
    Br, Bi, Cr, Ci = B_re.astype(F32), B_im.astype(F32), C_re.astype(F32), C_im.astype(F32)
    steps = jnp.arange(Tc + 1, dtype=F32)

    per_dir = []
    for dirn in range(2):
        dt = jnp.exp(log_dt[dirn].astype(F32))[:, None]
        ar, ai = A_re[dirn].astype(F32), A_im[dirn].astype(F32)
        mag = jnp.exp(dt * ar)
        er, ei = mag * jnp.cos(dt * ai), mag * jnp.sin(dt * ai)
        den = ar * ar + ai * ai
        fr = ((er - 1.0) * ar + ei * ai) / den
        fi = (ei * ar - (er - 1.0) * ai) / den
        bbr = fr[..., None] * Br - fi[..., None] * Bi
        bbi = fr[..., None] * Bi + fi[..., None] * Br
        pm = jnp.exp(steps[None, :, None] * (dt * ar)[:, None, :])
        ang = steps[None, :, None] * (dt * ai)[:, None, :]
        pr, pi = pm * jnp.cos(ang), pm * jnp.sin(ang)
        car = Cr[:, None] * pr[:, :, None, :] - Ci[:, None] * pi[:, :, None, :]
        cai = Cr[:, None] * pi[:, :, None, :] + Ci[:, None] * pr[:, :, None, :]
        kern = (jnp.einsum('gkpn,gnq->gkpq', car[:, :Tc], bbr, precision=hi)
                - jnp.einsum('gkpn,gnq->gkpq', cai[:, :Tc], bbi, precision=hi))
        abr = pr[..., None] * bbr[:, None] - pi[..., None] * bbi[:, None]
        abi = pr[..., None] * bbi[:, None] + pi[..., None] * bbr[:, None]
        per_dir.append(dict(kern=kern, car=car, cai=cai, abr=abr, abi=abi, pr=pr, pi=pi))

    f, b = per_dir
    s_idx = jnp.arange(Tc)[:, None]
    t_idx = jnp.arange(Tc)[None, :]
    lag = t_idx - s_idx
    kf = jnp.where((lag >= 0)[None, :, :, None, None], f['kern'][:, jnp.clip(lag, 0, Tc - 1)], 0.0)
    kb = jnp.where((lag <= 0)[None, :, :, None, None], b['kern'][:, jnp.clip(-lag, 0, Tc - 1)], 0.0)
    m = jnp.transpose(kf + kb, (0, 1, 4, 2, 3)).reshape(G, Tc * P, Tc * P)

    pad = jnp.zeros((G, Tc * P, LANES - N), F32)

    def st_cols(xr):
        return jnp.concatenate([jnp.transpose(xr, (0, 1, 3, 2)).reshape(G, Tc * P, N), pad], axis=-1)

    wst = jnp.concatenate([st_cols(f['abr'][:, Tc - 1::-1][:, :Tc]), st_cols(f['abi'][:, Tc - 1::-1][:, :Tc]),
                           st_cols(b['abr'][:, :Tc]), st_cols(b['abi'][:, :Tc])], axis=-1)

    padr = jnp.zeros((G, LANES - N, Tc * P), F32)

    def out_rows(x):
        return jnp.concatenate([jnp.transpose(x, (0, 3, 1, 2)).reshape(G, N, Tc * P), padr], axis=1)

    wout = jnp.concatenate([out_rows(f['car'][:, 1:Tc + 1]), out_rows(-f['cai'][:, 1:Tc + 1]),
                            out_rows(b['car'][:, Tc:0:-1]), out_rows(-b['cai'][:, Tc:0:-1])], axis=1)

    padc = jnp.zeros((G, LANES - N), F32)
    a64 = jnp.concatenate([f['pr'][:, Tc], padc, f['pi'][:, Tc], padc,
                           b['pr'][:, Tc], padc, b['pi'][:, Tc], padc], axis=-1)[:, None, :]
    return m.astype(BF16), wst.astype(BF16), wout.astype(BF16), a64


def _merge_kernel(x_ref, oa_ref, ob_ref, ys_ref, zc_ref, n1_ref, wg_ref, wpa_ref, wpb_ref, wpc_ref,
                  wglu_ref, bglu_ref, dskip_ref, wout_ref, n2_ref, wrg_ref, brg_ref, wre_ref, bre_ref,
                  x1_ref, xn2_ref, comb_ref):
    x = x_ref[...]
    xn = _rms(x, n1_ref[...]).astype(BF16)
    gates = jax.nn.sigmoid(jnp.dot(xn, wg_ref[...], preferred_element_type=F32))

    y = ys_ref[...] + dskip_ref[...] * zc_ref[...]
    g = jax.nn.gelu(y)
    glu = jnp.dot(g.astype(BF16), wglu_ref[...], preferred_element_type=F32) + bglu_ref[...]
    out_c = g * jax.nn.sigmoid(glu)

    merged = (gates[:, 0:D_MODEL] * jnp.dot(oa_ref[...], wpa_ref[...], preferred_element_type=F32)
              + gates[:, D_MODEL:2 * D_MODEL] * jnp.dot(ob_ref[...], wpb_ref[...], preferred_element_type=F32)
              + gates[:, 2 * D_MODEL:3 * D_MODEL]
              * jnp.dot(out_c.astype(BF16), wpc_ref[...], preferred_element_type=F32))
    x1 = x + jnp.dot(merged.astype(BF16), wout_ref[...], preferred_element_type=F32)
    x1_ref[...] = x1

    xn2 = _rms(x1, n2_ref[...])
    xn2_ref[...] = xn2.astype(BF16)

    hi = lax.Precision.HIGHEST
    lane = lax.broadcasted_iota(jnp.int32, (x.shape[0], LANES), 1).astype(F32)
    neg = jnp.float32(-jnp.inf)
    big = jnp.float32(LANES)
    lg = jnp.dot(xn2, wrg_ref[...], preferred_element_type=F32, precision=hi) + brg_ref[...]
    lg = jnp.where(lane < MOE_GROUPS, lg, neg)
    g_max = jnp.max(lg, axis=-1, keepdims=True)
    g_idx = jnp.min(jnp.where(lg == g_max, lane, big), axis=-1, keepdims=True)
    g_w = 1.0 / jnp.sum(jnp.exp(lg - g_max), axis=-1, keepdims=True)
    le = jnp.dot(xn2, wre_ref[...], preferred_element_type=F32, precision=hi) + bre_ref[...]
    in_group = (lane >= g_idx * EXPERTS_PER_GROUP) & (lane < (g_idx + 1) * EXPERTS_PER_GROUP)
    v = jnp.where(in_group, le, neg)
    top1 = jnp.max(v, axis=-1, keepdims=True)
    i1 = jnp.min(jnp.where(v == top1, lane, big), axis=-1, keepdims=True)
    v2 = jnp.where(lane == i1, neg, v)
    top2 = jnp.max(v2, axis=-1, keepdims=True)
    i2 = jnp.min(jnp.where(v2 == top2, lane, big), axis=-1, keepdims=True)
    e2 = jnp.exp(top2 - top1)
    inv = 1.0 / (1.0 + e2)
    comb_ref[...] = (jnp.where(lane == i1, inv * g_w, 0.0) + jnp.where(lane == i2, e2 * inv * g_w, 0.0))


def _merge(x2d, oa, ob, ys, zc, weights):
    T = x2d.shape[0]
    tm = MERGE_TILE

    def rows(width):
        return pl.BlockSpec((tm, width), lambda i: (i, 0))

    w_specs = [pl.BlockSpec(w.shape, lambda i, nd=w.ndim: (0,) * nd, pipeline_mode=pl.Buffered(1)) for w in weights]
    return pl.pallas_call(
        _merge_kernel,
        grid=(T // tm,),
        in_specs=[rows(D_MODEL), rows(A_WIDTH), rows(B_HEADS * LANES), rows(SSM_WIDTH), rows(SSM_WIDTH)] + w_specs,
        out_specs=[rows(D_MODEL), rows(D_MODEL), rows(LANES)],
        out_shape=[jax.ShapeDtypeStruct((T, D_MODEL), F32), jax.ShapeDtypeStruct((T, D_MODEL), BF16),
                   jax.ShapeDtypeStruct((T, LANES), F32)],
        compiler_params=_compiler_params(("parallel",)),
        name="merge_router",
    )(x2d, oa, ob, ys, zc, *weights)


def _moe_kernel(xn_ref, comb_ref, x1_ref, wg_ref, wu_ref, wd_ref, fw_ref, o_ref, acc_ref, *, final_norm):
    e = pl.program_id(1)

    @pl.when(e == 0)
    def _():
        acc_ref[...] = jnp.zeros(acc_ref.shape, F32)

    xn = xn_ref[...]
    h = jax.nn.silu(jnp.dot(xn, wg_ref[...], preferred_element_type=F32)) * jnp.dot(
        xn, wu_ref[...], preferred_element_type=F32)
    y = jnp.dot(h.astype(BF16), wd_ref[...], preferred_element_type=F32)
    comb = comb_ref[...]
    lane = lax.broadcasted_iota(jnp.int32, comb.shape, 1)
    c = jnp.sum(jnp.where(lane == e, comb, 0.0), axis=-1, keepdims=True)
    acc_ref[...] += c * y

    @pl.when(e == N_EXPERTS - 1)
    def _():
        out = x1_ref[...] + acc_ref[...]
        if final_norm:
            out = _rms(out, fw_ref[...])
        o_ref[...] = out


def _moe(xn2, comb, x1, wg, wu, wd, fw, final_norm):
    T = x1.shape[0]
    tm = ROW_TILE
    kernel = functools.partial(_moe_kernel, final_norm=final_norm)
    return pl.pallas_call(
        kernel,
        grid=(T // tm, N_EXPERTS),
        in_specs=[
            pl.BlockSpec((tm, D_MODEL), lambda i, e: (i, 0)),
            pl.BlockSpec((tm, LANES), lambda i, e: (i, 0)),
            pl.BlockSpec((tm, D_MODEL), lambda i, e: (i, 0)),
            pl.BlockSpec((None, D_MODEL, D_FF_EXPERT), lambda i, e: (e, 0, 0)),
            pl.BlockSpec((None, D_MODEL, D_FF_EXPERT), lambda i, e: (e, 0, 0)),
            pl.BlockSpec((None, D_FF_EXPERT, D_MODEL), lambda i, e: (e, 0, 0)),
            _const_spec((1, D_MODEL)),
        ],
        out_specs=pl.BlockSpec((tm, D_MODEL), lambda i, e: (i, 0)),
        out_shape=jax.ShapeDtypeStruct((T, D_MODEL), F32),
        scratch_shapes=[pltpu.VMEM((tm, D_MODEL), F32)],
        compiler_params=_compiler_params(("parallel", "arbitrary")),
        name="experts",
    )(xn2, comb, x1, wg, wu, wd, fw)


def _pad_heads(w, n_heads, width):
    w = w.reshape(D_MODEL, n_heads, -1)
    return jnp.pad(w, ((0, 0), (0, 0), (0, width - w.shape[-1]))).reshape(D_MODEL, n_heads * width)


def _pad_vec(v, width=LANES):
    return jnp.pad(v.astype(F32), (0, width - v.shape[0]))[None, :]


def _layer_params(l, p):
    scale = HEAD_DIM ** -0.5
    w_in = p['w_in'][l]
    wa, wb, wc, wg = jnp.split(w_in, [COLS_A, COLS_A + COLS_B, COLS_A + COLS_B + COLS_C], axis=-1)

    wqa, wka, wva = jnp.split(wa, [A_QK_COLS, 2 * A_QK_COLS], axis=-1)
    wva = _pad_heads(wva, A_HEADS, 2 * LANES)
    w_a = jnp.concatenate([wqa * scale, wka, wva], axis=-1).astype(BF16)
    ones_a = jnp.zeros((A_HEADS, 2 * LANES), F32).at[:, A_VDIM].set(1.0).reshape(1, -1)
    e_a = jnp.concatenate([jnp.zeros((1, 2 * A_QK_COLS), F32), ones_a], axis=-1)

    wqb, wkb, wvb = jnp.split(wb, [B_HEADS * HEAD_DIM, (B_HEADS + B_KV_HEADS) * HEAD_DIM], axis=-1)
    w_b = jnp.concatenate([_pad_heads(wqb, B_HEADS, LANES), _pad_heads(wkb, B_KV_HEADS, LANES),
                           _pad_heads(wvb, B_KV_HEADS, LANES)], axis=-1).astype(BF16)
    e_b = jnp.zeros((B_KV_HEADS, LANES), F32).at[:, HEAD_DIM].set(1.0).reshape(1, -1)
    qw = p['q_norm_w'][l].astype(F32) * scale
    kw = p['k_norm_w'][l].astype(F32)
    shift_b = (HEAD_DIM * SHIFT_MARGIN) * jnp.max(jnp.abs(qw)) * jnp.max(jnp.abs(kw))
    e_q = jnp.zeros((1, LANES), F32).at[0, HEAD_DIM].set(1.0)
    e_k = jnp.zeros((1, LANES), F32).at[0, HEAD_DIM].set(-shift_b)

    wpb = p['w_proj_b'][l].reshape(B_HEADS, HEAD_DIM, D_MODEL)
    wpb = jnp.pad(wpb, ((0, 0), (0, LANES - HEAD_DIM), (0, 0))).reshape(B_HEADS * LANES, D_MODEL)

    merge_w = [
        p['norm1_w'][l][None, :], wg.astype(BF16), p['w_proj_a'][l].astype(BF16), wpb.astype(BF16),
        p['w_proj_c'][l].astype(BF16), p['w_glu'][l].astype(BF16), p['b_glu'][l][None, :], p['ssm_D'][l][None, :],
        p['w_out'][l].astype(BF16), p['norm2_w'][l][None, :],
        jnp.pad(p['w_router_group'][l], ((0, 0), (0, LANES - MOE_GROUPS))), _pad_vec(p['b_router_group'][l]),
        jnp.pad(p['w_router_expert'][l], ((0, 0), (0, LANES - N_EXPERTS))), _pad_vec(p['b_router_expert'][l]),
    ]
    lam_init = 0.8 - 0.6 * math.exp(-0.3 * l)
    lamv = jnp.stack([_pad_vec(v[l])[0] for v in (p['lam_q1'], p['lam_k1'], p['lam_q2'], p['lam_k2'])])
    return dict(
        norm1=p['norm1_w'][l][None, :], w_a=w_a, e_a=e_a, w_b=w_b, e_b=e_b, w_c=wc.astype(BF16),
        qw=_pad_vec(qw), kw=_pad_vec(kw), e_q=e_q, e_k=e_k, shift_b=shift_b.reshape(1),
        lamv=lamv, lam_init=lam_init, subw=p['diff_subln_w'][l][None, :],
        ssm=_ssm_operators(p['ssm_A_re'][l], p['ssm_A_im'][l], p['ssm_log_dt'][l], p['ssm_B_re'][l],
                           p['ssm_B_im'][l], p['ssm_C_re'][l], p['ssm_C_im'][l]),
        merge_w=merge_w,
        wg=p['w_exp_gate'][l].astype(BF16), wu=p['w_exp_up'][l].astype(BF16), wd=p['w_exp_down'][l].astype(BF16),
    )


def _rope_tables(L):
    half = HEAD_DIM // 2
    inv = ROPE_BASE ** (-jnp.arange(0, half, 2, dtype=F32) / half)
    t = jnp.arange(L)
    row = (t // GRID_W).astype(F32)
    col = (t % GRID_W).astype(F32)
    ang = jnp.concatenate([row[:, None] * inv[None, :]] * 2 + [col[:, None] * inv[None, :]] * 2, axis=-1)
    cos, sin = jnp.cos(ang), jnp.sin(ang)
    first = (jnp.arange(HEAD_DIM) % half) < (half // 2)
    pad = ((0, 0), (0, LANES - HEAD_DIM))
    return (jnp.pad(cos, pad), jnp.pad(jnp.where(first, -sin, 0.0), pad), jnp.pad(jnp.where(first, 0.0, sin), pad))


def _alibi_slopes():
    s = 2.0 ** (-8.0 * jnp.arange(1, A_HEADS + 1, dtype=F32) / A_HEADS)
    return jnp.broadcast_to(s[:, None, None], (A_HEADS, 1, LANES))


def _trunk(x, layers, final_norm_w):
    Bsz, L, _ = x.shape
    T = Bsz * L
    tm = ROW_TILE
    n_chunks = L // SSM_CHUNK
    pos_blocks = L // tm
    cos, sa, sb = _rope_tables(L)
    slopes = _alibi_slopes()
    fw = final_norm_w[None, :]
    x2d = x.reshape(T, D_MODEL)

    for l, lp in enumerate(layers):
        na = lp['w_a'].shape[1]
        tn = 512
        qkv_a = _proj_call(
            "proj_a", _proj_a_kernel, x2d, lp['norm1'], lp['w_a'], [lp['e_a']],
            [pl.BlockSpec((1, tn), lambda i, j: (0, j))],
            jax.ShapeDtypeStruct((T, na), BF16), pl.BlockSpec((tm, tn), lambda i, j: (i, j)), tn)
        out_a = _diff_attention(qkv_a.reshape(Bsz, L, na), slopes, lp['lamv'], lp['subw'], lp['lam_init'], Bsz, L)

        nb = lp['w_b'].shape[1]
        heads_per_tile = tn // LANES
        n_q_tiles = B_HEADS // heads_per_tile
        tab = pl.BlockSpec((tm, LANES), lambda i, j: (i % pos_blocks, 0))
        qkv_b = _proj_call(
            "proj_b", functools.partial(_proj_b_kernel, n_q_tiles=n_q_tiles, heads_per_tile=heads_per_tile),
            x2d, lp['norm1'], lp['w_b'], [lp['qw'], lp['kw'], lp['e_q'], lp['e_k'], lp['e_b'], cos, sa, sb],
            [_const_spec((1, LANES))] * 4 + [_const_spec((1, B_KV_HEADS * LANES)), tab, tab, tab],
            jax.ShapeDtypeStruct((T, nb), BF16), pl.BlockSpec((tm, tn), lambda i, j: (i, j)), tn)
        out_b = _gqa_attention(qkv_b.reshape(Bsz, L, nb), lp['shift_b'], Bsz, L)

        zc32, zc16 = _proj_call(
            "proj_c", _proj_c_kernel, x2d, lp['norm1'], lp['w_c'], [], [],
            [jax.ShapeDtypeStruct((T, COLS_C), F32), jax.ShapeDtypeStruct((T, COLS_C), BF16)],
            [pl.BlockSpec((tm, COLS_C), lambda i, j: (i, j))] * 2, COLS_C)
        u_g = jnp.transpose(zc16.reshape(Bsz, n_chunks, SSM_CHUNK, SSM_GROUPS, SSM_P), (3, 0, 1, 2, 4))
        u_g = u_g.reshape(SSM_GROUPS, Bsz * n_chunks, SSM_FLAT)
        y_g = _ssm_apply(u_g, *lp['ssm'], Bsz, n_chunks)
        y_s = jnp.transpose(y_g.reshape(SSM_GROUPS, Bsz, n_chunks, SSM_CHUNK, SSM_P), (1, 2, 3, 0, 4))
        y_s = y_s.reshape(T, SSM_WIDTH)

        x1, xn2, comb = _merge(x2d, out_a.reshape(T, A_WIDTH), out_b.reshape(T, B_HEADS * LANES), y_s, zc32,
                               lp['merge_w'])
        x2d = _moe(xn2, comb, x1, lp['wg'], lp['wu'], lp['wd'], fw, final_norm=(l == len(layers) - 1))
    return x2d.reshape(Bsz, L, D_MODEL)


def kernel(x_prompt, x_sample, norm1_w, w_in, lam_q1, lam_k1, lam_q2, lam_k2, diff_subln_w, q_norm_w, k_norm_w,
           ssm_A_re, ssm_A_im, ssm_log_dt, ssm_B_re, ssm_B_im, ssm_C_re, ssm_C_im, ssm_D, w_glu, b_glu,
           w_proj_a, w_proj_b, w_proj_c, w_out, norm2_w, w_router_group, b_router_group, w_router_expert,
           b_router_expert, w_exp_gate, w_exp_up, w_exp_down, final_norm_w):
    p = dict(norm1_w=norm1_w, w_in=w_in, lam_q1=lam_q1, lam_k1=lam_k1, lam_q2=lam_q2, lam_k2=lam_k2,
             diff_subln_w=diff_subln_w, q_norm_w=q_norm_w, k_norm_w=k_norm_w, ssm_A_re=ssm_A_re, ssm_A_im=ssm_A_im,
             ssm_log_dt=ssm_log_dt, ssm_B_re=ssm_B_re, ssm_B_im=ssm_B_im, ssm_C_re=ssm_C_re, ssm_C_im=ssm_C_im,
             ssm_D=ssm_D, w_glu=w_glu, b_glu=b_glu, w_proj_a=w_proj_a, w_proj_b=w_proj_b, w_proj_c=w_proj_c,
             w_out=w_out, norm2_w=norm2_w, w_router_group=w_router_group, b_router_group=b_router_group,
             w_router_expert=w_router_expert, b_router_expert=b_router_expert, w_exp_gate=w_exp_gate,
             w_exp_up=w_exp_up, w_exp_down=w_exp_down)
    layers = [_layer_params(l, p) for l in range(DEPTH)]
    return (_trunk(x_prompt, layers, final_norm_w), _trunk(x_sample, layers, final_norm_w))
```

```python
import functools
import math

import jax
import jax.numpy as jnp
from jax import lax
from jax.experimental import pallas as pl
from jax.experimental.pallas import tpu as pltpu

F32 = jnp.float32
BF16 = jnp.bfloat16

D_MODEL = 1024
DEPTH = 2
HEAD_DIM = 64
EPS = 1e-6
A_HEADS = 4
A_VDIM = 2 * HEAD_DIM
A_QK_COLS = A_HEADS * 2 * HEAD_DIM
A_WIDTH = A_HEADS * A_VDIM
COLS_A = 2 * A_QK_COLS + A_WIDTH
B_HEADS = 8
B_KV_HEADS = 2
B_GROUP = B_HEADS // B_KV_HEADS
B_WIDTH = B_HEADS * HEAD_DIM
COLS_B = B_HEADS * HEAD_DIM + 2 * B_KV_HEADS * HEAD_DIM
ROPE_BASE = 10000.0
GRID_W = 64
SSM_WIDTH = 512
SSM_P = 16
SSM_GROUPS = SSM_WIDTH // SSM_P
SSM_N = 64
COLS_C = SSM_WIDTH
N_BRANCH = 3
MOE_GROUPS = 4
EXPERTS_PER_GROUP = 4
N_EXPERTS = MOE_GROUPS * EXPERTS_PER_GROUP
D_FF_EXPERT = 512

LANES = 128
SUBLANES = 8
VMEM_LIMIT_BYTES = 56 * 1024 * 1024

SSM_CHUNK = 64
SSM_FLAT = SSM_CHUNK * SSM_P
SSM_STATE_COLS = 4 * LANES

ROW_TILE = 512
MERGE_TILE = 512
EXPERT_TILE = 1024
ATT_ROWS = 1024
ATT_KV_TILE = 2048
MAX_STATIC_SHIFT = 40.0
SHIFT_MARGIN = 1.02


def _compiler_params(semantics):
    return pltpu.CompilerParams(dimension_semantics=semantics, vmem_limit_bytes=VMEM_LIMIT_BYTES)


def _const_spec(shape):
    zeros = (0,) * len(shape)
    return pl.BlockSpec(shape, lambda *_: zeros)


def _rms(x, w):
    ms = jnp.mean(x * x, axis=-1, keepdims=True)
    return x * lax.rsqrt(ms + EPS) * w


def _proj_kernel(x_ref, nw_ref, wa_ref, wb_ref, wc_ref, ea_ref, qw_ref, kw_ref, eq_ref, ek_ref, ev_ref,
                 cos_ref, sa_ref, sb_ref, oa_ref, ob_ref, oc32_ref, oc16_ref):
    xn = _rms(x_ref[...], nw_ref[...]).astype(BF16)

    oa_ref[...] = (jnp.dot(xn, wa_ref[...], preferred_element_type=F32) + ea_ref[...]).astype(oa_ref.dtype)

    zc = jnp.dot(xn, wc_ref[...], preferred_element_type=F32)
    oc32_ref[...] = zc
    oc16_ref[...] = zc.astype(BF16)

    zb = jnp.dot(xn, wb_ref[...], preferred_element_type=F32)
    cos, sa, sb = cos_ref[...], sa_ref[...], sb_ref[...]

    def norm_rope(zh, w):
        ms = jnp.sum(zh * zh, axis=-1, keepdims=True) * (1.0 / HEAD_DIM)
        y = zh * lax.rsqrt(ms + EPS) * w
        return y * cos + pltpu.roll(y, LANES - 16, 1) * sa + pltpu.roll(y, 16, 1) * sb

    for h in range(B_HEADS + B_KV_HEADS):
        sl = slice(h * LANES, (h + 1) * LANES)
        w, e = (qw_ref, eq_ref) if h < B_HEADS else (kw_ref, ek_ref)
        ob_ref[:, sl] = (norm_rope(zb[:, sl], w[...]) + e[...]).astype(ob_ref.dtype)
    sl = slice((B_HEADS + B_KV_HEADS) * LANES, (B_HEADS + 2 * B_KV_HEADS) * LANES)
    ob_ref[:, sl] = (zb[:, sl] + ev_ref[...]).astype(ob_ref.dtype)


def _project(x2d, lp, tables, pos_blocks):
    T = x2d.shape[0]
    tm = ROW_TILE
    na, nb = lp['w_a'].shape[1], lp['w_b'].shape[1]

    def rows(width):
        return pl.BlockSpec((tm, width), lambda i: (i, 0))

    def whole(arr):
        return pl.BlockSpec(arr.shape, lambda i: (0, 0), pipeline_mode=pl.Buffered(1))

    consts = [lp['norm1'], lp['w_a'], lp['w_b'], lp['w_c'], lp['e_a'], lp['qw'], lp['kw'], lp['e_q'], lp['e_k'],
              lp['e_b']]
    table_spec = pl.BlockSpec((tm, LANES), lambda i: (i % pos_blocks, 0))
    return pl.pallas_call(
        _proj_kernel,
        grid=(T // tm,),
        in_specs=[rows(D_MODEL)] + [whole(c) for c in consts] + [table_spec] * 3,
        out_specs=[rows(na), rows(nb), rows(COLS_C), rows(COLS_C)],
        out_shape=[jax.ShapeDtypeStruct((T, na), BF16), jax.ShapeDtypeStruct((T, nb), BF16),
                   jax.ShapeDtypeStruct((T, COLS_C), F32), jax.ShapeDtypeStruct((T, COLS_C), BF16)],
        compiler_params=_compiler_params(("parallel",)),
        name="in_proj",
    )(x2d, *consts, *tables)


_NT = (((1,), (1,)), ((), ()))


def _online_softmax(qs, k_fn, v_ref, m_ref, acc_ref, tk, n_chunks, bias_fn):
    m_ref[...] = jnp.full(m_ref.shape, -jnp.inf, F32)
    acc_ref[...] = jnp.zeros(acc_ref.shape, F32)

    def body(c, carry):
        start = pl.multiple_of(c * tk, tk)
        s = lax.dot_general(qs, k_fn(start), _NT, preferred_element_type=F32)
        if bias_fn is not None:
            s = s + bias_fn(start)
        m_prev = m_ref[...]
        m_new = jnp.maximum(m_prev, jnp.max(s, axis=-1, keepdims=True))
        p = jnp.exp(s - m_new)
        alpha = jnp.exp(m_prev - m_new)
        pv = jnp.dot(p.astype(BF16), v_ref[pl.ds(start, tk), :], preferred_element_type=F32)
        acc_ref[...] = alpha * acc_ref[...] + pv
        m_ref[...] = m_new
        return carry

    lax.fori_loop(0, n_chunks, body, 0)


def _shifted_softmax(qs, k_fn, v_ref, acc_ref, tk, lo, hi, bias_fn=None):
    def body(c, carry):
        start = pl.multiple_of(c * tk, tk)
        s = lax.dot_general(qs, k_fn(start), _NT, preferred_element_type=F32)
        if bias_fn is not None:
            s = s + bias_fn(start)
        acc_ref[...] += jnp.dot(jnp.exp(s).astype(BF16), v_ref[pl.ds(start, tk), :], preferred_element_type=F32)
        return carry

    lax.fori_loop(lo, hi, body, 0)


def _half_norms(x, lane):
    sq = x.astype(F32)
    sq = sq * sq
    n1 = jnp.sum(jnp.where(lane < HEAD_DIM, sq, 0.0), axis=-1, keepdims=True)
    n2 = jnp.sum(jnp.where(lane >= HEAD_DIM, sq, 0.0), axis=-1, keepdims=True)
    return jnp.sqrt(n1), jnp.sqrt(n2)


def _diff_attn_kernel(q_ref, k_ref, v_ref, slope_ref, lam_ref, subw_ref, o_ref, kmax_ref, kaug_ref, m_ref, acc_ref,
                      *, tq, tk, lam_init):
    i = pl.program_id(2)
    n_chunks = k_ref.shape[0] // tk
    slope = slope_ref[...][:, :1]
    lane_k = lax.broadcasted_iota(jnp.int32, (tk, LANES), 1)

    @pl.when(i == 0)
    def _():
        def body(c, carry):
            start = pl.multiple_of(c * tk, tk)
            n1, n2 = _half_norms(k_ref[pl.ds(start, tk), :], lane_k)
            k_pos = start + lax.broadcasted_iota(jnp.int32, (tk, LANES), 0)
            hi = (k_pos >> 7).astype(F32) * (slope * float(LANES))
            lo = (k_pos & (LANES - 1)).astype(F32) * slope
            aug = jnp.where(lane_k < 3, 1.0, jnp.where(lane_k == 3, hi, jnp.where(lane_k == 4, lo, 0.0)))
            kaug_ref[pl.ds(start, tk), :] = aug.astype(BF16)
            return (jnp.maximum(carry[0], jnp.max(n1, axis=0, keepdims=True)),
                    jnp.maximum(carry[1], jnp.max(n2, axis=0, keepdims=True)))

        zero11 = jnp.zeros((1, 1), F32)
        k1, k2 = lax.fori_loop(0, n_chunks, body, (zero11, zero11))
        kmax_ref[0:1, :] = jnp.broadcast_to(k1, (1, LANES))
        kmax_ref[1:2, :] = jnp.broadcast_to(k2, (1, LANES))

    q = q_ref[...]
    lane = lax.broadcasted_iota(jnp.int32, q.shape, 1)
    zero = jnp.zeros_like(q)
    qz = jnp.concatenate([jnp.where(lane < HEAD_DIM, q, zero), jnp.where(lane >= HEAD_DIM, q, zero)], axis=0)

    qn1, qn2 = _half_norms(q, lane)
    shift = jnp.concatenate([qn1 * kmax_ref[0:1, 0:1], qn2 * kmax_ref[1:2, 0:1]], axis=0) * SHIFT_MARGIN
    static_ok = jnp.max(shift) <= MAX_STATIC_SHIFT

    row = lax.broadcasted_iota(jnp.int32, (2 * tq, 1), 0)
    q_pos = i * tq + jnp.where(row >= tq, row - tq, row)
    q_pos_f = q_pos.astype(F32)
    lane2 = lax.broadcasted_iota(jnp.int32, (2 * tq, LANES), 1)
    shift_cols = jnp.where(lane2 == 0, -shift, 0.0)
    pos_cols = jnp.where(lane2 == 1, (q_pos >> 7).astype(F32) * (-slope * float(LANES)),
                         jnp.where(lane2 == 2, (q_pos & (LANES - 1)).astype(F32) * (-slope),
                                   jnp.where((lane2 == 3) | (lane2 == 4), 1.0, 0.0)))
    q_left = jnp.concatenate([qz, (shift_cols + pos_cols).astype(BF16)], axis=1)
    q_right = jnp.concatenate([qz, (shift_cols - pos_cols).astype(BF16)], axis=1)
    q_diag = jnp.concatenate([qz, shift_cols.astype(BF16)], axis=1)

    def k_fn(start):
        return jnp.concatenate([k_ref[pl.ds(start, tk), :], kaug_ref[pl.ds(start, tk), :]], axis=1)

    def bias_fn(start):
        k_pos = (start + lax.broadcasted_iota(jnp.int32, (1, tk), 1)).astype(F32)
        return -slope * jnp.abs(q_pos_f - k_pos)

    @pl.when(static_ok)
    def _():
        acc_ref[...] = jnp.zeros(acc_ref.shape, F32)
        c_diag = (i * tq) // tk
        _shifted_softmax(q_left, k_fn, v_ref, acc_ref, tk, 0, c_diag)
        _shifted_softmax(q_diag, k_fn, v_ref, acc_ref, tk, c_diag, c_diag + 1, bias_fn)
        _shifted_softmax(q_right, k_fn, v_ref, acc_ref, tk, c_diag + 1, n_chunks)

    @pl.when(jnp.logical_not(static_ok))
    def _():
        _online_softmax(q_diag, k_fn, v_ref, m_ref, acc_ref, tk, n_chunks, bias_fn)

    acc = acc_ref[...]
    o = acc[:, :A_VDIM] / acc[:, A_VDIM:A_VDIM + 1]
    lv = lam_ref[...]
    lam = (jnp.exp(jnp.sum(lv[0:1] * lv[1:2], axis=-1, keepdims=True))
           - jnp.exp(jnp.sum(lv[2:3] * lv[3:4], axis=-1, keepdims=True)) + lam_init)
    d = o[:tq] - lam * o[tq:]
    o_ref[...] = (_rms(d, subw_ref[...]) * (1.0 - lam_init)).astype(o_ref.dtype)


def _gqa_attn_kernel(shift_ref, q_ref, k_ref, v_ref, o_ref, m_ref, acc_ref, *, tq, tk):
    qs = jnp.concatenate([q_ref[:, h * LANES:(h + 1) * LANES] for h in range(B_GROUP)], axis=0)
    n_chunks = k_ref.shape[0] // tk
    static_ok = shift_ref[0] <= MAX_STATIC_SHIFT

    def k_fn(start):
        return k_ref[pl.ds(start, tk), :]

    @pl.when(static_ok)
    def _():
        acc_ref[...] = jnp.zeros(acc_ref.shape, F32)
        _shifted_softmax(qs, k_fn, v_ref, acc_ref, tk, 0, n_chunks)

    @pl.when(jnp.logical_not(static_ok))
    def _():
        _online_softmax(qs, k_fn, v_ref, m_ref, acc_ref, tk, n_chunks, None)

    acc = acc_ref[...]
    o = acc / acc[:, HEAD_DIM:HEAD_DIM + 1]
    for h in range(B_GROUP):
        o_ref[:, h * LANES:(h + 1) * LANES] = o[h * tq:(h + 1) * tq].astype(o_ref.dtype)


def _attn_tiles(L, stacked):
    return min(ATT_ROWS // stacked, L), min(ATT_KV_TILE, L)


def _resident(block_shape, index_map):
    return pl.BlockSpec(block_shape, index_map, pipeline_mode=pl.Buffered(1))


def _diff_attention(qkv, slopes, lamv, subw, lam_init, Bsz, L):
    tq, tk = _attn_tiles(L, 2)
    nq = A_HEADS
    kernel = functools.partial(_diff_attn_kernel, tq=tq, tk=tk, lam_init=lam_init)
    return pl.pallas_call(
        kernel,
        grid=(Bsz, A_HEADS, L // tq),
        in_specs=[
            pl.BlockSpec((None, tq, LANES), lambda b, h, i: (b, i, h)),
            _resident((None, L, LANES), lambda b, h, i: (b, 0, nq + h)),
            _resident((None, L, 2 * LANES), lambda b, h, i: (b, 0, nq + h)),
            pl.BlockSpec((None, 1, LANES), lambda b, h, i: (h, 0, 0)),
            _const_spec((4, LANES)),
            _const_spec((1, A_VDIM)),
        ],
        out_specs=pl.BlockSpec((None, tq, LANES), lambda b, h, i: (b, i, h)),
        out_shape=jax.ShapeDtypeStruct((Bsz, L, A_WIDTH), BF16),
        scratch_shapes=[pltpu.VMEM((SUBLANES, LANES), F32), pltpu.VMEM((L, LANES), BF16),
                        pltpu.VMEM((2 * tq, 1), F32), pltpu.VMEM((2 * tq, 2 * LANES), F32)],
        compiler_params=_compiler_params(("parallel", "parallel", "arbitrary")),
        name="diff_attn",
    )(qkv, qkv, qkv, slopes, lamv, subw)


def _gqa_attention(qkv, shift, Bsz, L):
    tq, tk = _attn_tiles(L, B_GROUP)
    q_blocks = B_HEADS // B_GROUP
    kernel = functools.partial(_gqa_attn_kernel, tq=tq, tk=tk)
    return pl.pallas_call(
        kernel,
        grid=(Bsz, B_KV_HEADS, L // tq),
        in_specs=[
            pl.BlockSpec(memory_space=pltpu.SMEM),
            pl.BlockSpec((None, tq, B_GROUP * LANES), lambda b, g, i: (b, i, g)),
            _resident((None, L, LANES), lambda b, g, i: (b, 0, B_HEADS + g)),
            _resident((None, L, LANES), lambda b, g, i: (b, 0, B_HEADS + B_KV_HEADS + g)),
        ],
        out_specs=pl.BlockSpec((None, tq, B_GROUP * LANES), lambda b, g, i: (b, i, g)),
        out_shape=jax.ShapeDtypeStruct((Bsz, L, q_blocks * B_GROUP * LANES), BF16),
        scratch_shapes=[pltpu.VMEM((B_GROUP * tq, 1), F32), pltpu.VMEM((B_GROUP * tq, LANES), F32)],
        compiler_params=_compiler_params(("parallel", "parallel", "arbitrary")),
        name="gqa_attn",
    )(shift, qkv, qkv, qkv)


def _ssm_kernel(u_ref, m_ref, wst_ref, wout_ref, a_ref, y_ref, s_ref, h_ref, *, n_batch, n_chunks):
    u = u_ref[...]
    y_ref[...] = jnp.dot(u, m_ref[...], preferred_element_type=F32)
    s_ref[...] = jnp.dot(u, wst_ref[...], preferred_element_type=F32)
    a = a_ref[...]
    afr, afi = a[:, 0:LANES], a[:, LANES:2 * LANES]
    abr, abi = a[:, 2 * LANES:3 * LANES], a[:, 3 * LANES:4 * LANES]
    zero = jnp.zeros((1, LANES), F32)

    def step(cg, carry):
        new = []
        for b in range(n_batch):
            hr, hi, gr, gi = carry[4 * b:4 * b + 4]
            rf = pl.multiple_of(b * n_chunks + cg * SUBLANES, SUBLANES)
            rb = pl.multiple_of(b * n_chunks + n_chunks - SUBLANES - cg * SUBLANES, SUBLANES)
            sf = s_ref[pl.ds(rf, SUBLANES), 0:2 * LANES]
            sb = s_ref[pl.ds(rb, SUBLANES), 2 * LANES:4 * LANES]
            rows_f, rows_b = [], [None] * SUBLANES
            for j in range(SUBLANES):
                rows_f.append(jnp.concatenate([hr, hi], axis=1))
                hr, hi = (afr * hr - afi * hi + sf[j:j + 1, 0:LANES],
                          afr * hi + afi * hr + sf[j:j + 1, LANES:2 * LANES])
                jb = SUBLANES - 1 - j
                rows_b[jb] = jnp.concatenate([gr, gi], axis=1)
                gr, gi = (abr * gr - abi * gi + sb[jb:jb + 1, 0:LANES],
                          abr * gi + abi * gr + sb[jb:jb + 1, LANES:2 * LANES])
            h_ref[pl.ds(rf, SUBLANES), 0:2 * LANES] = jnp.concatenate(rows_f, axis=0)
            h_ref[pl.ds(rb, SUBLANES), 2 * LANES:4 * LANES] = jnp.concatenate(rows_b, axis=0)
            new += [hr, hi, gr, gi]
        return tuple(new)

    lax.fori_loop(0, n_chunks // SUBLANES, step, (zero,) * (4 * n_batch))
    y_ref[...] += jnp.dot(h_ref[...].astype(BF16), wout_ref[...], preferred_element_type=F32)


def _ssm_apply(u_g, m, wst, wout, a64, n_batch, n_chunks):
    nc = n_batch * n_chunks
    kernel = functools.partial(_ssm_kernel, n_batch=n_batch, n_chunks=n_chunks)
    return pl.pallas_call(
        kernel,
        grid=(SSM_GROUPS,),
        in_specs=[
            pl.BlockSpec((None, nc, SSM_FLAT), lambda g: (g, 0, 0)),
            pl.BlockSpec((None, SSM_FLAT, SSM_FLAT), lambda g: (g, 0, 0)),
            pl.BlockSpec((None, SSM_FLAT, SSM_STATE_COLS), lambda g: (g, 0, 0)),
            pl.BlockSpec((None, SSM_STATE_COLS, SSM_FLAT), lambda g: (g, 0, 0)),
            pl.BlockSpec((None, 1, SSM_STATE_COLS), lambda g: (g, 0, 0)),
        ],
        out_specs=pl.BlockSpec((None, nc, SSM_FLAT), lambda g: (g, 0, 0)),
        out_shape=jax.ShapeDtypeStruct((SSM_GROUPS, nc, SSM_FLAT), F32),
        scratch_shapes=[pltpu.VMEM((nc, SSM_STATE_COLS), F32), pltpu.VMEM((nc, SSM_STATE_COLS), F32)],
        compiler_params=_compiler_params(("parallel",)),
        name="ssm_chunks",
    )(u_g, m, wst, wout, a64)


def _ssm_operators(A_re, A_im, log_dt, B_re, B_im, C_re, C_im):
    hi = lax.Precision.HIGHEST
    G, N, P, Tc = SSM_GROUPS, SSM_N, SSM_P, SSM_CHUNK
    Br, Bi, Cr, Ci = B_re.astype(F32), B_im.astype(F32), C_re.astype(F32), C_im.astype(F32)
    steps = jnp.arange(Tc + 1, dtype=F32)

    per_dir = []
    for dirn in range(2):
        dt = jnp.exp(log_dt[dirn].astype(F32))[:, None]
        ar, ai = A_re[dirn].astype(F32), A_im[dirn].astype(F32)
        mag = jnp.exp(dt * ar)
        er, ei = mag * jnp.cos(dt * ai), mag * jnp.sin(dt * ai)
        den = ar * ar + ai * ai
        fr = ((er - 1.0) * ar + ei * ai) / den
        fi = (ei * ar - (er - 1.0) * ai) / den
        bbr = fr[..., None] * Br - fi[..., None] * Bi
        bbi = fr[..., None] * Bi + fi[..., None] * Br
        pm = jnp.exp(steps[None, :, None] * (dt * ar)[:, None, :])
        ang = steps[None, :, None] * (dt * ai)[:, None, :]
        pr, pi = pm * jnp.cos(ang), pm * jnp.sin(ang)
        car = Cr[:, None] * pr[:, :, None, :] - Ci[:, None] * pi[:, :, None, :]
        cai = Cr[:, None] * pi[:, :, None, :] + Ci[:, None] * pr[:, :, None, :]
        kern = (jnp.einsum('gkpn,gnq->gkpq', car[:, :Tc], bbr, precision=hi)
                - jnp.einsum('gkpn,gnq->gkpq', cai[:, :Tc], bbi, precision=hi))
        abr = pr[..., None] * bbr[:, None] - pi[..., None] * bbi[:, None]
        abi = pr[..., None] * bbi[:, None] + pi[..., None] * bbr[:, None]
        per_dir.append(dict(kern=kern, car=car, cai=cai, abr=abr, abi=abi, pr=pr, pi=pi))

    f, b = per_dir
    k2 = jnp.concatenate([b['kern'][:, :0:-1], f['kern'][:, :1] + b['kern'][:, :1], f['kern'][:, 1:]], axis=1)
    k2 = jnp.transpose(k2, (0, 3, 1, 2)).astype(BF16)
    m = jnp.stack([k2[:, :, Tc - 1 - s:2 * Tc - 1 - s, :] for s in range(Tc)], axis=1)
    m = m.reshape(G, Tc * P, Tc * P)

    pad = jnp.zeros((G, Tc * P, LANES - N), F32)

    def st_cols(xr):
        return jnp.concatenate([jnp.transpose(xr, (0, 1, 3, 2)).reshape(G, Tc * P, N), pad], axis=-1)

    wst = jnp.concatenate([st_cols(f['abr'][:, Tc - 1::-1][:, :Tc]), st_cols(f['abi'][:, Tc - 1::-1][:, :Tc]),
                           st_cols(b['abr'][:, :Tc]), st_cols(b['abi'][:, :Tc])], axis=-1)

    padr = jnp.zeros((G, LANES - N, Tc * P), F32)

    def out_rows(x):
        return jnp.concatenate([jnp.transpose(x, (0, 3, 1, 2)).reshape(G, N, Tc * P), padr], axis=1)

    wout = jnp.concatenate([out_rows(f['car'][:, 1:Tc + 1]), out_rows(-f['cai'][:, 1:Tc + 1]),
                            out_rows(b['car'][:, Tc:0:-1]), out_rows(-b['cai'][:, Tc:0:-1])], axis=1)

    padc = jnp.zeros((G, LANES - N), F32)
    a64 = jnp.concatenate([f['pr'][:, Tc], padc, f['pi'][:, Tc], padc,
                           b['pr'][:, Tc], padc, b['pi'][:, Tc], padc], axis=-1)[:, None, :]
    return m.astype(BF16), wst.astype(BF16), wout.astype(BF16), a64


def _merge_kernel(x_ref, oa_ref, ob_ref, ys_ref, zc_ref, n1_ref, wg_ref, wpa_ref, wpb_ref, wpc_ref,
                  wglu_ref, bglu_ref, dskip_ref, wout_ref, n2_ref, wr_hi_ref, wr_lo_ref, br_ref,
                  x1_ref, xn2_ref, comb_ref):
    x = x_ref[...]
    xn = _rms(x, n1_ref[...]).astype(BF16)
    gates = jax.nn.sigmoid(jnp.dot(xn, wg_ref[...], preferred_element_type=F32))

    y = ys_ref[...] + dskip_ref[...] * zc_ref[...]
    g = jax.nn.gelu(y)
    glu = jnp.dot(g.astype(BF16), wglu_ref[...], preferred_element_type=F32) + bglu_ref[...]
    out_c = g * jax.nn.sigmoid(glu)

    merged = (gates[:, 0:D_MODEL] * jnp.dot(oa_ref[...], wpa_ref[...], preferred_element_type=F32)
              + gates[:, D_MODEL:2 * D_MODEL] * jnp.dot(ob_ref[...], wpb_ref[...], preferred_element_type=F32)
              + gates[:, 2 * D_MODEL:3 * D_MODEL]
              * jnp.dot(out_c.astype(BF16), wpc_ref[...], preferred_element_type=F32))
    x1 = x + jnp.dot(merged.astype(BF16), wout_ref[...], preferred_element_type=F32)
    x1_ref[...] = x1

    xn2 = _rms(x1, n2_ref[...])
    xn2_ref[...] = xn2.astype(BF16)

    lane = lax.broadcasted_iota(jnp.int32, (x.shape[0], LANES), 1).astype(F32)
    neg = jnp.float32(-jnp.inf)
    big = jnp.float32(LANES)
    x_hi = xn2.astype(BF16)
    x_lo = (xn2 - x_hi.astype(F32)).astype(BF16)
    le = (jnp.dot(x_hi, wr_hi_ref[...], preferred_element_type=F32)
          + (jnp.dot(x_hi, wr_lo_ref[...], preferred_element_type=F32)
             + jnp.dot(x_lo, wr_hi_ref[...], preferred_element_type=F32))) + br_ref[...]
    lg = jnp.where((lane >= N_EXPERTS) & (lane < N_EXPERTS + MOE_GROUPS), le, neg)
    g_max = jnp.max(lg, axis=-1, keepdims=True)
    g_idx = jnp.min(jnp.where(lg == g_max, lane, big), axis=-1, keepdims=True) - float(N_EXPERTS)
    g_w = 1.0 / jnp.sum(jnp.exp(lg - g_max), axis=-1, keepdims=True)
    in_group = (lane >= g_idx * EXPERTS_PER_GROUP) & (lane < (g_idx + 1) * EXPERTS_PER_GROUP)
    v = jnp.where(in_group, le, neg)
    top1 = jnp.max(v, axis=-1, keepdims=True)
    i1 = jnp.min(jnp.where(v == top1, lane, big), axis=-1, keepdims=True)
    v2 = jnp.where(lane == i1, neg, v)
    top2 = jnp.max(v2, axis=-1, keepdims=True)
    i2 = jnp.min(jnp.where(v2 == top2, lane, big), axis=-1, keepdims=True)
    e2 = jnp.exp(top2 - top1)
    inv = 1.0 / (1.0 + e2)
    comb_ref[...] = (jnp.where(lane == i1, inv * g_w, 0.0) + jnp.where(lane == i2, e2 * inv * g_w, 0.0))


def _merge(x2d, oa, ob, ys, zc, weights):
    T = x2d.shape[0]
    tm = MERGE_TILE

    def rows(width):
        return pl.BlockSpec((tm, width), lambda i: (i, 0))

    w_specs = [pl.BlockSpec(w.shape, lambda i, nd=w.ndim: (0,) * nd, pipeline_mode=pl.Buffered(1)) for w in weights]
    return pl.pallas_call(
        _merge_kernel,
        grid=(T // tm,),
        in_specs=[rows(D_MODEL), rows(A_WIDTH), rows(B_HEADS * LANES), rows(SSM_WIDTH), rows(SSM_WIDTH)] + w_specs,
        out_specs=[rows(D_MODEL), rows(D_MODEL), rows(LANES)],
        out_shape=[jax.ShapeDtypeStruct((T, D_MODEL), F32), jax.ShapeDtypeStruct((T, D_MODEL), BF16),
                   jax.ShapeDtypeStruct((T, LANES), F32)],
        compiler_params=_compiler_params(("parallel",)),
        name="merge_router",
    )(x2d, oa, ob, ys, zc, *weights)


def _moe_kernel(xn_ref, comb_ref, x1_ref, wg_ref, wu_ref, wd_ref, fw_ref, o_ref, acc_ref, *, final_norm):
    e = pl.program_id(1)

    @pl.when(e == 0)
    def _():
        acc_ref[...] = jnp.zeros(acc_ref.shape, F32)

    xn = xn_ref[...]
    h = jax.nn.silu(jnp.dot(xn, wg_ref[...], preferred_element_type=F32)) * jnp.dot(
        xn, wu_ref[...], preferred_element_type=F32)
    y = jnp.dot(h.astype(BF16), wd_ref[...], preferred_element_type=F32)
    comb = comb_ref[...]
    lane = lax.broadcasted_iota(jnp.int32, comb.shape, 1)
    c = jnp.sum(jnp.where(lane == e, comb, 0.0), axis=-1, keepdims=True)
    acc_ref[...] += c * y

    @pl.when(e == N_EXPERTS - 1)
    def _():
        out = x1_ref[...] + acc_ref[...]
        if final_norm:
            out = _rms(out, fw_ref[...])
        o_ref[...] = out


def _moe(xn2, comb, x1, wg, wu, wd, fw, final_norm):
    T = x1.shape[0]
    tm = EXPERT_TILE
    kernel = functools.partial(_moe_kernel, final_norm=final_norm)
    return pl.pallas_call(
        kernel,
        grid=(T // tm, N_EXPERTS),
        in_specs=[
            pl.BlockSpec((tm, D_MODEL), lambda i, e: (i, 0)),
            pl.BlockSpec((tm, LANES), lambda i, e: (i, 0)),
            pl.BlockSpec((tm, D_MODEL), lambda i, e: (i, 0)),
            pl.BlockSpec((None, D_MODEL, D_FF_EXPERT), lambda i, e: (e, 0, 0)),
            pl.BlockSpec((None, D_MODEL, D_FF_EXPERT), lambda i, e: (e, 0, 0)),
            pl.BlockSpec((None, D_FF_EXPERT, D_MODEL), lambda i, e: (e, 0, 0)),
            _const_spec((1, D_MODEL)),
        ],
        out_specs=pl.BlockSpec((tm, D_MODEL), lambda i, e: (i, 0)),
        out_shape=jax.ShapeDtypeStruct((T, D_MODEL), F32),
        scratch_shapes=[pltpu.VMEM((tm, D_MODEL), F32)],
        compiler_params=_compiler_params(("parallel", "arbitrary")),
        name="experts",
    )(xn2, comb, x1, wg, wu, wd, fw)


def _pad_heads(w, n_heads, width):
    w = w.reshape(D_MODEL, n_heads, -1)
    return jnp.pad(w, ((0, 0), (0, 0), (0, width - w.shape[-1]))).reshape(D_MODEL, n_heads * width)


def _pad_vec(v, width=LANES):
    return jnp.pad(v.astype(F32), (0, width - v.shape[0]))[None, :]


def _layer_params(l, p):
    scale = HEAD_DIM ** -0.5
    w_in = p['w_in'][l]
    wa, wb, wc, wg = jnp.split(w_in, [COLS_A, COLS_A + COLS_B, COLS_A + COLS_B + COLS_C], axis=-1)

    wqa, wka, wva = jnp.split(wa, [A_QK_COLS, 2 * A_QK_COLS], axis=-1)
    wva = _pad_heads(wva, A_HEADS, 2 * LANES)
    w_a = jnp.concatenate([wqa * scale, wka, wva], axis=-1).astype(BF16)
    ones_a = jnp.zeros((A_HEADS, 2 * LANES), F32).at[:, A_VDIM].set(1.0).reshape(1, -1)
    e_a = jnp.concatenate([jnp.zeros((1, 2 * A_QK_COLS), F32), ones_a], axis=-1)

    wqb, wkb, wvb = jnp.split(wb, [B_HEADS * HEAD_DIM, (B_HEADS + B_KV_HEADS) * HEAD_DIM], axis=-1)
    w_b = jnp.concatenate([_pad_heads(wqb, B_HEADS, LANES), _pad_heads(wkb, B_KV_HEADS, LANES),
                           _pad_heads(wvb, B_KV_HEADS, LANES)], axis=-1).astype(BF16)
    e_b = jnp.zeros((B_KV_HEADS, LANES), F32).at[:, HEAD_DIM].set(1.0).reshape(1, -1)
    qw = p['q_norm_w'][l].astype(F32) * scale
    kw = p['k_norm_w'][l].astype(F32)
    shift_b = (HEAD_DIM * SHIFT_MARGIN) * jnp.max(jnp.abs(qw)) * jnp.max(jnp.abs(kw))
    e_q = jnp.zeros((1, LANES), F32).at[0, HEAD_DIM].set(1.0)
    e_k = jnp.zeros((1, LANES), F32).at[0, HEAD_DIM].set(-shift_b)

    wpb = p['w_proj_b'][l].reshape(B_HEADS, HEAD_DIM, D_MODEL)
    wpb = jnp.pad(wpb, ((0, 0), (0, LANES - HEAD_DIM), (0, 0))).reshape(B_HEADS * LANES, D_MODEL)

    w_r = jnp.pad(jnp.concatenate([p['w_router_expert'][l], p['w_router_group'][l]], axis=-1).astype(F32),
                  ((0, 0), (0, LANES - N_EXPERTS - MOE_GROUPS)))
    w_r_hi = w_r.astype(BF16)
    merge_w = [
        p['norm1_w'][l][None, :], wg.astype(BF16), p['w_proj_a'][l].astype(BF16), wpb.astype(BF16),
        p['w_proj_c'][l].astype(BF16), p['w_glu'][l].astype(BF16), p['b_glu'][l][None, :], p['ssm_D'][l][None, :],
        p['w_out'][l].astype(BF16), p['norm2_w'][l][None, :],
        w_r_hi, (w_r - w_r_hi.astype(F32)).astype(BF16),
        _pad_vec(jnp.concatenate([p['b_router_expert'][l], p['b_router_group'][l]])),
    ]
    lam_init = 0.8 - 0.6 * math.exp(-0.3 * l)
    lamv = jnp.stack([_pad_vec(v[l])[0] for v in (p['lam_q1'], p['lam_k1'], p['lam_q2'], p['lam_k2'])])
    return dict(
        norm1=p['norm1_w'][l][None, :], w_a=w_a, e_a=e_a, w_b=w_b, e_b=e_b, w_c=wc.astype(BF16),
        qw=_pad_vec(qw), kw=_pad_vec(kw), e_q=e_q, e_k=e_k, shift_b=shift_b.reshape(1),
        lamv=lamv, lam_init=lam_init, subw=p['diff_subln_w'][l][None, :],
        ssm=_ssm_operators(p['ssm_A_re'][l], p['ssm_A_im'][l], p['ssm_log_dt'][l], p['ssm_B_re'][l],
                           p['ssm_B_im'][l], p['ssm_C_re'][l], p['ssm_C_im'][l]),
        merge_w=merge_w,
        wg=p['w_exp_gate'][l].astype(BF16), wu=p['w_exp_up'][l].astype(BF16), wd=p['w_exp_down'][l].astype(BF16),
    )


def _rope_tables(L):
    half = HEAD_DIM // 2
    inv = ROPE_BASE ** (-jnp.arange(0, half, 2, dtype=F32) / half)
    t = jnp.arange(L)
    row = (t // GRID_W).astype(F32)
    col = (t % GRID_W).astype(F32)
    ang = jnp.concatenate([row[:, None] * inv[None, :]] * 2 + [col[:, None] * inv[None, :]] * 2, axis=-1)
    cos, sin = jnp.cos(ang), jnp.sin(ang)
    first = (jnp.arange(HEAD_DIM) % half) < (half // 2)
    pad = ((0, 0), (0, LANES - HEAD_DIM))
    return (jnp.pad(cos, pad), jnp.pad(jnp.where(first, -sin, 0.0), pad), jnp.pad(jnp.where(first, 0.0, sin), pad))


def _alibi_slopes():
    s = 2.0 ** (-8.0 * jnp.arange(1, A_HEADS + 1, dtype=F32) / A_HEADS)
    return jnp.broadcast_to(s[:, None, None], (A_HEADS, 1, LANES))


def _trunk(x, layers, final_norm_w):
    Bsz, L, _ = x.shape
    T = Bsz * L
    tm = ROW_TILE
    n_chunks = L // SSM_CHUNK
    pos_blocks = L // tm
    cos, sa, sb = _rope_tables(L)
    slopes = _alibi_slopes()
    fw = final_norm_w[None, :]
    x2d = x.reshape(T, D_MODEL)

    for l, lp in enumerate(layers):
        qkv_a, qkv_b, zc32, zc16 = _project(x2d, lp, (cos, sa, sb), pos_blocks)
        out_a = _diff_attention(qkv_a.reshape(Bsz, L, -1), slopes, lp['lamv'], lp['subw'], lp['lam_init'], Bsz, L)
        out_b = _gqa_attention(qkv_b.reshape(Bsz, L, -1), lp['shift_b'], Bsz, L)

        u_g = jnp.transpose(zc16.reshape(Bsz, n_chunks, SSM_CHUNK, SSM_GROUPS, SSM_P), (3, 0, 1, 2, 4))
        u_g = u_g.reshape(SSM_GROUPS, Bsz * n_chunks, SSM_FLAT)
        y_g = _ssm_apply(u_g, *lp['ssm'], Bsz, n_chunks)
        y_s = jnp.transpose(y_g.reshape(SSM_GROUPS, Bsz, n_chunks, SSM_CHUNK, SSM_P), (1, 2, 3, 0, 4))
        y_s = y_s.reshape(T, SSM_WIDTH)

        x1, xn2, comb = _merge(x2d, out_a.reshape(T, A_WIDTH), out_b.reshape(T, B_HEADS * LANES), y_s, zc32,
                               lp['merge_w'])
        x2d = _moe(xn2, comb, x1, lp['wg'], lp['wu'], lp['wd'], fw, final_norm=(l == len(layers) - 1))
    return x2d.reshape(Bsz, L, D_MODEL)


def kernel(x_prompt, x_sample, norm1_w, w_in, lam_q1, lam_k1, lam_q2, lam_k2, diff_subln_w, q_norm_w, k_norm_w,
           ssm_A_re, ssm_A_im, ssm_log_dt, ssm_B_re, ssm_B_im, ssm_C_re, ssm_C_im, ssm_D, w_glu, b_glu,
           w_proj_a, w_proj_b, w_proj_c, w_out, norm2_w, w_router_group, b_router_group, w_router_expert,
           b_router_expert, w_exp_gate, w_exp_up, w_exp_down, final_norm_w):
    p = dict(norm1_w=norm1_w, w_in=w_in, lam_q1=lam_q1, lam_k1=lam_k1, lam_q2=lam_q2, lam_k2=lam_k2,
             diff_subln_w=diff_subln_w, q_norm_w=q_norm_w, k_norm_w=k_norm_w, ssm_A_re=ssm_A_re, ssm_A_im=ssm_A_im,
             ssm_log_dt=ssm_log_dt, ssm_B_re=ssm_B_re, ssm_B_im=ssm_B_im, ssm_C_re=ssm_C_re, ssm_C_im=ssm_C_im,
             ssm_D=ssm_D, w_glu=w_glu, b_glu=b_glu, w_proj_a=w_proj_a, w_proj_b=w_proj_b, w_proj_c=w_proj_c,
             w_out=w_out, norm2_w=norm2_w, w_router_group=w_router_group, b_router_group=b_router_group,
             w_router_expert=w_router_expert, b_router_expert=b_router_expert, w_exp_gate=w_exp_gate,
             w_exp_up=w_exp_up, w_exp_down=w_exp_down)
    layers = [_layer_params(l, p) for l in range(DEPTH)]
    return (_trunk(x_prompt, layers, final_norm_w), _trunk(x_sample, layers, final_norm_w))
```

```python
import functools
import math

import jax
import jax.numpy as jnp
from jax import lax
from jax.experimental import pallas as pl
from jax.experimental.pallas import tpu as pltpu

F32 = jnp.float32
BF16 = jnp.bfloat16

D_MODEL = 1024
DEPTH = 2
HEAD_DIM = 64
EPS = 1e-6
A_HEADS = 4
A_VDIM = 2 * HEAD_DIM
A_QK_COLS = A_HEADS * 2 * HEAD_DIM
A_WIDTH = A_HEADS * A_VDIM
COLS_A = 2 * A_QK_COLS + A_WIDTH
B_HEADS = 8
B_KV_HEADS = 2
B_GROUP = B_HEADS // B_KV_HEADS
B_WIDTH = B_HEADS * HEAD_DIM
COLS_B = B_HEADS * HEAD_DIM + 2 * B_KV_HEADS * HEAD_DIM
ROPE_BASE = 10000.0
GRID_W = 64
SSM_WIDTH = 512
SSM_P = 16
SSM_GROUPS = SSM_WIDTH // SSM_P
SSM_N = 64
COLS_C = SSM_WIDTH
N_BRANCH = 3
MOE_GROUPS = 4
EXPERTS_PER_GROUP = 4
N_EXPERTS = MOE_GROUPS * EXPERTS_PER_GROUP
D_FF_EXPERT = 512

LANES = 128
SUBLANES = 8
VMEM_LIMIT_BYTES = 56 * 1024 * 1024

SSM_CHUNK = 64
SSM_FLAT = SSM_CHUNK * SSM_P
SSM_STATE_COLS = 4 * LANES

ROW_TILE = 512
MERGE_TILE = 512
EXPERT_TILE = 1024
ATT_ROWS = 1024
ATT_KV_TILE = 4096
MAX_STATIC_SHIFT = 40.0
SHIFT_MARGIN = 1.02


def _compiler_params(semantics):
    return pltpu.CompilerParams(dimension_semantics=semantics, vmem_limit_bytes=VMEM_LIMIT_BYTES)


def _const_spec(shape):
    zeros = (0,) * len(shape)
    return pl.BlockSpec(shape, lambda *_: zeros)


def _rms(x, w):
    ms = jnp.mean(x * x, axis=-1, keepdims=True)
    return x * lax.rsqrt(ms + EPS) * w


def _proj_kernel(x_ref, nw_ref, wa_ref, wb_ref, wc_ref, ea_ref, qw_ref, kw_ref, eq_ref, ek_ref, ev_ref,
                 cos_ref, sa_ref, sb_ref, oa_ref, ob_ref, oc32_ref, oc16_ref):
    xn = _rms(x_ref[...], nw_ref[...]).astype(BF16)

    oa_ref[...] = (jnp.dot(xn, wa_ref[...], preferred_element_type=F32) + ea_ref[...]).astype(oa_ref.dtype)

    zc = jnp.dot(xn, wc_ref[...], preferred_element_type=F32)
    oc32_ref[...] = zc
    oc16_ref[...] = zc.astype(BF16)

    zb = jnp.dot(xn, wb_ref[...], preferred_element_type=F32)
    cos, sa, sb = cos_ref[...], sa_ref[...], sb_ref[...]

    def norm_rope(zh, w):
        ms = jnp.sum(zh * zh, axis=-1, keepdims=True) * (1.0 / HEAD_DIM)
        y = zh * lax.rsqrt(ms + EPS) * w
        return y * cos + pltpu.roll(y, LANES - 16, 1) * sa + pltpu.roll(y, 16, 1) * sb

    for h in range(B_HEADS + B_KV_HEADS):
        sl = slice(h * LANES, (h + 1) * LANES)
        w, e = (qw_ref, eq_ref) if h < B_HEADS else (kw_ref, ek_ref)
        ob_ref[:, sl] = (norm_rope(zb[:, sl], w[...]) + e[...]).astype(ob_ref.dtype)
    sl = slice((B_HEADS + B_KV_HEADS) * LANES, (B_HEADS + 2 * B_KV_HEADS) * LANES)
    ob_ref[:, sl] = (zb[:, sl] + ev_ref[...]).astype(ob_ref.dtype)


def _project(x2d, lp, tables, pos_blocks):
    T = x2d.shape[0]
    tm = ROW_TILE
    na, nb = lp['w_a'].shape[1], lp['w_b'].shape[1]

    def rows(width):
        return pl.BlockSpec((tm, width), lambda i: (i, 0))

    def whole(arr):
        return pl.BlockSpec(arr.shape, lambda i: (0, 0), pipeline_mode=pl.Buffered(1))

    consts = [lp['norm1'], lp['w_a'], lp['w_b'], lp['w_c'], lp['e_a'], lp['qw'], lp['kw'], lp['e_q'], lp['e_k'],
              lp['e_b']]
    table_spec = pl.BlockSpec((tm, LANES), lambda i: (i % pos_blocks, 0))
    return pl.pallas_call(
        _proj_kernel,
        grid=(T // tm,),
        in_specs=[rows(D_MODEL)] + [whole(c) for c in consts] + [table_spec] * 3,
        out_specs=[rows(na), rows(nb), rows(COLS_C), rows(COLS_C)],
        out_shape=[jax.ShapeDtypeStruct((T, na), BF16), jax.ShapeDtypeStruct((T, nb), BF16),
                   jax.ShapeDtypeStruct((T, COLS_C), F32), jax.ShapeDtypeStruct((T, COLS_C), BF16)],
        compiler_params=_compiler_params(("parallel",)),
        name="in_proj",
    )(x2d, *consts, *tables)


_NT = (((1,), (1,)), ((), ()))


def _online_softmax(qs, k_fn, v_ref, m_ref, acc_ref, tk, n_chunks, bias_fn):
    m_ref[...] = jnp.full(m_ref.shape, -jnp.inf, F32)
    acc_ref[...] = jnp.zeros(acc_ref.shape, F32)

    def body(c, carry):
        start = pl.multiple_of(c * tk, tk)
        s = lax.dot_general(qs, k_fn(start), _NT, preferred_element_type=F32)
        if bias_fn is not None:
            s = s + bias_fn(start)
        m_prev = m_ref[...]
        m_new = jnp.maximum(m_prev, jnp.max(s, axis=-1, keepdims=True))
        p = jnp.exp(s - m_new)
        alpha = jnp.exp(m_prev - m_new)
        pv = jnp.dot(p.astype(BF16), v_ref[pl.ds(start, tk), :], preferred_element_type=F32)
        acc_ref[...] = alpha * acc_ref[...] + pv
        m_ref[...] = m_new
        return carry

    lax.fori_loop(0, n_chunks, body, 0)


def _shifted_softmax(qs, k_fn, v_ref, acc_ref, tk, lo, hi, bias_fn=None):
    def body(c, carry):
        start = pl.multiple_of(c * tk, tk)
        s = lax.dot_general(qs, k_fn(start), _NT, preferred_element_type=F32)
        if bias_fn is not None:
            s = s + bias_fn(start)
        acc_ref[...] += jnp.dot(jnp.exp(s).astype(BF16), v_ref[pl.ds(start, tk), :], preferred_element_type=F32)
        return carry

    lax.fori_loop(lo, hi, body, 0)


def _half_norms(x, lane):
    sq = x.astype(F32)
    sq = sq * sq
    n1 = jnp.sum(jnp.where(lane < HEAD_DIM, sq, 0.0), axis=-1, keepdims=True)
    n2 = jnp.sum(jnp.where(lane >= HEAD_DIM, sq, 0.0), axis=-1, keepdims=True)
    return jnp.sqrt(n1), jnp.sqrt(n2)


def _diff_attn_kernel(q_ref, k_ref, v_ref, slope_ref, lam_ref, subw_ref, o_ref, kmax_ref, m_ref, acc_ref,
                      *, tq, tk, lam_init):
    i = pl.program_id(2)
    n_chunks = k_ref.shape[0] // tk
    slope = slope_ref[...][:, :1]

    @pl.when(i == 0)
    def _():
        lane_k = lax.broadcasted_iota(jnp.int32, (tk, LANES), 1)

        def body(c, carry):
            n1, n2 = _half_norms(k_ref[pl.ds(pl.multiple_of(c * tk, tk), tk), :], lane_k)
            return (jnp.maximum(carry[0], jnp.max(n1, axis=0, keepdims=True)),
                    jnp.maximum(carry[1], jnp.max(n2, axis=0, keepdims=True)))

        zero11 = jnp.zeros((1, 1), F32)
        k1, k2 = lax.fori_loop(0, n_chunks, body, (zero11, zero11))
        kmax_ref[0:1, :] = jnp.broadcast_to(k1, (1, LANES))
        kmax_ref[1:2, :] = jnp.broadcast_to(k2, (1, LANES))

    q = q_ref[...]
    lane = lax.broadcasted_iota(jnp.int32, q.shape, 1)
    zero = jnp.zeros_like(q)
    qz = jnp.concatenate([jnp.where(lane < HEAD_DIM, q, zero), jnp.where(lane >= HEAD_DIM, q, zero)], axis=0)

    qn1, qn2 = _half_norms(q, lane)
    shift = jnp.concatenate([qn1 * kmax_ref[0:1, 0:1], qn2 * kmax_ref[1:2, 0:1]], axis=0) * SHIFT_MARGIN
    static_ok = jnp.max(shift) <= MAX_STATIC_SHIFT

    row = lax.broadcasted_iota(jnp.int32, (2 * tq, 1), 0)
    q_slope_pos = (i * tq + jnp.where(row >= tq, row - tq, row)).astype(F32) * slope

    def k_fn(start):
        return k_ref[pl.ds(start, tk), :]

    def alibi(start):
        k_slope_pos = (start + lax.broadcasted_iota(jnp.int32, (1, tk), 1)).astype(F32) * slope
        return jnp.abs(q_slope_pos - k_slope_pos)

    @pl.when(static_ok)
    def _():
        acc_ref[...] = jnp.zeros(acc_ref.shape, F32)
        _shifted_softmax(qz, k_fn, v_ref, acc_ref, tk, 0, n_chunks, lambda start: -shift - alibi(start))

    @pl.when(jnp.logical_not(static_ok))
    def _():
        _online_softmax(qz, k_fn, v_ref, m_ref, acc_ref, tk, n_chunks, lambda start: -alibi(start))

    acc = acc_ref[...]
    o = acc[:, :A_VDIM] / acc[:, A_VDIM:A_VDIM + 1]
    lv = lam_ref[...]
    lam = (jnp.exp(jnp.sum(lv[0:1] * lv[1:2], axis=-1, keepdims=True))
           - jnp.exp(jnp.sum(lv[2:3] * lv[3:4], axis=-1, keepdims=True)) + lam_init)
    d = o[:tq] - lam * o[tq:]
    o_ref[...] = (_rms(d, subw_ref[...]) * (1.0 - lam_init)).astype(o_ref.dtype)


def _gqa_attn_kernel(shift_ref, q_ref, k_ref, v_ref, o_ref, m_ref, acc_ref, *, tq, tk):
    qs = jnp.concatenate([q_ref[:, h * LANES:(h + 1) * LANES] for h in range(B_GROUP)], axis=0)
    n_chunks = k_ref.shape[0] // tk
    static_ok = shift_ref[0] <= MAX_STATIC_SHIFT

    def k_fn(start):
        return k_ref[pl.ds(start, tk), :]

    @pl.when(static_ok)
    def _():
        acc_ref[...] = jnp.zeros(acc_ref.shape, F32)
        _shifted_softmax(qs, k_fn, v_ref, acc_ref, tk, 0, n_chunks)

    @pl.when(jnp.logical_not(static_ok))
    def _():
        _online_softmax(qs, k_fn, v_ref, m_ref, acc_ref, tk, n_chunks, None)

    acc = acc_ref[...]
    o = acc / acc[:, HEAD_DIM:HEAD_DIM + 1]
    for h in range(B_GROUP):
        o_ref[:, h * LANES:(h + 1) * LANES] = o[h * tq:(h + 1) * tq].astype(o_ref.dtype)


def _attn_tiles(L, stacked):
    return min(ATT_ROWS // stacked, L), min(ATT_KV_TILE, L)


def _resident(block_shape, index_map):
    return pl.BlockSpec(block_shape, index_map, pipeline_mode=pl.Buffered(1))


def _diff_attention(qkv, slopes, lamv, subw, lam_init, Bsz, L):
    tq, tk = _attn_tiles(L, 2)
    nq = A_HEADS
    kernel = functools.partial(_diff_attn_kernel, tq=tq, tk=tk, lam_init=lam_init)
    return pl.pallas_call(
        kernel,
        grid=(Bsz, A_HEADS, L // tq),
        in_specs=[
            pl.BlockSpec((None, tq, LANES), lambda b, h, i: (b, i, h)),
            _resident((None, L, LANES), lambda b, h, i: (b, 0, nq + h)),
            _resident((None, L, 2 * LANES), lambda b, h, i: (b, 0, nq + h)),
            pl.BlockSpec((None, 1, LANES), lambda b, h, i: (h, 0, 0)),
            _const_spec((4, LANES)),
            _const_spec((1, A_VDIM)),
        ],
        out_specs=pl.BlockSpec((None, tq, LANES), lambda b, h, i: (b, i, h)),
        out_shape=jax.ShapeDtypeStruct((Bsz, L, A_WIDTH), BF16),
        scratch_shapes=[pltpu.VMEM((SUBLANES, LANES), F32),
                        pltpu.VMEM((2 * tq, 1), F32), pltpu.VMEM((2 * tq, 2 * LANES), F32)],
        compiler_params=_compiler_params(("parallel", "parallel", "arbitrary")),
        name="diff_attn",
    )(qkv, qkv, qkv, slopes, lamv, subw)


def _gqa_attention(qkv, shift, Bsz, L):
    tq, tk = _attn_tiles(L, B_GROUP)
    q_blocks = B_HEADS // B_GROUP
    kernel = functools.partial(_gqa_attn_kernel, tq=tq, tk=tk)
    return pl.pallas_call(
        kernel,
        grid=(Bsz, B_KV_HEADS, L // tq),
        in_specs=[
            pl.BlockSpec(memory_space=pltpu.SMEM),
            pl.BlockSpec((None, tq, B_GROUP * LANES), lambda b, g, i: (b, i, g)),
            _resident((None, L, LANES), lambda b, g, i: (b, 0, B_HEADS + g)),
            _resident((None, L, LANES), lambda b, g, i: (b, 0, B_HEADS + B_KV_HEADS + g)),
        ],
        out_specs=pl.BlockSpec((None, tq, B_GROUP * LANES), lambda b, g, i: (b, i, g)),
        out_shape=jax.ShapeDtypeStruct((Bsz, L, q_blocks * B_GROUP * LANES), BF16),
        scratch_shapes=[pltpu.VMEM((B_GROUP * tq, 1), F32), pltpu.VMEM((B_GROUP * tq, LANES), F32)],
        compiler_params=_compiler_params(("parallel", "parallel", "arbitrary")),
        name="gqa_attn",
    )(shift, qkv, qkv, qkv)


def _ssm_kernel(u_ref, m_ref, wst_ref, wout_ref, a_ref, y_ref, s_ref, h_ref, *, n_batch, n_chunks):
    u = u_ref[...]
    y_ref[...] = jnp.dot(u, m_ref[...], preferred_element_type=F32)
    s_ref[...] = jnp.dot(u, wst_ref[...], preferred_element_type=F32)
    a = a_ref[...]
    afr, afi = a[:, 0:LANES], a[:, LANES:2 * LANES]
    abr, abi = a[:, 2 * LANES:3 * LANES], a[:, 3 * LANES:4 * LANES]
    zero = jnp.zeros((1, LANES), F32)

    def step(cg, carry):
        new = []
        for b in range(n_batch):
            hr, hi, gr, gi = carry[4 * b:4 * b + 4]
            rf = pl.multiple_of(b * n_chunks + cg * SUBLANES, SUBLANES)
            rb = pl.multiple_of(b * n_chunks + n_chunks - SUBLANES - cg * SUBLANES, SUBLANES)
            sf = s_ref[pl.ds(rf, SUBLANES), 0:2 * LANES]
            sb = s_ref[pl.ds(rb, SUBLANES), 2 * LANES:4 * LANES]
            rows_f, rows_b = [], [None] * SUBLANES
            for j in range(SUBLANES):
                rows_f.append(jnp.concatenate([hr, hi], axis=1))
                hr, hi = (afr * hr - afi * hi + sf[j:j + 1, 0:LANES],
                          afr * hi + afi * hr + sf[j:j + 1, LANES:2 * LANES])
                jb = SUBLANES - 1 - j
                rows_b[jb] = jnp.concatenate([gr, gi], axis=1)
                gr, gi = (abr * gr - abi * gi + sb[jb:jb + 1, 0:LANES],
                          abr * gi + abi * gr + sb[jb:jb + 1, LANES:2 * LANES])
            h_ref[pl.ds(rf, SUBLANES), 0:2 * LANES] = jnp.concatenate(rows_f, axis=0)
            h_ref[pl.ds(rb, SUBLANES), 2 * LANES:4 * LANES] = jnp.concatenate(rows_b, axis=0)
            new += [hr, hi, gr, gi]
        return tuple(new)

    lax.fori_loop(0, n_chunks // SUBLANES, step, (zero,) * (4 * n_batch))
    y_ref[...] += jnp.dot(h_ref[...].astype(BF16), wout_ref[...], preferred_element_type=F32)


def _ssm_apply(u_g, m, wst, wout, a64, n_batch, n_chunks):
    nc = n_batch * n_chunks
    kernel = functools.partial(_ssm_kernel, n_batch=n_batch, n_chunks=n_chunks)
    return pl.pallas_call(
        kernel,
        grid=(SSM_GROUPS,),
        in_specs=[
            pl.BlockSpec((None, nc, SSM_FLAT), lambda g: (g, 0, 0)),
            pl.BlockSpec((None, SSM_FLAT, SSM_FLAT), lambda g: (g, 0, 0)),
            pl.BlockSpec((None, SSM_FLAT, SSM_STATE_COLS), lambda g: (g, 0, 0)),
            pl.BlockSpec((None, SSM_STATE_COLS, SSM_FLAT), lambda g: (g, 0, 0)),
            pl.BlockSpec((None, 1, SSM_STATE_COLS), lambda g: (g, 0, 0)),
        ],
        out_specs=pl.BlockSpec((None, nc, SSM_FLAT), lambda g: (g, 0, 0)),
        out_shape=jax.ShapeDtypeStruct((SSM_GROUPS, nc, SSM_FLAT), F32),
        scratch_shapes=[pltpu.VMEM((nc, SSM_STATE_COLS), F32), pltpu.VMEM((nc, SSM_STATE_COLS), F32)],
        compiler_params=_compiler_params(("parallel",)),
        name="ssm_chunks",
    )(u_g, m, wst, wout, a64)


def _ssm_operators(A_re, A_im, log_dt, B_re, B_im, C_re, C_im):
    hi = lax.Precision.HIGHEST
    G, N, P, Tc = SSM_GROUPS, SSM_N, SSM_P, SSM_CHUNK
    Br, Bi, Cr, Ci = B_re.astype(F32), B_im.astype(F32), C_re.astype(F32), C_im.astype(F32)
    steps = jnp.arange(Tc + 1, dtype=F32)

    per_dir = []
    for dirn in range(2):
        dt = jnp.exp(log_dt[dirn].astype(F32))[:, None]
        ar, ai = A_re[dirn].astype(F32), A_im[dirn].astype(F32)
        mag = jnp.exp(dt * ar)
        er, ei = mag * jnp.cos(dt * ai), mag * jnp.sin(dt * ai)
        den = ar * ar + ai * ai
        fr = ((er - 1.0) * ar + ei * ai) / den
        fi = (ei * ar - (er - 1.0) * ai) / den
        bbr = fr[..., None] * Br - fi[..., None] * Bi
        bbi = fr[..., None] * Bi + fi[..., None] * Br
        pm = jnp.exp(steps[None, :, None] * (dt * ar)[:, None, :])
        ang = steps[None, :, None] * (dt * ai)[:, None, :]
        pr, pi = pm * jnp.cos(ang), pm * jnp.sin(ang)
        car = Cr[:, None] * pr[:, :, None, :] - Ci[:, None] * pi[:, :, None, :]
        cai = Cr[:, None] * pi[:, :, None, :] + Ci[:, None] * pr[:, :, None, :]
        kern = (jnp.einsum('gkpn,gnq->gkpq', car[:, :Tc], bbr, precision=hi)
                - jnp.einsum('gkpn,gnq->gkpq', cai[:, :Tc], bbi, precision=hi))
        abr = pr[..., None] * bbr[:, None] - pi[..., None] * bbi[:, None]
        abi = pr[..., None] * bbi[:, None] + pi[..., None] * bbr[:, None]
        per_dir.append(dict(kern=kern, car=car, cai=cai, abr=abr, abi=abi, pr=pr, pi=pi))

    f, b = per_dir
    k2 = jnp.concatenate([b['kern'][:, :0:-1], f['kern'][:, :1] + b['kern'][:, :1], f['kern'][:, 1:]], axis=1)
    k2 = jnp.transpose(k2, (0, 3, 1, 2)).astype(BF16).reshape(G, P, (2 * Tc - 1) * P)
    m = jnp.stack([k2[:, :, (Tc - 1 - s) * P:(2 * Tc - 1 - s) * P] for s in range(Tc)], axis=1)
    m = m.reshape(G, Tc * P, Tc * P)

    pad = jnp.zeros((G, Tc * P, LANES - N), F32)

    def st_cols(xr):
        return jnp.concatenate([jnp.transpose(xr, (0, 1, 3, 2)).reshape(G, Tc * P, N), pad], axis=-1)

    wst = jnp.concatenate([st_cols(f['abr'][:, Tc - 1::-1][:, :Tc]), st_cols(f['abi'][:, Tc - 1::-1][:, :Tc]),
                           st_cols(b['abr'][:, :Tc]), st_cols(b['abi'][:, :Tc])], axis=-1)

    padr = jnp.zeros((G, LANES - N, Tc * P), F32)

    def out_rows(x):
        return jnp.concatenate([jnp.transpose(x, (0, 3, 1, 2)).reshape(G, N, Tc * P), padr], axis=1)

    wout = jnp.concatenate([out_rows(f['car'][:, 1:Tc + 1]), out_rows(-f['cai'][:, 1:Tc + 1]),
                            out_rows(b['car'][:, Tc:0:-1]), out_rows(-b['cai'][:, Tc:0:-1])], axis=1)

    padc = jnp.zeros((G, LANES - N), F32)
    a64 = jnp.concatenate([f['pr'][:, Tc], padc, f['pi'][:, Tc], padc,
                           b['pr'][:, Tc], padc, b['pi'][:, Tc], padc], axis=-1)[:, None, :]
    return m.astype(BF16), wst.astype(BF16), wout.astype(BF16), a64


def _merge_kernel(x_ref, oa_ref, ob_ref, ys_ref, zc_ref, n1_ref, wg_ref, wpa_ref, wpb_ref, wpc_ref,
                  wglu_ref, bglu_ref, dskip_ref, wout_ref, n2_ref, wr_hi_ref, wr_lo_ref, br_ref,
                  x1_ref, xn2_ref, comb_ref):
    x = x_ref[...]
    xn = _rms(x, n1_ref[...]).astype(BF16)
    gates = jax.nn.sigmoid(jnp.dot(xn, wg_ref[...], preferred_element_type=F32))

    y = ys_ref[...] + dskip_ref[...] * zc_ref[...]
    g = jax.nn.gelu(y)
    glu = jnp.dot(g.astype(BF16), wglu_ref[...], preferred_element_type=F32) + bglu_ref[...]
    out_c = g * jax.nn.sigmoid(glu)

    merged = (gates[:, 0:D_MODEL] * jnp.dot(oa_ref[...], wpa_ref[...], preferred_element_type=F32)
              + gates[:, D_MODEL:2 * D_MODEL] * jnp.dot(ob_ref[...], wpb_ref[...], preferred_element_type=F32)
              + gates[:, 2 * D_MODEL:3 * D_MODEL]
              * jnp.dot(out_c.astype(BF16), wpc_ref[...], preferred_element_type=F32))
    x1 = x + jnp.dot(merged.astype(BF16), wout_ref[...], preferred_element_type=F32)
    x1_ref[...] = x1

    xn2 = _rms(x1, n2_ref[...])
    xn2_ref[...] = xn2.astype(BF16)

    lane = lax.broadcasted_iota(jnp.int32, (x.shape[0], LANES), 1).astype(F32)
    neg = jnp.float32(-jnp.inf)
    big = jnp.float32(LANES)
    x_hi = xn2.astype(BF16)
    x_lo = (xn2 - x_hi.astype(F32)).astype(BF16)
    le = (jnp.dot(x_hi, wr_hi_ref[...], preferred_element_type=F32)
          + (jnp.dot(x_hi, wr_lo_ref[...], preferred_element_type=F32)
             + jnp.dot(x_lo, wr_hi_ref[...], preferred_element_type=F32))) + br_ref[...]
    lg = jnp.where((lane >= N_EXPERTS) & (lane < N_EXPERTS + MOE_GROUPS), le, neg)
    g_max = jnp.max(lg, axis=-1, keepdims=True)
    g_idx = jnp.min(jnp.where(lg == g_max, lane, big), axis=-1, keepdims=True) - float(N_EXPERTS)
    g_w = 1.0 / jnp.sum(jnp.exp(lg - g_max), axis=-1, keepdims=True)
    in_group = (lane >= g_idx * EXPERTS_PER_GROUP) & (lane < (g_idx + 1) * EXPERTS_PER_GROUP)
    v = jnp.where(in_group, le, neg)
    top1 = jnp.max(v, axis=-1, keepdims=True)
    i1 = jnp.min(jnp.where(v == top1, lane, big), axis=-1, keepdims=True)
    v2 = jnp.where(lane == i1, neg, v)
    top2 = jnp.max(v2, axis=-1, keepdims=True)
    i2 = jnp.min(jnp.where(v2 == top2, lane, big), axis=-1, keepdims=True)
    e2 = jnp.exp(top2 - top1)
    inv = 1.0 / (1.0 + e2)
    comb_ref[...] = (jnp.where(lane == i1, inv * g_w, 0.0) + jnp.where(lane == i2, e2 * inv * g_w, 0.0))


def _merge(x2d, oa, ob, ys, zc, weights):
    T = x2d.shape[0]
    tm = MERGE_TILE

    def rows(width):
        return pl.BlockSpec((tm, width), lambda i: (i, 0))

    w_specs = [pl.BlockSpec(w.shape, lambda i, nd=w.ndim: (0,) * nd, pipeline_mode=pl.Buffered(1)) for w in weights]
    return pl.pallas_call(
        _merge_kernel,
        grid=(T // tm,),
        in_specs=[rows(D_MODEL), rows(A_WIDTH), rows(B_HEADS * LANES), rows(SSM_WIDTH), rows(SSM_WIDTH)] + w_specs,
        out_specs=[rows(D_MODEL), rows(D_MODEL), rows(LANES)],
        out_shape=[jax.ShapeDtypeStruct((T, D_MODEL), F32), jax.ShapeDtypeStruct((T, D_MODEL), BF16),
                   jax.ShapeDtypeStruct((T, LANES), F32)],
        compiler_params=_compiler_params(("parallel",)),
        name="merge_router",
    )(x2d, oa, ob, ys, zc, *weights)


def _moe_kernel(xn_ref, comb_ref, x1_ref, wg_ref, wu_ref, wd_ref, fw_ref, o_ref, acc_ref, *, final_norm):
    e = pl.program_id(1)

    @pl.when(e == 0)
    def _():
        acc_ref[...] = jnp.zeros(acc_ref.shape, F32)

    xn = xn_ref[...]
    h = jax.nn.silu(jnp.dot(xn, wg_ref[...], preferred_element_type=F32)) * jnp.dot(
        xn, wu_ref[...], preferred_element_type=F32)
    y = jnp.dot(h.astype(BF16), wd_ref[...], preferred_element_type=F32)
    comb = comb_ref[...]
    lane = lax.broadcasted_iota(jnp.int32, comb.shape, 1)
    c = jnp.sum(jnp.where(lane == e, comb, 0.0), axis=-1, keepdims=True)
    acc_ref[...] += c * y

    @pl.when(e == N_EXPERTS - 1)
    def _():
        out = x1_ref[...] + acc_ref[...]
        if final_norm:
            out = _rms(out, fw_ref[...])
        o_ref[...] = out


def _moe(xn2, comb, x1, wg, wu, wd, fw, final_norm):
    T = x1.shape[0]
    tm = EXPERT_TILE
    kernel = functools.partial(_moe_kernel, final_norm=final_norm)
    return pl.pallas_call(
        kernel,
        grid=(T // tm, N_EXPERTS),
        in_specs=[
            pl.BlockSpec((tm, D_MODEL), lambda i, e: (i, 0)),
            pl.BlockSpec((tm, LANES), lambda i, e: (i, 0)),
            pl.BlockSpec((tm, D_MODEL), lambda i, e: (i, 0)),
            pl.BlockSpec((None, D_MODEL, D_FF_EXPERT), lambda i, e: (e, 0, 0)),
            pl.BlockSpec((None, D_MODEL, D_FF_EXPERT), lambda i, e: (e, 0, 0)),
            pl.BlockSpec((None, D_FF_EXPERT, D_MODEL), lambda i, e: (e, 0, 0)),
            _const_spec((1, D_MODEL)),
        ],
        out_specs=pl.BlockSpec((tm, D_MODEL), lambda i, e: (i, 0)),
        out_shape=jax.ShapeDtypeStruct((T, D_MODEL), F32),
        scratch_shapes=[pltpu.VMEM((tm, D_MODEL), F32)],
        compiler_params=_compiler_params(("parallel", "arbitrary")),
        name="experts",
    )(xn2, comb, x1, wg, wu, wd, fw)


def _pad_heads(w, n_heads, width):
    w = w.reshape(D_MODEL, n_heads, -1)
    return jnp.pad(w, ((0, 0), (0, 0), (0, width - w.shape[-1]))).reshape(D_MODEL, n_heads * width)


def _pad_vec(v, width=LANES):
    return jnp.pad(v.astype(F32), (0, width - v.shape[0]))[None, :]


def _layer_params(l, p):
    scale = HEAD_DIM ** -0.5
    w_in = p['w_in'][l]
    wa, wb, wc, wg = jnp.split(w_in, [COLS_A, COLS_A + COLS_B, COLS_A + COLS_B + COLS_C], axis=-1)

    wqa, wka, wva = jnp.split(wa, [A_QK_COLS, 2 * A_QK_COLS], axis=-1)
    wva = _pad_heads(wva, A_HEADS, 2 * LANES)
    w_a = jnp.concatenate([wqa * scale, wka, wva], axis=-1).astype(BF16)
    ones_a = jnp.zeros((A_HEADS, 2 * LANES), F32).at[:, A_VDIM].set(1.0).reshape(1, -1)
    e_a = jnp.concatenate([jnp.zeros((1, 2 * A_QK_COLS), F32), ones_a], axis=-1)

    wqb, wkb, wvb = jnp.split(wb, [B_HEADS * HEAD_DIM, (B_HEADS + B_KV_HEADS) * HEAD_DIM], axis=-1)
    w_b = jnp.concatenate([_pad_heads(wqb, B_HEADS, LANES), _pad_heads(wkb, B_KV_HEADS, LANES),
                           _pad_heads(wvb, B_KV_HEADS, LANES)], axis=-1).astype(BF16)
    e_b = jnp.zeros((B_KV_HEADS, LANES), F32).at[:, HEAD_DIM].set(1.0).reshape(1, -1)
    qw = p['q_norm_w'][l].astype(F32) * scale
    kw = p['k_norm_w'][l].astype(F32)
    shift_b = (HEAD_DIM * SHIFT_MARGIN) * jnp.max(jnp.abs(qw)) * jnp.max(jnp.abs(kw))
    e_q = jnp.zeros((1, LANES), F32).at[0, HEAD_DIM].set(1.0)
    e_k = jnp.zeros((1, LANES), F32).at[0, HEAD_DIM].set(-shift_b)

    wpb = p['w_proj_b'][l].reshape(B_HEADS, HEAD_DIM, D_MODEL)
    wpb = jnp.pad(wpb, ((0, 0), (0, LANES - HEAD_DIM), (0, 0))).reshape(B_HEADS * LANES, D_MODEL)

    w_r = jnp.pad(jnp.concatenate([p['w_router_expert'][l], p['w_router_group'][l]], axis=-1).astype(F32),
                  ((0, 0), (0, LANES - N_EXPERTS - MOE_GROUPS)))
    w_r_hi = w_r.astype(BF16)
    merge_w = [
        p['norm1_w'][l][None, :], wg.astype(BF16), p['w_proj_a'][l].astype(BF16), wpb.astype(BF16),
        p['w_proj_c'][l].astype(BF16), p['w_glu'][l].astype(BF16), p['b_glu'][l][None, :], p['ssm_D'][l][None, :],
        p['w_out'][l].astype(BF16), p['norm2_w'][l][None, :],
        w_r_hi, (w_r - w_r_hi.astype(F32)).astype(BF16),
        _pad_vec(jnp.concatenate([p['b_router_expert'][l], p['b_router_group'][l]])),
    ]
    lam_init = 0.8 - 0.6 * math.exp(-0.3 * l)
    lamv = jnp.stack([_pad_vec(v[l])[0] for v in (p['lam_q1'], p['lam_k1'], p['lam_q2'], p['lam_k2'])])
    return dict(
        norm1=p['norm1_w'][l][None, :], w_a=w_a, e_a=e_a, w_b=w_b, e_b=e_b, w_c=wc.astype(BF16),
        qw=_pad_vec(qw), kw=_pad_vec(kw), e_q=e_q, e_k=e_k, shift_b=shift_b.reshape(1),
        lamv=lamv, lam_init=lam_init, subw=p['diff_subln_w'][l][None, :],
        ssm=_ssm_operators(p['ssm_A_re'][l], p['ssm_A_im'][l], p['ssm_log_dt'][l], p['ssm_B_re'][l],
                           p['ssm_B_im'][l], p['ssm_C_re'][l], p['ssm_C_im'][l]),
        merge_w=merge_w,
        wg=p['w_exp_gate'][l].astype(BF16), wu=p['w_exp_up'][l].astype(BF16), wd=p['w_exp_down'][l].astype(BF16),
    )


def _rope_tables(L):
    half = HEAD_DIM // 2
    inv = ROPE_BASE ** (-jnp.arange(0, half, 2, dtype=F32) / half)
    t = jnp.arange(L)
    row = (t // GRID_W).astype(F32)
    col = (t % GRID_W).astype(F32)
    ang = jnp.concatenate([row[:, None] * inv[None, :]] * 2 + [col[:, None] * inv[None, :]] * 2, axis=-1)
    cos, sin = jnp.cos(ang), jnp.sin(ang)
    first = (jnp.arange(HEAD_DIM) % half) < (half // 2)
    pad = ((0, 0), (0, LANES - HEAD_DIM))
    return (jnp.pad(cos, pad), jnp.pad(jnp.where(first, -sin, 0.0), pad), jnp.pad(jnp.where(first, 0.0, sin), pad))


def _alibi_slopes():
    s = 2.0 ** (-8.0 * jnp.arange(1, A_HEADS + 1, dtype=F32) / A_HEADS)
    return jnp.broadcast_to(s[:, None, None], (A_HEADS, 1, LANES))


def _trunk(x, layers, final_norm_w):
    Bsz, L, _ = x.shape
    T = Bsz * L
    tm = ROW_TILE
    n_chunks = L // SSM_CHUNK
    pos_blocks = L // tm
    cos, sa, sb = _rope_tables(L)
    slopes = _alibi_slopes()
    fw = final_norm_w[None, :]
    x2d = x.reshape(T, D_MODEL)

    for l, lp in enumerate(layers):
        qkv_a, qkv_b, zc32, zc16 = _project(x2d, lp, (cos, sa, sb), pos_blocks)
        out_a = _diff_attention(qkv_a.reshape(Bsz, L, -1), slopes, lp['lamv'], lp['subw'], lp['lam_init'], Bsz, L)
        out_b = _gqa_attention(qkv_b.reshape(Bsz, L, -1), lp['shift_b'], Bsz, L)

        u_g = jnp.transpose(zc16.reshape(Bsz, n_chunks, SSM_CHUNK, SSM_GROUPS, SSM_P), (3, 0, 1, 2, 4))
        u_g = u_g.reshape(SSM_GROUPS, Bsz * n_chunks, SSM_FLAT)
        y_g = _ssm_apply(u_g, *lp['ssm'], Bsz, n_chunks)
        y_s = jnp.transpose(y_g.reshape(SSM_GROUPS, Bsz, n_chunks, SSM_CHUNK, SSM_P), (1, 2, 3, 0, 4))
        y_s = y_s.reshape(T, SSM_WIDTH)

        x1, xn2, comb = _merge(x2d, out_a.reshape(T, A_WIDTH), out_b.reshape(T, B_HEADS * LANES), y_s, zc32,
                               lp['merge_w'])
        x2d = _moe(xn2, comb, x1, lp['wg'], lp['wu'], lp['wd'], fw, final_norm=(l == len(layers) - 1))
    return x2d.reshape(Bsz, L, D_MODEL)


def kernel(x_prompt, x_sample, norm1_w, w_in, lam_q1, lam_k1, lam_q2, lam_k2, diff_subln_w, q_norm_w, k_norm_w,
           ssm_A_re, ssm_A_im, ssm_log_dt, ssm_B_re, ssm_B_im, ssm_C_re, ssm_C_im, ssm_D, w_glu, b_glu,
           w_proj_a, w_proj_b, w_proj_c, w_out, norm2_w, w_router_group, b_router_group, w_router_expert,
           b_router_expert, w_exp_gate, w_exp_up, w_exp_down, final_norm_w):
    p = dict(norm1_w=norm1_w, w_in=w_in, lam_q1=lam_q1, lam_k1=lam_k1, lam_q2=lam_q2, lam_k2=lam_k2,
             diff_subln_w=diff_subln_w, q_norm_w=q_norm_w, k_norm_w=k_norm_w, ssm_A_re=ssm_A_re, ssm_A_im=ssm_A_im,
             ssm_log_dt=ssm_log_dt, ssm_B_re=ssm_B_re, ssm_B_im=ssm_B_im, ssm_C_re=ssm_C_re, ssm_C_im=ssm_C_im,
             ssm_D=ssm_D, w_glu=w_glu, b_glu=b_glu, w_proj_a=w_proj_a, w_proj_b=w_proj_b, w_proj_c=w_proj_c,
             w_out=w_out, norm2_w=norm2_w, w_router_group=w_router_group, b_router_group=b_router_group,
             w_router_expert=w_router_expert, b_router_expert=b_router_expert, w_exp_gate=w_exp_gate,
             w_exp_up=w_exp_up, w_exp_down=w_exp_down)
    layers = [_layer_params(l, p) for l in range(DEPTH)]
    return (_trunk(x_prompt, layers, final_norm_w), _trunk(x_sample, layers, final_norm_w))
```

```python
import functools
import math

import jax
import jax.numpy as jnp
from jax import lax
from jax.experimental import pallas as pl
from jax.experimental.pallas import tpu as pltpu

F32 = jnp.float32
BF16 = jnp.bfloat16

D_MODEL = 1024
DEPTH = 2
HEAD_DIM = 64
EPS = 1e-6
A_HEADS = 4
A_VDIM = 2 * HEAD_DIM
A_QK_COLS = A_HEADS * 2 * HEAD_DIM
A_WIDTH = A_HEADS * A_VDIM
COLS_A = 2 * A_QK_COLS + A_WIDTH
B_HEADS = 8
B_KV_HEADS = 2
B_GROUP = B_HEADS // B_KV_HEADS
B_WIDTH = B_HEADS * HEAD_DIM
COLS_B = B_HEADS * HEAD_DIM + 2 * B_KV_HEADS * HEAD_DIM
ROPE_BASE = 10000.0
GRID_W = 64
SSM_WIDTH = 512
SSM_P = 16
SSM_GROUPS = SSM_WIDTH // SSM_P
SSM_N = 64
COLS_C = SSM_WIDTH
N_BRANCH = 3
MOE_GROUPS = 4
EXPERTS_PER_GROUP = 4
N_EXPERTS = MOE_GROUPS * EXPERTS_PER_GROUP
D_FF_EXPERT = 512

LANES = 128
SUBLANES = 8
VMEM_LIMIT_BYTES = 56 * 1024 * 1024

SSM_CHUNK = 64
SSM_FLAT = SSM_CHUNK * SSM_P
SSM_STATE_COLS = 4 * LANES
SSM_SHIFTS = LANES // SSM_P
SSM_K2_WIDTH = (SSM_CHUNK - 1) // SSM_SHIFTS * LANES + SSM_FLAT

ROW_TILE = 512
MERGE_TILE = 512
EXPERT_TILE = 1024
ATT_ROWS = 1024
ATT_KV_TILE = 4096
MAX_STATIC_SHIFT = 40.0
SHIFT_MARGIN = 1.02


def _compiler_params(semantics):
    return pltpu.CompilerParams(dimension_semantics=semantics, vmem_limit_bytes=VMEM_LIMIT_BYTES)


def _const_spec(shape):
    zeros = (0,) * len(shape)
    return pl.BlockSpec(shape, lambda *_: zeros)


def _rms(x, w):
    ms = jnp.mean(x * x, axis=-1, keepdims=True)
    return x * lax.rsqrt(ms + EPS) * w


def _proj_kernel(x_ref, nw_ref, wa_ref, wb_ref, wc_ref, ea_ref, qw_ref, kw_ref, eq_ref, ek_ref, ev_ref,
                 cos_ref, sa_ref, sb_ref, oa_ref, ob_ref, oc32_ref, oc16_ref):
    xn = _rms(x_ref[...], nw_ref[...]).astype(BF16)

    oa_ref[...] = (jnp.dot(xn, wa_ref[...], preferred_element_type=F32) + ea_ref[...]).astype(oa_ref.dtype)

    zc = jnp.dot(xn, wc_ref[...], preferred_element_type=F32)
    oc32_ref[...] = zc
    oc16_ref[...] = zc.astype(BF16)

    zb = jnp.dot(xn, wb_ref[...], preferred_element_type=F32)
    cos, sa, sb = cos_ref[...], sa_ref[...], sb_ref[...]

    def norm_rope(zh, w):
        ms = jnp.sum(zh * zh, axis=-1, keepdims=True) * (1.0 / HEAD_DIM)
        y = zh * lax.rsqrt(ms + EPS) * w
        return y * cos + pltpu.roll(y, LANES - 16, 1) * sa + pltpu.roll(y, 16, 1) * sb

    for h in range(B_HEADS + B_KV_HEADS):
        sl = slice(h * LANES, (h + 1) * LANES)
        w, e = (qw_ref, eq_ref) if h < B_HEADS else (kw_ref, ek_ref)
        ob_ref[:, sl] = (norm_rope(zb[:, sl], w[...]) + e[...]).astype(ob_ref.dtype)
    sl = slice((B_HEADS + B_KV_HEADS) * LANES, (B_HEADS + 2 * B_KV_HEADS) * LANES)
    ob_ref[:, sl] = (zb[:, sl] + ev_ref[...]).astype(ob_ref.dtype)


def _project(x2d, lp, tables, pos_blocks):
    T = x2d.shape[0]
    tm = ROW_TILE
    na, nb = lp['w_a'].shape[1], lp['w_b'].shape[1]

    def rows(width):
        return pl.BlockSpec((tm, width), lambda i: (i, 0))

    def whole(arr):
        return pl.BlockSpec(arr.shape, lambda i: (0, 0), pipeline_mode=pl.Buffered(1))

    consts = [lp['norm1'], lp['w_a'], lp['w_b'], lp['w_c'], lp['e_a'], lp['qw'], lp['kw'], lp['e_q'], lp['e_k'],
              lp['e_b']]
    table_spec = pl.BlockSpec((tm, LANES), lambda i: (i % pos_blocks, 0))
    return pl.pallas_call(
        _proj_kernel,
        grid=(T // tm,),
        in_specs=[rows(D_MODEL)] + [whole(c) for c in consts] + [table_spec] * 3,
        out_specs=[rows(na), rows(nb), rows(COLS_C), rows(COLS_C)],
        out_shape=[jax.ShapeDtypeStruct((T, na), BF16), jax.ShapeDtypeStruct((T, nb), BF16),
                   jax.ShapeDtypeStruct((T, COLS_C), F32), jax.ShapeDtypeStruct((T, COLS_C), BF16)],
        compiler_params=_compiler_params(("parallel",)),
        name="in_proj",
    )(x2d, *consts, *tables)


_NT = (((1,), (1,)), ((), ()))


def _online_softmax(qs, k_fn, v_ref, m_ref, acc_ref, tk, n_chunks, bias_fn):
    m_ref[...] = jnp.full(m_ref.shape, -jnp.inf, F32)
    acc_ref[...] = jnp.zeros(acc_ref.shape, F32)

    def body(c, carry):
        start = pl.multiple_of(c * tk, tk)
        s = lax.dot_general(qs, k_fn(start), _NT, preferred_element_type=F32)
        if bias_fn is not None:
            s = s + bias_fn(start)
        m_prev = m_ref[...]
        m_new = jnp.maximum(m_prev, jnp.max(s, axis=-1, keepdims=True))
        p = jnp.exp(s - m_new)
        alpha = jnp.exp(m_prev - m_new)
        pv = jnp.dot(p.astype(BF16), v_ref[pl.ds(start, tk), :], preferred_element_type=F32)
        acc_ref[...] = alpha * acc_ref[...] + pv
        m_ref[...] = m_new
        return carry

    lax.fori_loop(0, n_chunks, body, 0)


def _shifted_softmax(qs, k_fn, v_ref, acc_ref, tk, lo, hi, bias_fn=None):
    def body(c, carry):
        start = pl.multiple_of(c * tk, tk)
        s = lax.dot_general(qs, k_fn(start), _NT, preferred_element_type=F32)
        if bias_fn is not None:
            s = s + bias_fn(start)
        acc_ref[...] += jnp.dot(jnp.exp(s).astype(BF16), v_ref[pl.ds(start, tk), :], preferred_element_type=F32)
        return carry

    lax.fori_loop(lo, hi, body, 0)


def _half_norms(x, lane):
    sq = x.astype(F32)
    sq = sq * sq
    n1 = jnp.sum(jnp.where(lane < HEAD_DIM, sq, 0.0), axis=-1, keepdims=True)
    n2 = jnp.sum(jnp.where(lane >= HEAD_DIM, sq, 0.0), axis=-1, keepdims=True)
    return jnp.sqrt(n1), jnp.sqrt(n2)


def _diff_attn_kernel(q_ref, k_ref, v_ref, slope_ref, lam_ref, subw_ref, o_ref, kmax_ref, m_ref, acc_ref,
                      *, tq, tk, lam_init):
    i = pl.program_id(2)
    n_chunks = k_ref.shape[0] // tk
    slope = slope_ref[...][:, :1]

    @pl.when(i == 0)
    def _():
        lane_k = lax.broadcasted_iota(jnp.int32, (tk, LANES), 1)

        def body(c, carry):
            n1, n2 = _half_norms(k_ref[pl.ds(pl.multiple_of(c * tk, tk), tk), :], lane_k)
            return (jnp.maximum(carry[0], jnp.max(n1, axis=0, keepdims=True)),
                    jnp.maximum(carry[1], jnp.max(n2, axis=0, keepdims=True)))

        zero11 = jnp.zeros((1, 1), F32)
        k1, k2 = lax.fori_loop(0, n_chunks, body, (zero11, zero11))
        kmax_ref[0:1, :] = jnp.broadcast_to(k1, (1, LANES))
        kmax_ref[1:2, :] = jnp.broadcast_to(k2, (1, LANES))

    q = q_ref[...]
    lane = lax.broadcasted_iota(jnp.int32, q.shape, 1)
    zero = jnp.zeros_like(q)
    qz = jnp.concatenate([jnp.where(lane < HEAD_DIM, q, zero), jnp.where(lane >= HEAD_DIM, q, zero)], axis=0)

    qn1, qn2 = _half_norms(q, lane)
    shift = jnp.concatenate([qn1 * kmax_ref[0:1, 0:1], qn2 * kmax_ref[1:2, 0:1]], axis=0) * SHIFT_MARGIN
    static_ok = jnp.max(shift) <= MAX_STATIC_SHIFT

    row = lax.broadcasted_iota(jnp.int32, (2 * tq, 1), 0)
    q_slope_pos = (i * tq + jnp.where(row >= tq, row - tq, row)).astype(F32) * slope

    def k_fn(start):
        return k_ref[pl.ds(start, tk), :]

    def alibi(start):
        k_slope_pos = (start + lax.broadcasted_iota(jnp.int32, (1, tk), 1)).astype(F32) * slope
        return jnp.abs(q_slope_pos - k_slope_pos)

    @pl.when(static_ok)
    def _():
        acc_ref[...] = jnp.zeros(acc_ref.shape, F32)
        _shifted_softmax(qz, k_fn, v_ref, acc_ref, tk, 0, n_chunks, lambda start: -shift - alibi(start))

    @pl.when(jnp.logical_not(static_ok))
    def _():
        _online_softmax(qz, k_fn, v_ref, m_ref, acc_ref, tk, n_chunks, lambda start: -alibi(start))

    acc = acc_ref[...]
    o = acc[:, :A_VDIM] / acc[:, A_VDIM:A_VDIM + 1]
    lv = lam_ref[...]
    lam = (jnp.exp(jnp.sum(lv[0:1] * lv[1:2], axis=-1, keepdims=True))
           - jnp.exp(jnp.sum(lv[2:3] * lv[3:4], axis=-1, keepdims=True)) + lam_init)
    d = o[:tq] - lam * o[tq:]
    o_ref[...] = (_rms(d, subw_ref[...]) * (1.0 - lam_init)).astype(o_ref.dtype)


def _gqa_attn_kernel(shift_ref, q_ref, k_ref, v_ref, o_ref, m_ref, acc_ref, *, tq, tk):
    qs = jnp.concatenate([q_ref[:, h * LANES:(h + 1) * LANES] for h in range(B_GROUP)], axis=0)
    n_chunks = k_ref.shape[0] // tk
    static_ok = shift_ref[0] <= MAX_STATIC_SHIFT

    def k_fn(start):
        return k_ref[pl.ds(start, tk), :]

    @pl.when(static_ok)
    def _():
        acc_ref[...] = jnp.zeros(acc_ref.shape, F32)
        _shifted_softmax(qs, k_fn, v_ref, acc_ref, tk, 0, n_chunks)

    @pl.when(jnp.logical_not(static_ok))
    def _():
        _online_softmax(qs, k_fn, v_ref, m_ref, acc_ref, tk, n_chunks, None)

    acc = acc_ref[...]
    o = acc / acc[:, HEAD_DIM:HEAD_DIM + 1]
    for h in range(B_GROUP):
        o_ref[:, h * LANES:(h + 1) * LANES] = o[h * tq:(h + 1) * tq].astype(o_ref.dtype)


def _attn_tiles(L, stacked):
    return min(ATT_ROWS // stacked, L), min(ATT_KV_TILE, L)


def _resident(block_shape, index_map):
    return pl.BlockSpec(block_shape, index_map, pipeline_mode=pl.Buffered(1))


def _diff_attention(qkv, slopes, lamv, subw, lam_init, Bsz, L):
    tq, tk = _attn_tiles(L, 2)
    nq = A_HEADS
    kernel = functools.partial(_diff_attn_kernel, tq=tq, tk=tk, lam_init=lam_init)
    return pl.pallas_call(
        kernel,
        grid=(Bsz, A_HEADS, L // tq),
        in_specs=[
            pl.BlockSpec((None, tq, LANES), lambda b, h, i: (b, i, h)),
            _resident((None, L, LANES), lambda b, h, i: (b, 0, nq + h)),
            _resident((None, L, 2 * LANES), lambda b, h, i: (b, 0, nq + h)),
            pl.BlockSpec((None, 1, LANES), lambda b, h, i: (h, 0, 0)),
            _const_spec((4, LANES)),
            _const_spec((1, A_VDIM)),
        ],
        out_specs=pl.BlockSpec((None, tq, LANES), lambda b, h, i: (b, i, h)),
        out_shape=jax.ShapeDtypeStruct((Bsz, L, A_WIDTH), BF16),
        scratch_shapes=[pltpu.VMEM((SUBLANES, LANES), F32),
                        pltpu.VMEM((2 * tq, 1), F32), pltpu.VMEM((2 * tq, 2 * LANES), F32)],
        compiler_params=_compiler_params(("parallel", "parallel", "arbitrary")),
        name="diff_attn",
    )(qkv, qkv, qkv, slopes, lamv, subw)


def _gqa_attention(qkv, shift, Bsz, L):
    tq, tk = _attn_tiles(L, B_GROUP)
    q_blocks = B_HEADS // B_GROUP
    kernel = functools.partial(_gqa_attn_kernel, tq=tq, tk=tk)
    return pl.pallas_call(
        kernel,
        grid=(Bsz, B_KV_HEADS, L // tq),
        in_specs=[
            pl.BlockSpec(memory_space=pltpu.SMEM),
            pl.BlockSpec((None, tq, B_GROUP * LANES), lambda b, g, i: (b, i, g)),
            _resident((None, L, LANES), lambda b, g, i: (b, 0, B_HEADS + g)),
            _resident((None, L, LANES), lambda b, g, i: (b, 0, B_HEADS + B_KV_HEADS + g)),
        ],
        out_specs=pl.BlockSpec((None, tq, B_GROUP * LANES), lambda b, g, i: (b, i, g)),
        out_shape=jax.ShapeDtypeStruct((Bsz, L, q_blocks * B_GROUP * LANES), BF16),
        scratch_shapes=[pltpu.VMEM((B_GROUP * tq, 1), F32), pltpu.VMEM((B_GROUP * tq, LANES), F32)],
        compiler_params=_compiler_params(("parallel", "parallel", "arbitrary")),
        name="gqa_attn",
    )(shift, qkv, qkv, qkv)


def _ssm_kernel(u_ref, k2_ref, wst_ref, wout_ref, a_ref, y_ref, m_ref, s_ref, h_ref, *, n_batch, n_chunks):
    for s in range(SSM_CHUNK):
        first_lag = SSM_CHUNK - 1 - s
        j, a = first_lag % SSM_SHIFTS, first_lag // SSM_SHIFTS
        m_ref[s * SSM_P:(s + 1) * SSM_P, :] = k2_ref[j, :, a * LANES:a * LANES + SSM_FLAT]
    u = u_ref[...]
    y_ref[...] = jnp.dot(u, m_ref[...], preferred_element_type=F32)
    s_ref[...] = jnp.dot(u, wst_ref[...], preferred_element_type=F32)
    a = a_ref[...]
    afr, afi = a[:, 0:LANES], a[:, LANES:2 * LANES]
    abr, abi = a[:, 2 * LANES:3 * LANES], a[:, 3 * LANES:4 * LANES]
    zero = jnp.zeros((1, LANES), F32)

    def step(cg, carry):
        new = []
        for b in range(n_batch):
            hr, hi, gr, gi = carry[4 * b:4 * b + 4]
            rf = pl.multiple_of(b * n_chunks + cg * SUBLANES, SUBLANES)
            rb = pl.multiple_of(b * n_chunks + n_chunks - SUBLANES - cg * SUBLANES, SUBLANES)
            sf = s_ref[pl.ds(rf, SUBLANES), 0:2 * LANES]
            sb = s_ref[pl.ds(rb, SUBLANES), 2 * LANES:4 * LANES]
            rows_f, rows_b = [], [None] * SUBLANES
            for j in range(SUBLANES):
                rows_f.append(jnp.concatenate([hr, hi], axis=1))
                hr, hi = (afr * hr - afi * hi + sf[j:j + 1, 0:LANES],
                          afr * hi + afi * hr + sf[j:j + 1, LANES:2 * LANES])
                jb = SUBLANES - 1 - j
                rows_b[jb] = jnp.concatenate([gr, gi], axis=1)
                gr, gi = (abr * gr - abi * gi + sb[jb:jb + 1, 0:LANES],
                          abr * gi + abi * gr + sb[jb:jb + 1, LANES:2 * LANES])
            h_ref[pl.ds(rf, SUBLANES), 0:2 * LANES] = jnp.concatenate(rows_f, axis=0)
            h_ref[pl.ds(rb, SUBLANES), 2 * LANES:4 * LANES] = jnp.concatenate(rows_b, axis=0)
            new += [hr, hi, gr, gi]
        return tuple(new)

    lax.fori_loop(0, n_chunks // SUBLANES, step, (zero,) * (4 * n_batch))
    y_ref[...] += jnp.dot(h_ref[...].astype(BF16), wout_ref[...], preferred_element_type=F32)


def _ssm_apply(u_g, k2, wst, wout, a64, n_batch, n_chunks):
    nc = n_batch * n_chunks
    kernel = functools.partial(_ssm_kernel, n_batch=n_batch, n_chunks=n_chunks)
    return pl.pallas_call(
        kernel,
        grid=(SSM_GROUPS,),
        in_specs=[
            pl.BlockSpec((None, nc, SSM_FLAT), lambda g: (g, 0, 0)),
            pl.BlockSpec((None, SSM_SHIFTS, SSM_P, SSM_K2_WIDTH), lambda g: (g, 0, 0, 0)),
            pl.BlockSpec((None, SSM_FLAT, SSM_STATE_COLS), lambda g: (g, 0, 0)),
            pl.BlockSpec((None, SSM_STATE_COLS, SSM_FLAT), lambda g: (g, 0, 0)),
            pl.BlockSpec((None, 1, SSM_STATE_COLS), lambda g: (g, 0, 0)),
        ],
        out_specs=pl.BlockSpec((None, nc, SSM_FLAT), lambda g: (g, 0, 0)),
        out_shape=jax.ShapeDtypeStruct((SSM_GROUPS, nc, SSM_FLAT), F32),
        scratch_shapes=[pltpu.VMEM((SSM_FLAT, SSM_FLAT), BF16),
                        pltpu.VMEM((nc, SSM_STATE_COLS), F32), pltpu.VMEM((nc, SSM_STATE_COLS), F32)],
        compiler_params=_compiler_params(("parallel",)),
        name="ssm_chunks",
    )(u_g, k2, wst, wout, a64)


def _ssm_operators(A_re, A_im, log_dt, B_re, B_im, C_re, C_im):
    hi = lax.Precision.HIGHEST
    G, N, P, Tc = SSM_GROUPS, SSM_N, SSM_P, SSM_CHUNK
    Br, Bi, Cr, Ci = B_re.astype(F32), B_im.astype(F32), C_re.astype(F32), C_im.astype(F32)
    steps = jnp.arange(Tc + 1, dtype=F32)

    per_dir = []
    for dirn in range(2):
        dt = jnp.exp(log_dt[dirn].astype(F32))[:, None]
        ar, ai = A_re[dirn].astype(F32), A_im[dirn].astype(F32)
        mag = jnp.exp(dt * ar)
        er, ei = mag * jnp.cos(dt * ai), mag * jnp.sin(dt * ai)
        den = ar * ar + ai * ai
        fr = ((er - 1.0) * ar + ei * ai) / den
        fi = (ei * ar - (er - 1.0) * ai) / den
        bbr = fr[..., None] * Br - fi[..., None] * Bi
        bbi = fr[..., None] * Bi + fi[..., None] * Br
        pm = jnp.exp(steps[None, :, None] * (dt * ar)[:, None, :])
        ang = steps[None, :, None] * (dt * ai)[:, None, :]
        pr, pi = pm * jnp.cos(ang), pm * jnp.sin(ang)
        car = Cr[:, None] * pr[:, :, None, :] - Ci[:, None] * pi[:, :, None, :]
        cai = Cr[:, None] * pi[:, :, None, :] + Ci[:, None] * pr[:, :, None, :]
        kern = (jnp.einsum('gkpn,gnq->gkpq', car[:, :Tc], bbr, precision=hi)
                - jnp.einsum('gkpn,gnq->gkpq', cai[:, :Tc], bbi, precision=hi))
        bbr_t, bbi_t = jnp.swapaxes(bbr, 1, 2)[:, None], jnp.swapaxes(bbi, 1, 2)[:, None]
        prk, pik = pr[:, :, None, :], pi[:, :, None, :]
        abr = prk * bbr_t - pik * bbi_t
        abi = prk * bbi_t + pik * bbr_t
        cr_t, ci_t = jnp.swapaxes(Cr, 1, 2)[:, :, None, :], jnp.swapaxes(Ci, 1, 2)[:, :, None, :]
        prn, pin = jnp.swapaxes(pr, 1, 2)[..., None], jnp.swapaxes(pi, 1, 2)[..., None]
        per_dir.append(dict(kern=kern, abr=abr, abi=abi, pr=pr, pi=pi,
                            car=cr_t * prn - ci_t * pin, cai=cr_t * pin + ci_t * prn))

    f, b = per_dir
    k2 = jnp.concatenate([b['kern'][:, :0:-1], f['kern'][:, :1] + b['kern'][:, :1], f['kern'][:, 1:]], axis=1)
    k2 = jnp.transpose(k2, (0, 3, 1, 2)).astype(BF16).reshape(G, P, (2 * Tc - 1) * P)
    k2 = jnp.stack([k2[:, :, j * P:j * P + SSM_K2_WIDTH] for j in range(SSM_SHIFTS)], axis=1)

    pad = jnp.zeros((G, Tc * P, LANES - N), F32)

    def st_cols(xr):
        return jnp.concatenate([xr.reshape(G, Tc * P, N), pad], axis=-1)

    wst = jnp.concatenate([st_cols(f['abr'][:, Tc - 1::-1][:, :Tc]), st_cols(f['abi'][:, Tc - 1::-1][:, :Tc]),
                           st_cols(b['abr'][:, :Tc]), st_cols(b['abi'][:, :Tc])], axis=-1)

    padr = jnp.zeros((G, LANES - N, Tc * P), F32)

    def out_rows(x):
        return jnp.concatenate([x.reshape(G, N, Tc * P), padr], axis=1)

    wout = jnp.concatenate([out_rows(f['car'][:, :, 1:Tc + 1]), out_rows(-f['cai'][:, :, 1:Tc + 1]),
                            out_rows(b['car'][:, :, Tc:0:-1]), out_rows(-b['cai'][:, :, Tc:0:-1])], axis=1)

    padc = jnp.zeros((G, LANES - N), F32)
    a64 = jnp.concatenate([f['pr'][:, Tc], padc, f['pi'][:, Tc], padc,
                           b['pr'][:, Tc], padc, b['pi'][:, Tc], padc], axis=-1)[:, None, :]
    return k2, wst.astype(BF16), wout.astype(BF16), a64


def _merge_kernel(x_ref, oa_ref, ob_ref, ys_ref, zc_ref, n1_ref, wg_ref, wpa_ref, wpb_ref, wpc_ref,
                  wglu_ref, bglu_ref, dskip_ref, wout_ref, n2_ref, wr_hi_ref, wr_lo_ref, br_ref,
                  x1_ref, xn2_ref, comb_ref):
    x = x_ref[...]
    xn = _rms(x, n1_ref[...]).astype(BF16)
    gates = jax.nn.sigmoid(jnp.dot(xn, wg_ref[...], preferred_element_type=F32))

    y = ys_ref[...] + dskip_ref[...] * zc_ref[...]
    g = jax.nn.gelu(y)
    glu = jnp.dot(g.astype(BF16), wglu_ref[...], preferred_element_type=F32) + bglu_ref[...]
    out_c = g * jax.nn.sigmoid(glu)

    merged = (gates[:, 0:D_MODEL] * jnp.dot(oa_ref[...], wpa_ref[...], preferred_element_type=F32)
              + gates[:, D_MODEL:2 * D_MODEL] * jnp.dot(ob_ref[...], wpb_ref[...], preferred_element_type=F32)
              + gates[:, 2 * D_MODEL:3 * D_MODEL]
              * jnp.dot(out_c.astype(BF16), wpc_ref[...], preferred_element_type=F32))
    x1 = x + jnp.dot(merged.astype(BF16), wout_ref[...], preferred_element_type=F32)
    x1_ref[...] = x1

    xn2 = _rms(x1, n2_ref[...])
    xn2_ref[...] = xn2.astype(BF16)

    lane = lax.broadcasted_iota(jnp.int32, (x.shape[0], LANES), 1).astype(F32)
    neg = jnp.float32(-jnp.inf)
    big = jnp.float32(LANES)
    x_hi = xn2.astype(BF16)
    x_lo = (xn2 - x_hi.astype(F32)).astype(BF16)
    le = (jnp.dot(x_hi, wr_hi_ref[...], preferred_element_type=F32)
          + (jnp.dot(x_hi, wr_lo_ref[...], preferred_element_type=F32)
             + jnp.dot(x_lo, wr_hi_ref[...], preferred_element_type=F32))) + br_ref[...]
    lg = jnp.where((lane >= N_EXPERTS) & (lane < N_EXPERTS + MOE_GROUPS), le, neg)
    g_max = jnp.max(lg, axis=-1, keepdims=True)
    g_idx = jnp.min(jnp.where(lg == g_max, lane, big), axis=-1, keepdims=True) - float(N_EXPERTS)
    g_w = 1.0 / jnp.sum(jnp.exp(lg - g_max), axis=-1, keepdims=True)
    in_group = (lane >= g_idx * EXPERTS_PER_GROUP) & (lane < (g_idx + 1) * EXPERTS_PER_GROUP)
    v = jnp.where(in_group, le, neg)
    top1 = jnp.max(v, axis=-1, keepdims=True)
    i1 = jnp.min(jnp.where(v == top1, lane, big), axis=-1, keepdims=True)
    v2 = jnp.where(lane == i1, neg, v)
    top2 = jnp.max(v2, axis=-1, keepdims=True)
    i2 = jnp.min(jnp.where(v2 == top2, lane, big), axis=-1, keepdims=True)
    e2 = jnp.exp(top2 - top1)
    inv = 1.0 / (1.0 + e2)
    comb_ref[...] = (jnp.where(lane == i1, inv * g_w, 0.0) + jnp.where(lane == i2, e2 * inv * g_w, 0.0))


def _merge(x2d, oa, ob, ys, zc, weights):
    T = x2d.shape[0]
    tm = MERGE_TILE

    def rows(width):
        return pl.BlockSpec((tm, width), lambda i: (i, 0))

    w_specs = [pl.BlockSpec(w.shape, lambda i, nd=w.ndim: (0,) * nd, pipeline_mode=pl.Buffered(1)) for w in weights]
    return pl.pallas_call(
        _merge_kernel,
        grid=(T // tm,),
        in_specs=[rows(D_MODEL), rows(A_WIDTH), rows(B_HEADS * LANES), rows(SSM_WIDTH), rows(SSM_WIDTH)] + w_specs,
        out_specs=[rows(D_MODEL), rows(D_MODEL), rows(LANES)],
        out_shape=[jax.ShapeDtypeStruct((T, D_MODEL), F32), jax.ShapeDtypeStruct((T, D_MODEL), BF16),
                   jax.ShapeDtypeStruct((T, LANES), F32)],
        compiler_params=_compiler_params(("parallel",)),
        name="merge_router",
    )(x2d, oa, ob, ys, zc, *weights)


def _moe_kernel(xn_ref, comb_ref, x1_ref, wg_ref, wu_ref, wd_ref, fw_ref, o_ref, acc_ref, *, final_norm):
    e = pl.program_id(1)

    @pl.when(e == 0)
    def _():
        acc_ref[...] = jnp.zeros(acc_ref.shape, F32)

    xn = xn_ref[...]
    h = jax.nn.silu(jnp.dot(xn, wg_ref[...].astype(BF16), preferred_element_type=F32)) * jnp.dot(
        xn, wu_ref[...].astype(BF16), preferred_element_type=F32)
    y = jnp.dot(h.astype(BF16), wd_ref[...].astype(BF16), preferred_element_type=F32)
    comb = comb_ref[...]
    lane = lax.broadcasted_iota(jnp.int32, comb.shape, 1)
    c = jnp.sum(jnp.where(lane == e, comb, 0.0), axis=-1, keepdims=True)
    acc_ref[...] += c * y

    @pl.when(e == N_EXPERTS - 1)
    def _():
        out = x1_ref[...] + acc_ref[...]
        if final_norm:
            out = _rms(out, fw_ref[...])
        o_ref[...] = out


def _moe(xn2, comb, x1, wg, wu, wd, fw, layer, final_norm):
    T = x1.shape[0]
    tm = EXPERT_TILE
    kernel = functools.partial(_moe_kernel, final_norm=final_norm)
    return pl.pallas_call(
        kernel,
        grid=(T // tm, N_EXPERTS),
        in_specs=[
            pl.BlockSpec((tm, D_MODEL), lambda i, e: (i, 0)),
            pl.BlockSpec((tm, LANES), lambda i, e: (i, 0)),
            pl.BlockSpec((tm, D_MODEL), lambda i, e: (i, 0)),
            pl.BlockSpec((None, None, D_MODEL, D_FF_EXPERT), lambda i, e: (layer, e, 0, 0)),
            pl.BlockSpec((None, None, D_MODEL, D_FF_EXPERT), lambda i, e: (layer, e, 0, 0)),
            pl.BlockSpec((None, None, D_FF_EXPERT, D_MODEL), lambda i, e: (layer, e, 0, 0)),
            _const_spec((1, D_MODEL)),
        ],
        out_specs=pl.BlockSpec((tm, D_MODEL), lambda i, e: (i, 0)),
        out_shape=jax.ShapeDtypeStruct((T, D_MODEL), F32),
        scratch_shapes=[pltpu.VMEM((tm, D_MODEL), F32)],
        compiler_params=_compiler_params(("parallel", "arbitrary")),
        name="experts",
    )(xn2, comb, x1, wg, wu, wd, fw)


def _pad_heads(w, n_heads, width):
    w = w.reshape(D_MODEL, n_heads, -1)
    return jnp.pad(w, ((0, 0), (0, 0), (0, width - w.shape[-1]))).reshape(D_MODEL, n_heads * width)


def _pad_vec(v, width=LANES):
    return jnp.pad(v.astype(F32), (0, width - v.shape[0]))[None, :]


def _layer_params(l, p):
    scale = HEAD_DIM ** -0.5
    w_in = p['w_in'][l]
    wa, wb, wc, wg = jnp.split(w_in, [COLS_A, COLS_A + COLS_B, COLS_A + COLS_B + COLS_C], axis=-1)

    wqa, wka, wva = jnp.split(wa, [A_QK_COLS, 2 * A_QK_COLS], axis=-1)
    wva = _pad_heads(wva, A_HEADS, 2 * LANES)
    w_a = jnp.concatenate([wqa * scale, wka, wva], axis=-1).astype(BF16)
    ones_a = jnp.zeros((A_HEADS, 2 * LANES), F32).at[:, A_VDIM].set(1.0).reshape(1, -1)
    e_a = jnp.concatenate([jnp.zeros((1, 2 * A_QK_COLS), F32), ones_a], axis=-1)

    wqb, wkb, wvb = jnp.split(wb, [B_HEADS * HEAD_DIM, (B_HEADS + B_KV_HEADS) * HEAD_DIM], axis=-1)
    w_b = jnp.concatenate([_pad_heads(wqb, B_HEADS, LANES), _pad_heads(wkb, B_KV_HEADS, LANES),
                           _pad_heads(wvb, B_KV_HEADS, LANES)], axis=-1).astype(BF16)
    e_b = jnp.zeros((B_KV_HEADS, LANES), F32).at[:, HEAD_DIM].set(1.0).reshape(1, -1)
    qw = p['q_norm_w'][l].astype(F32) * scale
    kw = p['k_norm_w'][l].astype(F32)
    shift_b = (HEAD_DIM * SHIFT_MARGIN) * jnp.max(jnp.abs(qw)) * jnp.max(jnp.abs(kw))
    e_q = jnp.zeros((1, LANES), F32).at[0, HEAD_DIM].set(1.0)
    e_k = jnp.zeros((1, LANES), F32).at[0, HEAD_DIM].set(-shift_b)

    wpb = p['w_proj_b'][l].reshape(B_HEADS, HEAD_DIM, D_MODEL)
    wpb = jnp.pad(wpb, ((0, 0), (0, LANES - HEAD_DIM), (0, 0))).reshape(B_HEADS * LANES, D_MODEL)

    w_r = jnp.pad(jnp.concatenate([p['w_router_expert'][l], p['w_router_group'][l]], axis=-1).astype(F32),
                  ((0, 0), (0, LANES - N_EXPERTS - MOE_GROUPS)))
    w_r_hi = w_r.astype(BF16)
    merge_w = [
        p['norm1_w'][l][None, :], wg.astype(BF16), p['w_proj_a'][l].astype(BF16), wpb.astype(BF16),
        p['w_proj_c'][l].astype(BF16), p['w_glu'][l].astype(BF16), p['b_glu'][l][None, :], p['ssm_D'][l][None, :],
        p['w_out'][l].astype(BF16), p['norm2_w'][l][None, :],
        w_r_hi, (w_r - w_r_hi.astype(F32)).astype(BF16),
        _pad_vec(jnp.concatenate([p['b_router_expert'][l], p['b_router_group'][l]])),
    ]
    lam_init = 0.8 - 0.6 * math.exp(-0.3 * l)
    lamv = jnp.stack([_pad_vec(v[l])[0] for v in (p['lam_q1'], p['lam_k1'], p['lam_q2'], p['lam_k2'])])
    return dict(
        norm1=p['norm1_w'][l][None, :], w_a=w_a, e_a=e_a, w_b=w_b, e_b=e_b, w_c=wc.astype(BF16),
        qw=_pad_vec(qw), kw=_pad_vec(kw), e_q=e_q, e_k=e_k, shift_b=shift_b.reshape(1),
        lamv=lamv, lam_init=lam_init, subw=p['diff_subln_w'][l][None, :],
        ssm=_ssm_operators(p['ssm_A_re'][l], p['ssm_A_im'][l], p['ssm_log_dt'][l], p['ssm_B_re'][l],
                           p['ssm_B_im'][l], p['ssm_C_re'][l], p['ssm_C_im'][l]),
        merge_w=merge_w,
        wg=p['w_exp_gate'], wu=p['w_exp_up'], wd=p['w_exp_down'],
    )


def _rope_tables(L):
    half = HEAD_DIM // 2
    inv = ROPE_BASE ** (-jnp.arange(0, half, 2, dtype=F32) / half)
    t = jnp.arange(L)
    row = (t // GRID_W).astype(F32)
    col = (t % GRID_W).astype(F32)
    ang = jnp.concatenate([row[:, None] * inv[None, :]] * 2 + [col[:, None] * inv[None, :]] * 2, axis=-1)
    cos, sin = jnp.cos(ang), jnp.sin(ang)
    first = (jnp.arange(HEAD_DIM) % half) < (half // 2)
    pad = ((0, 0), (0, LANES - HEAD_DIM))
    return (jnp.pad(cos, pad), jnp.pad(jnp.where(first, -sin, 0.0), pad), jnp.pad(jnp.where(first, 0.0, sin), pad))


def _alibi_slopes():
    s = 2.0 ** (-8.0 * jnp.arange(1, A_HEADS + 1, dtype=F32) / A_HEADS)
    return jnp.broadcast_to(s[:, None, None], (A_HEADS, 1, LANES))


def _trunk(x, layers, final_norm_w):
    Bsz, L, _ = x.shape
    T = Bsz * L
    tm = ROW_TILE
    n_chunks = L // SSM_CHUNK
    pos_blocks = L // tm
    cos, sa, sb = _rope_tables(L)
    slopes = _alibi_slopes()
    fw = final_norm_w[None, :]
    x2d = x.reshape(T, D_MODEL)

    for l, lp in enumerate(layers):
        qkv_a, qkv_b, zc32, zc16 = _project(x2d, lp, (cos, sa, sb), pos_blocks)
        out_a = _diff_attention(qkv_a.reshape(Bsz, L, -1), slopes, lp['lamv'], lp['subw'], lp['lam_init'], Bsz, L)
        out_b = _gqa_attention(qkv_b.reshape(Bsz, L, -1), lp['shift_b'], Bsz, L)

        u_g = jnp.transpose(zc16.reshape(Bsz, n_chunks, SSM_CHUNK, SSM_GROUPS, SSM_P), (3, 0, 1, 2, 4))
        u_g = u_g.reshape(SSM_GROUPS, Bsz * n_chunks, SSM_FLAT)
        y_g = _ssm_apply(u_g, *lp['ssm'], Bsz, n_chunks)
        y_s = jnp.transpose(y_g.reshape(SSM_GROUPS, Bsz, n_chunks, SSM_CHUNK, SSM_P), (1, 2, 3, 0, 4))
        y_s = y_s.reshape(T, SSM_WIDTH)

        x1, xn2, comb = _merge(x2d, out_a.reshape(T, A_WIDTH), out_b.reshape(T, B_HEADS * LANES), y_s, zc32,
                               lp['merge_w'])
        x2d = _moe(xn2, comb, x1, lp['wg'], lp['wu'], lp['wd'], fw, layer=l, final_norm=(l == len(layers) - 1))
    return x2d.reshape(Bsz, L, D_MODEL)


def kernel(x_prompt, x_sample, norm1_w, w_in, lam_q1, lam_k1, lam_q2, lam_k2, diff_subln_w, q_norm_w, k_norm_w,
           ssm_A_re, ssm_A_im, ssm_log_dt, ssm_B_re, ssm_B_im, ssm_C_re, ssm_C_im, ssm_D, w_glu, b_glu,
           w_proj_a, w_proj_b, w_proj_c, w_out, norm2_w, w_router_group, b_router_group, w_router_expert,
           b_router_expert, w_exp_gate, w_exp_up, w_exp_down, final_norm_w):
    p = dict(norm1_w=norm1_w, w_in=w_in, lam_q1=lam_q1, lam_k1=lam_k1, lam_q2=lam_q2, lam_k2=lam_k2,
             diff_subln_w=diff_subln_w, q_norm_w=q_norm_w, k_norm_w=k_norm_w, ssm_A_re=ssm_A_re, ssm_A_im=ssm_A_im,
             ssm_log_dt=ssm_log_dt, ssm_B_re=ssm_B_re, ssm_B_im=ssm_B_im, ssm_C_re=ssm_C_re, ssm_C_im=ssm_C_im,
             ssm_D=ssm_D, w_glu=w_glu, b_glu=b_glu, w_proj_a=w_proj_a, w_proj_b=w_proj_b, w_proj_c=w_proj_c,
             w_out=w_out, norm2_w=norm2_w, w_router_group=w_router_group, b_router_group=b_router_group,
             w_router_expert=w_router_expert, b_router_expert=b_router_expert, w_exp_gate=w_exp_gate,
             w_exp_up=w_exp_up, w_exp_down=w_exp_down)
    layers = [_layer_params(l, p) for l in range(DEPTH)]
    return (_trunk(x_prompt, layers, final_norm_w), _trunk(x_sample, layers, final_norm_w))
```

```python
import functools
import math

import jax
import jax.numpy as jnp
from jax import lax
from jax.experimental import pallas as pl
from jax.experimental.pallas import tpu as pltpu

F32 = jnp.float32
BF16 = jnp.bfloat16

D_MODEL = 1024
DEPTH = 2
HEAD_DIM = 64
EPS = 1e-6
A_HEADS = 4
A_VDIM = 2 * HEAD_DIM
A_QK_COLS = A_HEADS * 2 * HEAD_DIM
A_WIDTH = A_HEADS * A_VDIM
COLS_A = 2 * A_QK_COLS + A_WIDTH
B_HEADS = 8
B_KV_HEADS = 2
B_GROUP = B_HEADS // B_KV_HEADS
B_WIDTH = B_HEADS * HEAD_DIM
COLS_B = B_HEADS * HEAD_DIM + 2 * B_KV_HEADS * HEAD_DIM
ROPE_BASE = 10000.0
GRID_W = 64
SSM_WIDTH = 512
SSM_P = 16
SSM_GROUPS = SSM_WIDTH // SSM_P
SSM_N = 64
COLS_C = SSM_WIDTH
N_BRANCH = 3
MOE_GROUPS = 4
EXPERTS_PER_GROUP = 4
N_EXPERTS = MOE_GROUPS * EXPERTS_PER_GROUP
D_FF_EXPERT = 512

LANES = 128
SUBLANES = 8
VMEM_LIMIT_BYTES = 60 * 1024 * 1024

SSM_CHUNK = 64
SSM_FLAT = SSM_CHUNK * SSM_P
SSM_STATE_COLS = 4 * LANES
SSM_SHIFTS = LANES // SSM_P
SSM_K2_WIDTH = (SSM_CHUNK - 1) // SSM_SHIFTS * LANES + SSM_FLAT

ROW_TILE = 512
MERGE_TILE = 512
EXPERT_TILE = 1024
EXPERTS_PER_STEP = 2
ATT_ROWS = 1024
ATT_KV_TILE = 4096
MAX_STATIC_SHIFT = 40.0
SHIFT_MARGIN = 1.02


def _compiler_params(semantics):
    return pltpu.CompilerParams(dimension_semantics=semantics, vmem_limit_bytes=VMEM_LIMIT_BYTES)


def _const_spec(shape):
    zeros = (0,) * len(shape)
    return pl.BlockSpec(shape, lambda *_: zeros)


def _rms(x, w):
    ms = jnp.mean(x * x, axis=-1, keepdims=True)
    return x * lax.rsqrt(ms + EPS) * w


def _proj_kernel(x_ref, nw_ref, wa_ref, wb_ref, wc_ref, ea_ref, qw_ref, kw_ref, eq_ref, ek_ref, ev_ref,
                 cos_ref, sa_ref, sb_ref, oa_ref, ob_ref, oc32_ref, oc16_ref):
    xn = _rms(x_ref[...], nw_ref[...]).astype(BF16)

    zb = jnp.dot(xn, wb_ref[...], preferred_element_type=F32)
    cos, sa, sb = cos_ref[...], sa_ref[...], sb_ref[...]

    def norm_rope(zh, w):
        ms = jnp.sum(zh * zh, axis=-1, keepdims=True) * (1.0 / HEAD_DIM)
        y = zh * lax.rsqrt(ms + EPS) * w
        return y * cos + pltpu.roll(y, LANES - 16, 1) * sa + pltpu.roll(y, 16, 1) * sb

    for h in range(B_HEADS + B_KV_HEADS):
        sl = slice(h * LANES, (h + 1) * LANES)
        w, e = (qw_ref, eq_ref) if h < B_HEADS else (kw_ref, ek_ref)
        ob_ref[:, sl] = (norm_rope(zb[:, sl], w[...]) + e[...]).astype(ob_ref.dtype)
    sl = slice((B_HEADS + B_KV_HEADS) * LANES, (B_HEADS + 2 * B_KV_HEADS) * LANES)
    ob_ref[:, sl] = (zb[:, sl] + ev_ref[...]).astype(ob_ref.dtype)

    oa_ref[...] = (jnp.dot(xn, wa_ref[...], preferred_element_type=F32) + ea_ref[...]).astype(oa_ref.dtype)

    zc = jnp.dot(xn, wc_ref[...], preferred_element_type=F32)
    oc32_ref[...] = zc
    oc16_ref[...] = zc.astype(BF16)


def _project(x2d, lp, tables, pos_blocks):
    T = x2d.shape[0]
    tm = ROW_TILE
    na, nb = lp['w_a'].shape[1], lp['w_b'].shape[1]

    def rows(width):
        return pl.BlockSpec((tm, width), lambda i: (i, 0))

    def whole(arr):
        return pl.BlockSpec(arr.shape, lambda i: (0, 0), pipeline_mode=pl.Buffered(1))

    consts = [lp['norm1'], lp['w_a'], lp['w_b'], lp['w_c'], lp['e_a'], lp['qw'], lp['kw'], lp['e_q'], lp['e_k'],
              lp['e_b']]
    table_spec = pl.BlockSpec((tm, LANES), lambda i: (i % pos_blocks, 0))
    return pl.pallas_call(
        _proj_kernel,
        grid=(T // tm,),
        in_specs=[rows(D_MODEL)] + [whole(c) for c in consts] + [table_spec] * 3,
        out_specs=[rows(na), rows(nb), rows(COLS_C), rows(COLS_C)],
        out_shape=[jax.ShapeDtypeStruct((T, na), BF16), jax.ShapeDtypeStruct((T, nb), BF16),
                   jax.ShapeDtypeStruct((T, COLS_C), F32), jax.ShapeDtypeStruct((T, COLS_C), BF16)],
        compiler_params=_compiler_params(("parallel",)),
        name="in_proj",
    )(x2d, *consts, *tables)


_NT = (((1,), (1,)), ((), ()))


def _online_softmax(qs, k_fn, v_ref, m_ref, acc_ref, tk, n_chunks, bias_fn):
    m_ref[...] = jnp.full(m_ref.shape, -jnp.inf, F32)
    acc_ref[...] = jnp.zeros(acc_ref.shape, F32)

    def body(c, carry):
        start = pl.multiple_of(c * tk, tk)
        s = lax.dot_general(qs, k_fn(start), _NT, preferred_element_type=F32)
        if bias_fn is not None:
            s = s + bias_fn(start)
        m_prev = m_ref[...]
        m_new = jnp.maximum(m_prev, jnp.max(s, axis=-1, keepdims=True))
        p = jnp.exp(s - m_new)
        alpha = jnp.exp(m_prev - m_new)
        pv = jnp.dot(p.astype(BF16), v_ref[pl.ds(start, tk), :], preferred_element_type=F32)
        acc_ref[...] = alpha * acc_ref[...] + pv
        m_ref[...] = m_new
        return carry

    lax.fori_loop(0, n_chunks, body, 0)


def _shifted_softmax(qs, k_fn, v_ref, acc_ref, tk, lo, hi, bias_fn=None):
    def body(c, carry):
        start = pl.multiple_of(c * tk, tk)
        s = lax.dot_general(qs, k_fn(start), _NT, preferred_element_type=F32)
        if bias_fn is not None:
            s = s + bias_fn(start)
        acc_ref[...] += jnp.dot(jnp.exp(s).astype(BF16), v_ref[pl.ds(start, tk), :], preferred_element_type=F32)
        return carry

    lax.fori_loop(lo, hi, body, 0)


def _half_norms(x, lane):
    sq = x.astype(F32)
    sq = sq * sq
    n1 = jnp.sum(jnp.where(lane < HEAD_DIM, sq, 0.0), axis=-1, keepdims=True)
    n2 = jnp.sum(jnp.where(lane >= HEAD_DIM, sq, 0.0), axis=-1, keepdims=True)
    return jnp.sqrt(n1), jnp.sqrt(n2)


def _diff_attn_kernel(q_ref, k_ref, v_ref, slope_ref, lam_ref, subw_ref, o_ref, kmax_ref, m_ref, acc_ref,
                      *, tq, tk, lam_init):
    i = pl.program_id(2)
    n_chunks = k_ref.shape[0] // tk
    slope = slope_ref[...][:, :1]

    @pl.when(i == 0)
    def _():
        lane_k = lax.broadcasted_iota(jnp.int32, (tk, LANES), 1)

        def body(c, carry):
            n1, n2 = _half_norms(k_ref[pl.ds(pl.multiple_of(c * tk, tk), tk), :], lane_k)
            return (jnp.maximum(carry[0], jnp.max(n1, axis=0, keepdims=True)),
                    jnp.maximum(carry[1], jnp.max(n2, axis=0, keepdims=True)))

        zero11 = jnp.zeros((1, 1), F32)
        k1, k2 = lax.fori_loop(0, n_chunks, body, (zero11, zero11))
        kmax_ref[0:1, :] = jnp.broadcast_to(k1, (1, LANES))
        kmax_ref[1:2, :] = jnp.broadcast_to(k2, (1, LANES))

    q = q_ref[...]
    lane = lax.broadcasted_iota(jnp.int32, q.shape, 1)
    zero = jnp.zeros_like(q)
    qz = jnp.concatenate([jnp.where(lane < HEAD_DIM, q, zero), jnp.where(lane >= HEAD_DIM, q, zero)], axis=0)

    qn1, qn2 = _half_norms(q, lane)
    shift = jnp.concatenate([qn1 * kmax_ref[0:1, 0:1], qn2 * kmax_ref[1:2, 0:1]], axis=0) * SHIFT_MARGIN
    static_ok = jnp.max(shift) <= MAX_STATIC_SHIFT

    row = lax.broadcasted_iota(jnp.int32, (2 * tq, 1), 0)
    q_slope_pos = (i * tq + jnp.where(row >= tq, row - tq, row)).astype(F32) * slope

    def k_fn(start):
        return k_ref[pl.ds(start, tk), :]

    def alibi(start):
        k_slope_pos = (start + lax.broadcasted_iota(jnp.int32, (1, tk), 1)).astype(F32) * slope
        return jnp.abs(q_slope_pos - k_slope_pos)

    @pl.when(static_ok)
    def _():
        acc_ref[...] = jnp.zeros(acc_ref.shape, F32)
        _shifted_softmax(qz, k_fn, v_ref, acc_ref, tk, 0, n_chunks, lambda start: -shift - alibi(start))

    @pl.when(jnp.logical_not(static_ok))
    def _():
        _online_softmax(qz, k_fn, v_ref, m_ref, acc_ref, tk, n_chunks, lambda start: -alibi(start))

    acc = acc_ref[...]
    o = acc[:, :A_VDIM] / acc[:, A_VDIM:A_VDIM + 1]
    lv = lam_ref[...]
    lam = (jnp.exp(jnp.sum(lv[0:1] * lv[1:2], axis=-1, keepdims=True))
           - jnp.exp(jnp.sum(lv[2:3] * lv[3:4], axis=-1, keepdims=True)) + lam_init)
    d = o[:tq] - lam * o[tq:]
    o_ref[...] = (_rms(d, subw_ref[...]) * (1.0 - lam_init)).astype(o_ref.dtype)


def _gqa_attn_kernel(shift_ref, q_ref, k_ref, v_ref, o_ref, m_ref, acc_ref, *, tq, tk):
    qs = jnp.concatenate([q_ref[:, h * LANES:(h + 1) * LANES] for h in range(B_GROUP)], axis=0)
    n_chunks = k_ref.shape[0] // tk
    static_ok = shift_ref[0] <= MAX_STATIC_SHIFT

    def k_fn(start):
        return k_ref[pl.ds(start, tk), :]

    @pl.when(static_ok)
    def _():
        acc_ref[...] = jnp.zeros(acc_ref.shape, F32)
        _shifted_softmax(qs, k_fn, v_ref, acc_ref, tk, 0, n_chunks)

    @pl.when(jnp.logical_not(static_ok))
    def _():
        _online_softmax(qs, k_fn, v_ref, m_ref, acc_ref, tk, n_chunks, None)

    acc = acc_ref[...]
    o = acc / acc[:, HEAD_DIM:HEAD_DIM + 1]
    for h in range(B_GROUP):
        o_ref[:, h * LANES:(h + 1) * LANES] = o[h * tq:(h + 1) * tq].astype(o_ref.dtype)


def _attn_tiles(L, stacked):
    return min(ATT_ROWS // stacked, L), min(ATT_KV_TILE, L)


def _resident(block_shape, index_map):
    return pl.BlockSpec(block_shape, index_map, pipeline_mode=pl.Buffered(1))


def _diff_attention(qkv, slopes, lamv, subw, lam_init, Bsz, L):
    tq, tk = _attn_tiles(L, 2)
    nq = A_HEADS
    kernel = functools.partial(_diff_attn_kernel, tq=tq, tk=tk, lam_init=lam_init)
    return pl.pallas_call(
        kernel,
        grid=(Bsz, A_HEADS, L // tq),
        in_specs=[
            pl.BlockSpec((None, tq, LANES), lambda b, h, i: (b, i, h)),
            _resident((None, L, LANES), lambda b, h, i: (b, 0, nq + h)),
            _resident((None, L, 2 * LANES), lambda b, h, i: (b, 0, nq + h)),
            pl.BlockSpec((None, 1, LANES), lambda b, h, i: (h, 0, 0)),
            _const_spec((4, LANES)),
            _const_spec((1, A_VDIM)),
        ],
        out_specs=pl.BlockSpec((None, tq, LANES), lambda b, h, i: (b, i, h)),
        out_shape=jax.ShapeDtypeStruct((Bsz, L, A_WIDTH), BF16),
        scratch_shapes=[pltpu.VMEM((SUBLANES, LANES), F32),
                        pltpu.VMEM((2 * tq, 1), F32), pltpu.VMEM((2 * tq, 2 * LANES), F32)],
        compiler_params=_compiler_params(("parallel", "parallel", "arbitrary")),
        name="diff_attn",
    )(qkv, qkv, qkv, slopes, lamv, subw)


def _gqa_attention(qkv, shift, Bsz, L):
    tq, tk = _attn_tiles(L, B_GROUP)
    q_blocks = B_HEADS // B_GROUP
    kernel = functools.partial(_gqa_attn_kernel, tq=tq, tk=tk)
    return pl.pallas_call(
        kernel,
        grid=(Bsz, B_KV_HEADS, L // tq),
        in_specs=[
            pl.BlockSpec(memory_space=pltpu.SMEM),
            pl.BlockSpec((None, tq, B_GROUP * LANES), lambda b, g, i: (b, i, g)),
            _resident((None, L, LANES), lambda b, g, i: (b, 0, B_HEADS + g)),
            _resident((None, L, LANES), lambda b, g, i: (b, 0, B_HEADS + B_KV_HEADS + g)),
        ],
        out_specs=pl.BlockSpec((None, tq, B_GROUP * LANES), lambda b, g, i: (b, i, g)),
        out_shape=jax.ShapeDtypeStruct((Bsz, L, q_blocks * B_GROUP * LANES), BF16),
        scratch_shapes=[pltpu.VMEM((B_GROUP * tq, 1), F32), pltpu.VMEM((B_GROUP * tq, LANES), F32)],
        compiler_params=_compiler_params(("parallel", "parallel", "arbitrary")),
        name="gqa_attn",
    )(shift, qkv, qkv, qkv)


def _ssm_kernel(u_ref, k2_ref, wst_ref, wout_ref, a_ref, y_ref, m_ref, s_ref, h_ref, *, n_batch, n_chunks):
    for s in range(SSM_CHUNK):
        first_lag = SSM_CHUNK - 1 - s
        j, a = first_lag % SSM_SHIFTS, first_lag // SSM_SHIFTS
        m_ref[s * SSM_P:(s + 1) * SSM_P, :] = k2_ref[j, :, a * LANES:a * LANES + SSM_FLAT]
    u = u_ref[...]
    y_ref[...] = jnp.dot(u, m_ref[...], preferred_element_type=F32)
    s_ref[...] = jnp.dot(u, wst_ref[...], preferred_element_type=F32)
    a = a_ref[...]
    afr, afi = a[:, 0:LANES], a[:, LANES:2 * LANES]
    abr, abi = a[:, 2 * LANES:3 * LANES], a[:, 3 * LANES:4 * LANES]
    zero = jnp.zeros((1, LANES), F32)

    def step(cg, carry):
        new = []
        for b in range(n_batch):
            hr, hi, gr, gi = carry[4 * b:4 * b + 4]
            rf = pl.multiple_of(b * n_chunks + cg * SUBLANES, SUBLANES)
            rb = pl.multiple_of(b * n_chunks + n_chunks - SUBLANES - cg * SUBLANES, SUBLANES)
            sf = s_ref[pl.ds(rf, SUBLANES), 0:2 * LANES]
            sb = s_ref[pl.ds(rb, SUBLANES), 2 * LANES:4 * LANES]
            rows_f, rows_b = [], [None] * SUBLANES
            for j in range(SUBLANES):
                rows_f.append(jnp.concatenate([hr, hi], axis=1))
                hr, hi = (afr * hr - afi * hi + sf[j:j + 1, 0:LANES],
                          afr * hi + afi * hr + sf[j:j + 1, LANES:2 * LANES])
                jb = SUBLANES - 1 - j
                rows_b[jb] = jnp.concatenate([gr, gi], axis=1)
                gr, gi = (abr * gr - abi * gi + sb[jb:jb + 1, 0:LANES],
                          abr * gi + abi * gr + sb[jb:jb + 1, LANES:2 * LANES])
            h_ref[pl.ds(rf, SUBLANES), 0:2 * LANES] = jnp.concatenate(rows_f, axis=0)
            h_ref[pl.ds(rb, SUBLANES), 2 * LANES:4 * LANES] = jnp.concatenate(rows_b, axis=0)
            new += [hr, hi, gr, gi]
        return tuple(new)

    lax.fori_loop(0, n_chunks // SUBLANES, step, (zero,) * (4 * n_batch))
    y_ref[...] += jnp.dot(h_ref[...].astype(BF16), wout_ref[...], preferred_element_type=F32)


def _ssm_apply(u_g, k2, wst, wout, a64, n_batch, n_chunks):
    nc = n_batch * n_chunks
    kernel = functools.partial(_ssm_kernel, n_batch=n_batch, n_chunks=n_chunks)
    return pl.pallas_call(
        kernel,
        grid=(SSM_GROUPS,),
        in_specs=[
            pl.BlockSpec((None, nc, SSM_FLAT), lambda g: (g, 0, 0)),
            pl.BlockSpec((None, SSM_SHIFTS, SSM_P, SSM_K2_WIDTH), lambda g: (g, 0, 0, 0)),
            pl.BlockSpec((None, SSM_FLAT, SSM_STATE_COLS), lambda g: (g, 0, 0)),
            pl.BlockSpec((None, SSM_STATE_COLS, SSM_FLAT), lambda g: (g, 0, 0)),
            pl.BlockSpec((None, 1, SSM_STATE_COLS), lambda g: (g, 0, 0)),
        ],
        out_specs=pl.BlockSpec((None, nc, SSM_FLAT), lambda g: (g, 0, 0)),
        out_shape=jax.ShapeDtypeStruct((SSM_GROUPS, nc, SSM_FLAT), F32),
        scratch_shapes=[pltpu.VMEM((SSM_FLAT, SSM_FLAT), BF16),
                        pltpu.VMEM((nc, SSM_STATE_COLS), F32), pltpu.VMEM((nc, SSM_STATE_COLS), F32)],
        compiler_params=_compiler_params(("parallel",)),
        name="ssm_chunks",
    )(u_g, k2, wst, wout, a64)


def _ssm_operators(A_re, A_im, log_dt, B_re, B_im, C_re, C_im):
    hi = lax.Precision.HIGHEST
    G, N, P, Tc = SSM_GROUPS, SSM_N, SSM_P, SSM_CHUNK
    Br, Bi, Cr, Ci = B_re.astype(F32), B_im.astype(F32), C_re.astype(F32), C_im.astype(F32)
    steps = jnp.arange(Tc + 1, dtype=F32)

    per_dir = []
    for dirn in range(2):
        dt = jnp.exp(log_dt[dirn].astype(F32))[:, None]
        ar, ai = A_re[dirn].astype(F32), A_im[dirn].astype(F32)
        mag = jnp.exp(dt * ar)
        er, ei = mag * jnp.cos(dt * ai), mag * jnp.sin(dt * ai)
        den = ar * ar + ai * ai
        fr = ((er - 1.0) * ar + ei * ai) / den
        fi = (ei * ar - (er - 1.0) * ai) / den
        bbr = fr[..., None] * Br - fi[..., None] * Bi
        bbi = fr[..., None] * Bi + fi[..., None] * Br
        pm = jnp.exp(steps[None, :, None] * (dt * ar)[:, None, :])
        ang = steps[None, :, None] * (dt * ai)[:, None, :]
        pr, pi = pm * jnp.cos(ang), pm * jnp.sin(ang)
        car = Cr[:, None] * pr[:, :, None, :] - Ci[:, None] * pi[:, :, None, :]
        cai = Cr[:, None] * pi[:, :, None, :] + Ci[:, None] * pr[:, :, None, :]
        kern = (jnp.einsum('gkpn,gnq->gkpq', car[:, :Tc], bbr, precision=hi)
                - jnp.einsum('gkpn,gnq->gkpq', cai[:, :Tc], bbi, precision=hi))
        bbr_t, bbi_t = jnp.swapaxes(bbr, 1, 2)[:, None], jnp.swapaxes(bbi, 1, 2)[:, None]
        prk, pik = pr[:, :, None, :], pi[:, :, None, :]
        abr = prk * bbr_t - pik * bbi_t
        abi = prk * bbi_t + pik * bbr_t
        cr_t, ci_t = jnp.swapaxes(Cr, 1, 2)[:, :, None, :], jnp.swapaxes(Ci, 1, 2)[:, :, None, :]
        prn, pin = jnp.swapaxes(pr, 1, 2)[..., None], jnp.swapaxes(pi, 1, 2)[..., None]
        per_dir.append(dict(kern=kern, abr=abr, abi=abi, pr=pr, pi=pi,
                            car=cr_t * prn - ci_t * pin, cai=cr_t * pin + ci_t * prn))

    f, b = per_dir
    k2 = jnp.concatenate([b['kern'][:, :0:-1], f['kern'][:, :1] + b['kern'][:, :1], f['kern'][:, 1:]], axis=1)
    k2 = jnp.transpose(k2, (0, 3, 1, 2)).astype(BF16).reshape(G, P, (2 * Tc - 1) * P)
    k2 = jnp.stack([k2[:, :, j * P:j * P + SSM_K2_WIDTH] for j in range(SSM_SHIFTS)], axis=1)

    pad = jnp.zeros((G, Tc * P, LANES - N), BF16)

    def st_cols(xr):
        return jnp.concatenate([xr.astype(BF16).reshape(G, Tc * P, N), pad], axis=-1)

    wst = jnp.concatenate([st_cols(f['abr'][:, Tc - 1::-1][:, :Tc]), st_cols(f['abi'][:, Tc - 1::-1][:, :Tc]),
                           st_cols(b['abr'][:, :Tc]), st_cols(b['abi'][:, :Tc])], axis=-1)

    padr = jnp.zeros((G, LANES - N, Tc * P), BF16)

    def out_rows(x):
        return jnp.concatenate([x.astype(BF16).reshape(G, N, Tc * P), padr], axis=1)

    wout = jnp.concatenate([out_rows(f['car'][:, :, 1:Tc + 1]), out_rows(-f['cai'][:, :, 1:Tc + 1]),
                            out_rows(b['car'][:, :, Tc:0:-1]), out_rows(-b['cai'][:, :, Tc:0:-1])], axis=1)

    padc = jnp.zeros((G, LANES - N), F32)
    a64 = jnp.concatenate([f['pr'][:, Tc], padc, f['pi'][:, Tc], padc,
                           b['pr'][:, Tc], padc, b['pi'][:, Tc], padc], axis=-1)[:, None, :]
    return k2, wst, wout, a64


def _merge_kernel(x_ref, oa_ref, ob_ref, ys_ref, zc_ref, n1_ref, wg_ref, wpa_ref, wpb_ref, wpc_ref,
                  wglu_ref, bglu_ref, dskip_ref, wout_ref, n2_ref, wr_hi_ref, wr_lo_ref, br_ref,
                  x1_ref, xn2_ref, comb_ref):
    x = x_ref[...]
    xn = _rms(x, n1_ref[...]).astype(BF16)
    def gate(branch):
        w = wg_ref[:, branch * D_MODEL:(branch + 1) * D_MODEL]
        return jax.nn.sigmoid(jnp.dot(xn, w, preferred_element_type=F32))

    y = ys_ref[...] + dskip_ref[...] * zc_ref[...]
    g = jax.nn.gelu(y)
    glu = jnp.dot(g.astype(BF16), wglu_ref[...], preferred_element_type=F32) + bglu_ref[...]
    out_c = g * jax.nn.sigmoid(glu)

    merged = gate(0) * jnp.dot(oa_ref[...], wpa_ref[...], preferred_element_type=F32)
    merged += gate(1) * jnp.dot(ob_ref[...], wpb_ref[...], preferred_element_type=F32)
    merged += gate(2) * jnp.dot(out_c.astype(BF16), wpc_ref[...], preferred_element_type=F32)
    x1 = x + jnp.dot(merged.astype(BF16), wout_ref[...], preferred_element_type=F32)
    x1_ref[...] = x1

    xn2 = _rms(x1, n2_ref[...])
    xn2_ref[...] = xn2.astype(BF16)

    lane = lax.broadcasted_iota(jnp.int32, (x.shape[0], LANES), 1).astype(F32)
    neg = jnp.float32(-jnp.inf)
    big = jnp.float32(LANES)
    x_hi = xn2.astype(BF16)
    x_lo = (xn2 - x_hi.astype(F32)).astype(BF16)
    le = (jnp.dot(x_hi, wr_hi_ref[...], preferred_element_type=F32)
          + (jnp.dot(x_hi, wr_lo_ref[...], preferred_element_type=F32)
             + jnp.dot(x_lo, wr_hi_ref[...], preferred_element_type=F32))) + br_ref[...]
    lg = jnp.where((lane >= N_EXPERTS) & (lane < N_EXPERTS + MOE_GROUPS), le, neg)
    g_max = jnp.max(lg, axis=-1, keepdims=True)
    g_idx = jnp.min(jnp.where(lg == g_max, lane, big), axis=-1, keepdims=True) - float(N_EXPERTS)
    g_w = 1.0 / jnp.sum(jnp.exp(lg - g_max), axis=-1, keepdims=True)
    in_group = (lane >= g_idx * EXPERTS_PER_GROUP) & (lane < (g_idx + 1) * EXPERTS_PER_GROUP)
    v = jnp.where(in_group, le, neg)
    top1 = jnp.max(v, axis=-1, keepdims=True)
    i1 = jnp.min(jnp.where(v == top1, lane, big), axis=-1, keepdims=True)
    v2 = jnp.where(lane == i1, neg, v)
    top2 = jnp.max(v2, axis=-1, keepdims=True)
    i2 = jnp.min(jnp.where(v2 == top2, lane, big), axis=-1, keepdims=True)
    e2 = jnp.exp(top2 - top1)
    inv = 1.0 / (1.0 + e2)
    comb_ref[...] = (jnp.where(lane == i1, inv * g_w, 0.0) + jnp.where(lane == i2, e2 * inv * g_w, 0.0))


def _merge(x2d, oa, ob, ys, zc, weights):
    T = x2d.shape[0]
    tm = MERGE_TILE

    def rows(width):
        return pl.BlockSpec((tm, width), lambda i: (i, 0))

    w_specs = [pl.BlockSpec(w.shape, lambda i, nd=w.ndim: (0,) * nd, pipeline_mode=pl.Buffered(1)) for w in weights]
    return pl.pallas_call(
        _merge_kernel,
        grid=(T // tm,),
        in_specs=[rows(D_MODEL), rows(A_WIDTH), rows(B_HEADS * LANES), rows(SSM_WIDTH), rows(SSM_WIDTH)] + w_specs,
        out_specs=[rows(D_MODEL), rows(D_MODEL), rows(LANES)],
        out_shape=[jax.ShapeDtypeStruct((T, D_MODEL), F32), jax.ShapeDtypeStruct((T, D_MODEL), BF16),
                   jax.ShapeDtypeStruct((T, LANES), F32)],
        compiler_params=_compiler_params(("parallel",)),
        name="merge_router",
    )(x2d, oa, ob, ys, zc, *weights)


def _moe_kernel(xn_ref, comb_ref, x1_ref, wg_ref, wu_ref, wd_ref, fw_ref, o_ref, acc_ref, *, final_norm):
    step = pl.program_id(1)

    @pl.when(step == 0)
    def _():
        acc_ref[...] = jnp.zeros(acc_ref.shape, F32)

    xn = xn_ref[...]
    comb = comb_ref[...]
    lane = lax.broadcasted_iota(jnp.int32, comb.shape, 1)
    hidden = []
    for j in range(EXPERTS_PER_STEP):
        h = jax.nn.silu(jnp.dot(xn, wg_ref[j].astype(BF16), preferred_element_type=F32)) * jnp.dot(
            xn, wu_ref[j].astype(BF16), preferred_element_type=F32)
        c = jnp.sum(jnp.where(lane == step * EXPERTS_PER_STEP + j, comb, 0.0), axis=-1, keepdims=True)
        hidden.append((c * h).astype(BF16))
    wd = wd_ref[...].astype(BF16).reshape(EXPERTS_PER_STEP * D_FF_EXPERT, D_MODEL)
    acc_ref[...] += jnp.dot(jnp.concatenate(hidden, axis=1), wd, preferred_element_type=F32)

    @pl.when(step == N_EXPERTS // EXPERTS_PER_STEP - 1)
    def _():
        out = x1_ref[...] + acc_ref[...]
        if final_norm:
            out = _rms(out, fw_ref[...])
        o_ref[...] = out


def _moe(xn2, comb, x1, wg, wu, wd, fw, layer, final_norm):
    T = x1.shape[0]
    tm = EXPERT_TILE
    kernel = functools.partial(_moe_kernel, final_norm=final_norm)
    return pl.pallas_call(
        kernel,
        grid=(T // tm, N_EXPERTS // EXPERTS_PER_STEP),
        in_specs=[
            pl.BlockSpec((tm, D_MODEL), lambda i, e: (i, 0)),
            pl.BlockSpec((tm, LANES), lambda i, e: (i, 0)),
            pl.BlockSpec((tm, D_MODEL), lambda i, e: (i, 0)),
            pl.BlockSpec((None, EXPERTS_PER_STEP, D_MODEL, D_FF_EXPERT), lambda i, e: (layer, e, 0, 0)),
            pl.BlockSpec((None, EXPERTS_PER_STEP, D_MODEL, D_FF_EXPERT), lambda i, e: (layer, e, 0, 0)),
            pl.BlockSpec((None, EXPERTS_PER_STEP, D_FF_EXPERT, D_MODEL), lambda i, e: (layer, e, 0, 0)),
            _const_spec((1, D_MODEL)),
        ],
        out_specs=pl.BlockSpec((tm, D_MODEL), lambda i, e: (i, 0)),
        out_shape=jax.ShapeDtypeStruct((T, D_MODEL), F32),
        scratch_shapes=[pltpu.VMEM((tm, D_MODEL), F32)],
        compiler_params=_compiler_params(("parallel", "arbitrary")),
        name="experts",
    )(xn2, comb, x1, wg, wu, wd, fw)


def _pad_heads(w, n_heads, width):
    w = w.reshape(D_MODEL, n_heads, -1)
    return jnp.pad(w, ((0, 0), (0, 0), (0, width - w.shape[-1]))).reshape(D_MODEL, n_heads * width)


def _pad_vec(v, width=LANES):
    return jnp.pad(v.astype(F32), (0, width - v.shape[0]))[None, :]


def _layer_params(l, p):
    scale = HEAD_DIM ** -0.5
    w_in = p['w_in'][l]
    wa, wb, wc, wg = jnp.split(w_in, [COLS_A, COLS_A + COLS_B, COLS_A + COLS_B + COLS_C], axis=-1)

    wqa, wka, wva = jnp.split(wa, [A_QK_COLS, 2 * A_QK_COLS], axis=-1)
    wva = _pad_heads(wva, A_HEADS, 2 * LANES)
    w_a = jnp.concatenate([wqa * scale, wka, wva], axis=-1).astype(BF16)
    ones_a = jnp.zeros((A_HEADS, 2 * LANES), F32).at[:, A_VDIM].set(1.0).reshape(1, -1)
    e_a = jnp.concatenate([jnp.zeros((1, 2 * A_QK_COLS), F32), ones_a], axis=-1)

    wqb, wkb, wvb = jnp.split(wb, [B_HEADS * HEAD_DIM, (B_HEADS + B_KV_HEADS) * HEAD_DIM], axis=-1)
    w_b = jnp.concatenate([_pad_heads(wqb, B_HEADS, LANES), _pad_heads(wkb, B_KV_HEADS, LANES),
                           _pad_heads(wvb, B_KV_HEADS, LANES)], axis=-1).astype(BF16)
    e_b = jnp.zeros((B_KV_HEADS, LANES), F32).at[:, HEAD_DIM].set(1.0).reshape(1, -1)
    qw = p['q_norm_w'][l].astype(F32) * scale
    kw = p['k_norm_w'][l].astype(F32)
    shift_b = (HEAD_DIM * SHIFT_MARGIN) * jnp.max(jnp.abs(qw)) * jnp.max(jnp.abs(kw))
    e_q = jnp.zeros((1, LANES), F32).at[0, HEAD_DIM].set(1.0)
    e_k = jnp.zeros((1, LANES), F32).at[0, HEAD_DIM].set(-shift_b)

    wpb = p['w_proj_b'][l].reshape(B_HEADS, HEAD_DIM, D_MODEL)
    wpb = jnp.pad(wpb, ((0, 0), (0, LANES - HEAD_DIM), (0, 0))).reshape(B_HEADS * LANES, D_MODEL)

    w_r = jnp.pad(jnp.concatenate([p['w_router_expert'][l], p['w_router_group'][l]], axis=-1).astype(F32),
                  ((0, 0), (0, LANES - N_EXPERTS - MOE_GROUPS)))
    w_r_hi = w_r.astype(BF16)
    merge_w = [
        p['norm1_w'][l][None, :], wg.astype(BF16), p['w_proj_a'][l].astype(BF16), wpb.astype(BF16),
        p['w_proj_c'][l].astype(BF16), p['w_glu'][l].astype(BF16), p['b_glu'][l][None, :], p['ssm_D'][l][None, :],
        p['w_out'][l].astype(BF16), p['norm2_w'][l][None, :],
        w_r_hi, (w_r - w_r_hi.astype(F32)).astype(BF16),
        _pad_vec(jnp.concatenate([p['b_router_expert'][l], p['b_router_group'][l]])),
    ]
    lam_init = 0.8 - 0.6 * math.exp(-0.3 * l)
    lamv = jnp.stack([_pad_vec(v[l])[0] for v in (p['lam_q1'], p['lam_k1'], p['lam_q2'], p['lam_k2'])])
    return dict(
        norm1=p['norm1_w'][l][None, :], w_a=w_a, e_a=e_a, w_b=w_b, e_b=e_b, w_c=wc.astype(BF16),
        qw=_pad_vec(qw), kw=_pad_vec(kw), e_q=e_q, e_k=e_k, shift_b=shift_b.reshape(1),
        lamv=lamv, lam_init=lam_init, subw=p['diff_subln_w'][l][None, :],
        ssm=_ssm_operators(p['ssm_A_re'][l], p['ssm_A_im'][l], p['ssm_log_dt'][l], p['ssm_B_re'][l],
                           p['ssm_B_im'][l], p['ssm_C_re'][l], p['ssm_C_im'][l]),
        merge_w=merge_w,
        wg=p['w_exp_gate'], wu=p['w_exp_up'], wd=p['w_exp_down'],
    )


def _rope_tables(L):
    half = HEAD_DIM // 2
    inv = ROPE_BASE ** (-jnp.arange(0, half, 2, dtype=F32) / half)
    t = jnp.arange(L)
    row = (t // GRID_W).astype(F32)
    col = (t % GRID_W).astype(F32)
    ang = jnp.concatenate([row[:, None] * inv[None, :]] * 2 + [col[:, None] * inv[None, :]] * 2, axis=-1)
    cos, sin = jnp.cos(ang), jnp.sin(ang)
    first = (jnp.arange(HEAD_DIM) % half) < (half // 2)
    pad = ((0, 0), (0, LANES - HEAD_DIM))
    return (jnp.pad(cos, pad), jnp.pad(jnp.where(first, -sin, 0.0), pad), jnp.pad(jnp.where(first, 0.0, sin), pad))


def _alibi_slopes():
    s = 2.0 ** (-8.0 * jnp.arange(1, A_HEADS + 1, dtype=F32) / A_HEADS)
    return jnp.broadcast_to(s[:, None, None], (A_HEADS, 1, LANES))


def _trunk(x, layers, final_norm_w):
    Bsz, L, _ = x.shape
    T = Bsz * L
    tm = ROW_TILE
    n_chunks = L // SSM_CHUNK
    pos_blocks = L // tm
    cos, sa, sb = _rope_tables(L)
    slopes = _alibi_slopes()
    fw = final_norm_w[None, :]
    x2d = x.reshape(T, D_MODEL)

    for l, lp in enumerate(layers):
        qkv_a, qkv_b, zc32, zc16 = _project(x2d, lp, (cos, sa, sb), pos_blocks)
        out_a = _diff_attention(qkv_a.reshape(Bsz, L, -1), slopes, lp['lamv'], lp['subw'], lp['lam_init'], Bsz, L)
        out_b = _gqa_attention(qkv_b.reshape(Bsz, L, -1), lp['shift_b'], Bsz, L)

        u_g = jnp.transpose(zc16.reshape(Bsz, n_chunks, SSM_CHUNK, SSM_GROUPS, SSM_P), (3, 0, 1, 2, 4))
        u_g = u_g.reshape(SSM_GROUPS, Bsz * n_chunks, SSM_FLAT)
        y_g = _ssm_apply(u_g, *lp['ssm'], Bsz, n_chunks)
        y_s = jnp.transpose(y_g.reshape(SSM_GROUPS, Bsz, n_chunks, SSM_CHUNK, SSM_P), (1, 2, 3, 0, 4))
        y_s = y_s.reshape(T, SSM_WIDTH)

        x1, xn2, comb = _merge(x2d, out_a.reshape(T, A_WIDTH), out_b.reshape(T, B_HEADS * LANES), y_s, zc32,
                               lp['merge_w'])
        x2d = _moe(xn2, comb, x1, lp['wg'], lp['wu'], lp['wd'], fw, layer=l, final_norm=(l == len(layers) - 1))
    return x2d.reshape(Bsz, L, D_MODEL)


def kernel(x_prompt, x_sample, norm1_w, w_in, lam_q1, lam_k1, lam_q2, lam_k2, diff_subln_w, q_norm_w, k_norm_w,
           ssm_A_re, ssm_A_im, ssm_log_dt, ssm_B_re, ssm_B_im, ssm_C_re, ssm_C_im, ssm_D, w_glu, b_glu,
           w_proj_a, w_proj_b, w_proj_c, w_out, norm2_w, w_router_group, b_router_group, w_router_expert,
           b_router_expert, w_exp_gate, w_exp_up, w_exp_down, final_norm_w):
    p = dict(norm1_w=norm1_w, w_in=w_in, lam_q1=lam_q1, lam_k1=lam_k1, lam_q2=lam_q2, lam_k2=lam_k2,
             diff_subln_w=diff_subln_w, q_norm_w=q_norm_w, k_norm_w=k_norm_w, ssm_A_re=ssm_A_re, ssm_A_im=ssm_A_im,
             ssm_log_dt=ssm_log_dt, ssm_B_re=ssm_B_re, ssm_B_im=ssm_B_im, ssm_C_re=ssm_C_re, ssm_C_im=ssm_C_im,
             ssm_D=ssm_D, w_glu=w_glu, b_glu=b_glu, w_proj_a=w_proj_a, w_proj_b=w_proj_b, w_proj_c=w_proj_c,
             w_out=w_out, norm2_w=norm2_w, w_router_group=w_router_group, b_router_group=b_router_group,
             w_router_expert=w_router_expert, b_router_expert=b_router_expert, w_exp_gate=w_exp_gate,
             w_exp_up=w_exp_up, w_exp_down=w_exp_down)
    layers = [_layer_params(l, p) for l in range(DEPTH)]
    return (_trunk(x_prompt, layers, final_norm_w), _trunk(x_sample, layers, final_norm_w))
```

```python
import functools
import math

import jax
import jax.numpy as jnp
from jax import lax
from jax.experimental import pallas as pl
from jax.experimental.pallas import tpu as pltpu

F32 = jnp.float32
BF16 = jnp.bfloat16

D_MODEL = 1024
DEPTH = 2
HEAD_DIM = 64
EPS = 1e-6
A_HEADS = 4
A_VDIM = 2 * HEAD_DIM
A_QK_COLS = A_HEADS * 2 * HEAD_DIM
A_WIDTH = A_HEADS * A_VDIM
COLS_A = 2 * A_QK_COLS + A_WIDTH
B_HEADS = 8
B_KV_HEADS = 2
B_GROUP = B_HEADS // B_KV_HEADS
B_WIDTH = B_HEADS * HEAD_DIM
COLS_B = B_HEADS * HEAD_DIM + 2 * B_KV_HEADS * HEAD_DIM
ROPE_BASE = 10000.0
GRID_W = 64
SSM_WIDTH = 512
SSM_P = 16
SSM_GROUPS = SSM_WIDTH // SSM_P
SSM_N = 64
COLS_C = SSM_WIDTH
N_BRANCH = 3
MOE_GROUPS = 4
EXPERTS_PER_GROUP = 4
N_EXPERTS = MOE_GROUPS * EXPERTS_PER_GROUP
D_FF_EXPERT = 512

LANES = 128
SUBLANES = 8
VMEM_LIMIT_BYTES = 60 * 1024 * 1024

SSM_CHUNK = 64
SSM_FLAT = SSM_CHUNK * SSM_P
SSM_STATE_COLS = 4 * LANES
SSM_SHIFTS = LANES // SSM_P
SSM_K2_WIDTH = (SSM_CHUNK - 1) // SSM_SHIFTS * LANES + SSM_FLAT

ROW_TILE = 512
MERGE_TILE = 512
EXPERT_TILE = 1024
EXPERTS_PER_STEP = 4
ATT_ROWS = 1024
ATT_KV_TILE = 4096
MAX_STATIC_SHIFT = 40.0
SHIFT_MARGIN = 1.02


def _compiler_params(semantics):
    return pltpu.CompilerParams(dimension_semantics=semantics, vmem_limit_bytes=VMEM_LIMIT_BYTES)


def _const_spec(shape):
    zeros = (0,) * len(shape)
    return pl.BlockSpec(shape, lambda *_: zeros)


def _rms(x, w):
    ms = jnp.mean(x * x, axis=-1, keepdims=True)
    return x * lax.rsqrt(ms + EPS) * w


def _proj_kernel(x_ref, nw_ref, wa_ref, wb_ref, wc_ref, ea_ref, qw_ref, kw_ref, eq_ref, ek_ref, ev_ref,
                 cos_ref, sa_ref, sb_ref, oa_ref, ob_ref, oc32_ref, oc16_ref):
    xn = _rms(x_ref[...], nw_ref[...]).astype(BF16)

    zb = jnp.dot(xn, wb_ref[...], preferred_element_type=F32)
    cos, sa, sb = cos_ref[...], sa_ref[...], sb_ref[...]

    def norm_rope(zh, w):
        ms = jnp.sum(zh * zh, axis=-1, keepdims=True) * (1.0 / HEAD_DIM)
        y = zh * lax.rsqrt(ms + EPS) * w
        return y * cos + pltpu.roll(y, LANES - 16, 1) * sa + pltpu.roll(y, 16, 1) * sb

    for h in range(B_HEADS + B_KV_HEADS):
        sl = slice(h * LANES, (h + 1) * LANES)
        w, e = (qw_ref, eq_ref) if h < B_HEADS else (kw_ref, ek_ref)
        ob_ref[:, sl] = (norm_rope(zb[:, sl], w[...]) + e[...]).astype(ob_ref.dtype)
    sl = slice((B_HEADS + B_KV_HEADS) * LANES, (B_HEADS + 2 * B_KV_HEADS) * LANES)
    ob_ref[:, sl] = (zb[:, sl] + ev_ref[...]).astype(ob_ref.dtype)

    oa_ref[...] = (jnp.dot(xn, wa_ref[...], preferred_element_type=F32) + ea_ref[...]).astype(oa_ref.dtype)

    zc = jnp.dot(xn, wc_ref[...], preferred_element_type=F32)
    oc32_ref[...] = zc
    oc16_ref[...] = zc.astype(BF16)


def _project(x2d, lp, tables, pos_blocks):
    T = x2d.shape[0]
    tm = ROW_TILE
    na, nb = lp['w_a'].shape[1], lp['w_b'].shape[1]

    def rows(width):
        return pl.BlockSpec((tm, width), lambda i: (i, 0))

    def whole(arr):
        return pl.BlockSpec(arr.shape, lambda i: (0, 0), pipeline_mode=pl.Buffered(1))

    consts = [lp['norm1'], lp['w_a'], lp['w_b'], lp['w_c'], lp['e_a'], lp['qw'], lp['kw'], lp['e_q'], lp['e_k'],
              lp['e_b']]
    table_spec = pl.BlockSpec((tm, LANES), lambda i: (i % pos_blocks, 0))
    return pl.pallas_call(
        _proj_kernel,
        grid=(T // tm,),
        in_specs=[rows(D_MODEL)] + [whole(c) for c in consts] + [table_spec] * 3,
        out_specs=[rows(na), rows(nb), rows(COLS_C), rows(COLS_C)],
        out_shape=[jax.ShapeDtypeStruct((T, na), BF16), jax.ShapeDtypeStruct((T, nb), BF16),
                   jax.ShapeDtypeStruct((T, COLS_C), F32), jax.ShapeDtypeStruct((T, COLS_C), BF16)],
        compiler_params=_compiler_params(("parallel",)),
        name="in_proj",
    )(x2d, *consts, *tables)


_NT = (((1,), (1,)), ((), ()))


def _online_softmax(qs, k_fn, v_ref, m_ref, acc_ref, tk, n_chunks, bias_fn):
    m_ref[...] = jnp.full(m_ref.shape, -jnp.inf, F32)
    acc_ref[...] = jnp.zeros(acc_ref.shape, F32)

    def body(c, carry):
        start = pl.multiple_of(c * tk, tk)
        s = lax.dot_general(qs, k_fn(start), _NT, preferred_element_type=F32)
        if bias_fn is not None:
            s = s + bias_fn(start)
        m_prev = m_ref[...]
        m_new = jnp.maximum(m_prev, jnp.max(s, axis=-1, keepdims=True))
        p = jnp.exp(s - m_new)
        alpha = jnp.exp(m_prev - m_new)
        pv = jnp.dot(p.astype(BF16), v_ref[pl.ds(start, tk), :], preferred_element_type=F32)
        acc_ref[...] = alpha * acc_ref[...] + pv
        m_ref[...] = m_new
        return carry

    lax.fori_loop(0, n_chunks, body, 0)


def _shifted_softmax(qs, k_fn, v_ref, acc_ref, tk, lo, hi, bias_fn=None):
    def body(c, carry):
        start = pl.multiple_of(c * tk, tk)
        s = lax.dot_general(qs, k_fn(start), _NT, preferred_element_type=F32)
        if bias_fn is not None:
            s = s + bias_fn(start)
        acc_ref[...] += jnp.dot(jnp.exp(s).astype(BF16), v_ref[pl.ds(start, tk), :], preferred_element_type=F32)
        return carry

    lax.fori_loop(lo, hi, body, 0)


def _half_norms(x, lane):
    sq = x.astype(F32)
    sq = sq * sq
    n1 = jnp.sum(jnp.where(lane < HEAD_DIM, sq, 0.0), axis=-1, keepdims=True)
    n2 = jnp.sum(jnp.where(lane >= HEAD_DIM, sq, 0.0), axis=-1, keepdims=True)
    return jnp.sqrt(n1), jnp.sqrt(n2)


def _diff_attn_kernel(q_ref, k_ref, v_ref, slope_ref, lam_ref, subw_ref, o_ref, kmax_ref, m_ref, acc_ref,
                      *, tq, tk, lam_init):
    i = pl.program_id(2)
    n_chunks = k_ref.shape[0] // tk
    slope = slope_ref[...][:, :1]

    @pl.when(i == 0)
    def _():
        lane_k = lax.broadcasted_iota(jnp.int32, (tk, LANES), 1)

        def body(c, carry):
            n1, n2 = _half_norms(k_ref[pl.ds(pl.multiple_of(c * tk, tk), tk), :], lane_k)
            return (jnp.maximum(carry[0], jnp.max(n1, axis=0, keepdims=True)),
                    jnp.maximum(carry[1], jnp.max(n2, axis=0, keepdims=True)))

        zero11 = jnp.zeros((1, 1), F32)
        k1, k2 = lax.fori_loop(0, n_chunks, body, (zero11, zero11))
        kmax_ref[0:1, :] = jnp.broadcast_to(k1, (1, LANES))
        kmax_ref[1:2, :] = jnp.broadcast_to(k2, (1, LANES))

    q = q_ref[...]
    lane = lax.broadcasted_iota(jnp.int32, q.shape, 1)
    zero = jnp.zeros_like(q)
    qz = jnp.concatenate([jnp.where(lane < HEAD_DIM, q, zero), jnp.where(lane >= HEAD_DIM, q, zero)], axis=0)

    qn1, qn2 = _half_norms(q, lane)
    shift = jnp.concatenate([qn1 * kmax_ref[0:1, 0:1], qn2 * kmax_ref[1:2, 0:1]], axis=0) * SHIFT_MARGIN
    static_ok = jnp.max(shift) <= MAX_STATIC_SHIFT

    row = lax.broadcasted_iota(jnp.int32, (2 * tq, 1), 0)
    q_slope_pos = (i * tq + jnp.where(row >= tq, row - tq, row)).astype(F32) * slope

    def k_fn(start):
        return k_ref[pl.ds(start, tk), :]

    def alibi(start):
        k_slope_pos = (start + lax.broadcasted_iota(jnp.int32, (1, tk), 1)).astype(F32) * slope
        return jnp.abs(q_slope_pos - k_slope_pos)

    @pl.when(static_ok)
    def _():
        acc_ref[...] = jnp.zeros(acc_ref.shape, F32)
        _shifted_softmax(qz, k_fn, v_ref, acc_ref, tk, 0, n_chunks, lambda start: -shift - alibi(start))

    @pl.when(jnp.logical_not(static_ok))
    def _():
        _online_softmax(qz, k_fn, v_ref, m_ref, acc_ref, tk, n_chunks, lambda start: -alibi(start))

    acc = acc_ref[...]
    o = acc[:, :A_VDIM] / acc[:, A_VDIM:A_VDIM + 1]
    lv = lam_ref[...]
    lam = (jnp.exp(jnp.sum(lv[0:1] * lv[1:2], axis=-1, keepdims=True))
           - jnp.exp(jnp.sum(lv[2:3] * lv[3:4], axis=-1, keepdims=True)) + lam_init)
    d = o[:tq] - lam * o[tq:]
    o_ref[...] = (_rms(d, subw_ref[...]) * (1.0 - lam_init)).astype(o_ref.dtype)


def _gqa_attn_kernel(shift_ref, q_ref, k_ref, v_ref, o_ref, m_ref, acc_ref, *, tq, tk):
    qs = jnp.concatenate([q_ref[:, h * LANES:(h + 1) * LANES] for h in range(B_GROUP)], axis=0)
    n_chunks = k_ref.shape[0] // tk
    static_ok = shift_ref[0] <= MAX_STATIC_SHIFT

    def k_fn(start):
        return k_ref[pl.ds(start, tk), :]

    @pl.when(static_ok)
    def _():
        acc_ref[...] = jnp.zeros(acc_ref.shape, F32)
        _shifted_softmax(qs, k_fn, v_ref, acc_ref, tk, 0, n_chunks)

    @pl.when(jnp.logical_not(static_ok))
    def _():
        _online_softmax(qs, k_fn, v_ref, m_ref, acc_ref, tk, n_chunks, None)

    acc = acc_ref[...]
    o = acc / acc[:, HEAD_DIM:HEAD_DIM + 1]
    for h in range(B_GROUP):
        o_ref[:, h * LANES:(h + 1) * LANES] = o[h * tq:(h + 1) * tq].astype(o_ref.dtype)


def _attn_tiles(L, stacked):
    return min(ATT_ROWS // stacked, L), min(ATT_KV_TILE, L)


def _resident(block_shape, index_map):
    return pl.BlockSpec(block_shape, index_map, pipeline_mode=pl.Buffered(1))


def _diff_attention(qkv, slopes, lamv, subw, lam_init, Bsz, L):
    tq, tk = _attn_tiles(L, 2)
    nq = A_HEADS
    kernel = functools.partial(_diff_attn_kernel, tq=tq, tk=tk, lam_init=lam_init)
    return pl.pallas_call(
        kernel,
        grid=(Bsz, A_HEADS, L // tq),
        in_specs=[
            pl.BlockSpec((None, tq, LANES), lambda b, h, i: (b, i, h)),
            _resident((None, L, LANES), lambda b, h, i: (b, 0, nq + h)),
            _resident((None, L, 2 * LANES), lambda b, h, i: (b, 0, nq + h)),
            pl.BlockSpec((None, 1, LANES), lambda b, h, i: (h, 0, 0)),
            _const_spec((4, LANES)),
            _const_spec((1, A_VDIM)),
        ],
        out_specs=pl.BlockSpec((None, tq, LANES), lambda b, h, i: (b, i, h)),
        out_shape=jax.ShapeDtypeStruct((Bsz, L, A_WIDTH), BF16),
        scratch_shapes=[pltpu.VMEM((SUBLANES, LANES), F32),
                        pltpu.VMEM((2 * tq, 1), F32), pltpu.VMEM((2 * tq, 2 * LANES), F32)],
        compiler_params=_compiler_params(("parallel", "parallel", "arbitrary")),
        name="diff_attn",
    )(qkv, qkv, qkv, slopes, lamv, subw)


def _gqa_attention(qkv, shift, Bsz, L):
    tq, tk = _attn_tiles(L, B_GROUP)
    q_blocks = B_HEADS // B_GROUP
    kernel = functools.partial(_gqa_attn_kernel, tq=tq, tk=tk)
    return pl.pallas_call(
        kernel,
        grid=(Bsz, B_KV_HEADS, L // tq),
        in_specs=[
            pl.BlockSpec(memory_space=pltpu.SMEM),
            pl.BlockSpec((None, tq, B_GROUP * LANES), lambda b, g, i: (b, i, g)),
            _resident((None, L, LANES), lambda b, g, i: (b, 0, B_HEADS + g)),
            _resident((None, L, LANES), lambda b, g, i: (b, 0, B_HEADS + B_KV_HEADS + g)),
        ],
        out_specs=pl.BlockSpec((None, tq, B_GROUP * LANES), lambda b, g, i: (b, i, g)),
        out_shape=jax.ShapeDtypeStruct((Bsz, L, q_blocks * B_GROUP * LANES), BF16),
        scratch_shapes=[pltpu.VMEM((B_GROUP * tq, 1), F32), pltpu.VMEM((B_GROUP * tq, LANES), F32)],
        compiler_params=_compiler_params(("parallel", "parallel", "arbitrary")),
        name="gqa_attn",
    )(shift, qkv, qkv, qkv)


def _ssm_kernel(u_ref, k2_ref, wst_ref, wout_ref, a_ref, y_ref, m_ref, s_ref, h_ref, *, n_batch, n_chunks):
    for s in range(SSM_CHUNK):
        first_lag = SSM_CHUNK - 1 - s
        j, a = first_lag % SSM_SHIFTS, first_lag // SSM_SHIFTS
        m_ref[s * SSM_P:(s + 1) * SSM_P, :] = k2_ref[j, :, a * LANES:a * LANES + SSM_FLAT]
    u = u_ref[...]
    y_ref[...] = jnp.dot(u, m_ref[...], preferred_element_type=F32)
    s_ref[...] = jnp.dot(u, wst_ref[...], preferred_element_type=F32)
    a = a_ref[...]
    afr, afi = a[:, 0:LANES], a[:, LANES:2 * LANES]
    abr, abi = a[:, 2 * LANES:3 * LANES], a[:, 3 * LANES:4 * LANES]
    zero = jnp.zeros((1, LANES), F32)

    def step(cg, carry):
        new = []
        for b in range(n_batch):
            hr, hi, gr, gi = carry[4 * b:4 * b + 4]
            rf = pl.multiple_of(b * n_chunks + cg * SUBLANES, SUBLANES)
            rb = pl.multiple_of(b * n_chunks + n_chunks - SUBLANES - cg * SUBLANES, SUBLANES)
            sf = s_ref[pl.ds(rf, SUBLANES), 0:2 * LANES]
            sb = s_ref[pl.ds(rb, SUBLANES), 2 * LANES:4 * LANES]
            rows_f, rows_b = [], [None] * SUBLANES
            for j in range(SUBLANES):
                rows_f.append(jnp.concatenate([hr, hi], axis=1))
                hr, hi = (afr * hr - afi * hi + sf[j:j + 1, 0:LANES],
                          afr * hi + afi * hr + sf[j:j + 1, LANES:2 * LANES])
                jb = SUBLANES - 1 - j
                rows_b[jb] = jnp.concatenate([gr, gi], axis=1)
                gr, gi = (abr * gr - abi * gi + sb[jb:jb + 1, 0:LANES],
                          abr * gi + abi * gr + sb[jb:jb + 1, LANES:2 * LANES])
            h_ref[pl.ds(rf, SUBLANES), 0:2 * LANES] = jnp.concatenate(rows_f, axis=0)
            h_ref[pl.ds(rb, SUBLANES), 2 * LANES:4 * LANES] = jnp.concatenate(rows_b, axis=0)
            new += [hr, hi, gr, gi]
        return tuple(new)

    lax.fori_loop(0, n_chunks // SUBLANES, step, (zero,) * (4 * n_batch))
    y_ref[...] += jnp.dot(h_ref[...].astype(BF16), wout_ref[...], preferred_element_type=F32)


def _ssm_apply(u_g, k2, wst, wout, a64, n_batch, n_chunks):
    nc = n_batch * n_chunks
    kernel = functools.partial(_ssm_kernel, n_batch=n_batch, n_chunks=n_chunks)
    return pl.pallas_call(
        kernel,
        grid=(SSM_GROUPS,),
        in_specs=[
            pl.BlockSpec((None, nc, SSM_FLAT), lambda g: (g, 0, 0)),
            pl.BlockSpec((None, SSM_SHIFTS, SSM_P, SSM_K2_WIDTH), lambda g: (g, 0, 0, 0)),
            pl.BlockSpec((None, SSM_FLAT, SSM_STATE_COLS), lambda g: (g, 0, 0)),
            pl.BlockSpec((None, SSM_STATE_COLS, SSM_FLAT), lambda g: (g, 0, 0)),
            pl.BlockSpec((None, 1, SSM_STATE_COLS), lambda g: (g, 0, 0)),
        ],
        out_specs=pl.BlockSpec((None, nc, SSM_FLAT), lambda g: (g, 0, 0)),
        out_shape=jax.ShapeDtypeStruct((SSM_GROUPS, nc, SSM_FLAT), F32),
        scratch_shapes=[pltpu.VMEM((SSM_FLAT, SSM_FLAT), BF16),
                        pltpu.VMEM((nc, SSM_STATE_COLS), F32), pltpu.VMEM((nc, SSM_STATE_COLS), F32)],
        compiler_params=_compiler_params(("parallel",)),
        name="ssm_chunks",
    )(u_g, k2, wst, wout, a64)


def _ssm_operators(A_re, A_im, log_dt, B_re, B_im, C_re, C_im):
    hi = lax.Precision.HIGHEST
    G, N, P, Tc = SSM_GROUPS, SSM_N, SSM_P, SSM_CHUNK
    Br, Bi, Cr, Ci = B_re.astype(F32), B_im.astype(F32), C_re.astype(F32), C_im.astype(F32)
    steps = jnp.arange(Tc + 1, dtype=F32)

    per_dir = []
    for dirn in range(2):
        dt = jnp.exp(log_dt[dirn].astype(F32))[:, None]
        ar, ai = A_re[dirn].astype(F32), A_im[dirn].astype(F32)
        mag = jnp.exp(dt * ar)
        er, ei = mag * jnp.cos(dt * ai), mag * jnp.sin(dt * ai)
        den = ar * ar + ai * ai
        fr = ((er - 1.0) * ar + ei * ai) / den
        fi = (ei * ar - (er - 1.0) * ai) / den
        bbr = fr[..., None] * Br - fi[..., None] * Bi
        bbi = fr[..., None] * Bi + fi[..., None] * Br
        pm = jnp.exp(steps[None, :, None] * (dt * ar)[:, None, :])
        ang = steps[None, :, None] * (dt * ai)[:, None, :]
        pr, pi = pm * jnp.cos(ang), pm * jnp.sin(ang)
        car = Cr[:, None] * pr[:, :, None, :] - Ci[:, None] * pi[:, :, None, :]
        cai = Cr[:, None] * pi[:, :, None, :] + Ci[:, None] * pr[:, :, None, :]
        kern = (jnp.einsum('gkpn,gnq->gkpq', car[:, :Tc], bbr, precision=hi)
                - jnp.einsum('gkpn,gnq->gkpq', cai[:, :Tc], bbi, precision=hi))
        bbr_t, bbi_t = jnp.swapaxes(bbr, 1, 2)[:, None], jnp.swapaxes(bbi, 1, 2)[:, None]
        prk, pik = pr[:, :, None, :], pi[:, :, None, :]
        abr = prk * bbr_t - pik * bbi_t
        abi = prk * bbi_t + pik * bbr_t
        cr_t, ci_t = jnp.swapaxes(Cr, 1, 2)[:, :, None, :], jnp.swapaxes(Ci, 1, 2)[:, :, None, :]
        prn, pin = jnp.swapaxes(pr, 1, 2)[..., None], jnp.swapaxes(pi, 1, 2)[..., None]
        per_dir.append(dict(kern=kern, abr=abr, abi=abi, pr=pr, pi=pi,
                            car=cr_t * prn - ci_t * pin, cai=cr_t * pin + ci_t * prn))

    f, b = per_dir
    k2 = jnp.concatenate([b['kern'][:, :0:-1], f['kern'][:, :1] + b['kern'][:, :1], f['kern'][:, 1:]], axis=1)
    k2 = jnp.transpose(k2, (0, 3, 1, 2)).astype(BF16).reshape(G, P, (2 * Tc - 1) * P)
    k2 = jnp.stack([k2[:, :, j * P:j * P + SSM_K2_WIDTH] for j in range(SSM_SHIFTS)], axis=1)

    pad = jnp.zeros((G, Tc * P, LANES - N), BF16)

    def st_cols(xr):
        return jnp.concatenate([xr.astype(BF16).reshape(G, Tc * P, N), pad], axis=-1)

    wst = jnp.concatenate([st_cols(f['abr'][:, Tc - 1::-1][:, :Tc]), st_cols(f['abi'][:, Tc - 1::-1][:, :Tc]),
                           st_cols(b['abr'][:, :Tc]), st_cols(b['abi'][:, :Tc])], axis=-1)

    padr = jnp.zeros((G, LANES - N, Tc * P), BF16)

    def out_rows(x):
        return jnp.concatenate([x.astype(BF16).reshape(G, N, Tc * P), padr], axis=1)

    wout = jnp.concatenate([out_rows(f['car'][:, :, 1:Tc + 1]), out_rows(-f['cai'][:, :, 1:Tc + 1]),
                            out_rows(b['car'][:, :, Tc:0:-1]), out_rows(-b['cai'][:, :, Tc:0:-1])], axis=1)

    padc = jnp.zeros((G, LANES - N), F32)
    a64 = jnp.concatenate([f['pr'][:, Tc], padc, f['pi'][:, Tc], padc,
                           b['pr'][:, Tc], padc, b['pi'][:, Tc], padc], axis=-1)[:, None, :]
    return k2, wst, wout, a64


def _merge_kernel(x_ref, oa_ref, ob_ref, ys_ref, zc_ref, n1_ref, wg_ref, wpa_ref, wpb_ref, wpc_ref,
                  wglu_ref, bglu_ref, dskip_ref, wout_ref, n2_ref, wr_hi_ref, wr_lo_ref, br_ref,
                  x1_ref, xn2_ref, comb_ref):
    x = x_ref[...]
    xn = _rms(x, n1_ref[...]).astype(BF16)
    def gate(branch):
        w = wg_ref[:, branch * D_MODEL:(branch + 1) * D_MODEL]
        return jax.nn.sigmoid(jnp.dot(xn, w, preferred_element_type=F32))

    y = ys_ref[...] + dskip_ref[...] * zc_ref[...]
    g = jax.nn.gelu(y)
    glu = jnp.dot(g.astype(BF16), wglu_ref[...], preferred_element_type=F32) + bglu_ref[...]
    out_c = g * jax.nn.sigmoid(glu)

    merged = gate(0) * jnp.dot(oa_ref[...], wpa_ref[...], preferred_element_type=F32)
    merged += gate(1) * jnp.dot(ob_ref[...], wpb_ref[...], preferred_element_type=F32)
    merged += gate(2) * jnp.dot(out_c.astype(BF16), wpc_ref[...], preferred_element_type=F32)
    x1 = x + jnp.dot(merged.astype(BF16), wout_ref[...], preferred_element_type=F32)
    x1_ref[...] = x1

    xn2 = _rms(x1, n2_ref[...])
    xn2_ref[...] = xn2.astype(BF16)

    lane = lax.broadcasted_iota(jnp.int32, (x.shape[0], LANES), 1).astype(F32)
    neg = jnp.float32(-jnp.inf)
    big = jnp.float32(LANES)
    x_hi = xn2.astype(BF16)
    x_lo = (xn2 - x_hi.astype(F32)).astype(BF16)
    le = (jnp.dot(x_hi, wr_hi_ref[...], preferred_element_type=F32)
          + (jnp.dot(x_hi, wr_lo_ref[...], preferred_element_type=F32)
             + jnp.dot(x_lo, wr_hi_ref[...], preferred_element_type=F32))) + br_ref[...]
    lg = jnp.where((lane >= N_EXPERTS) & (lane < N_EXPERTS + MOE_GROUPS), le, neg)
    g_max = jnp.max(lg, axis=-1, keepdims=True)
    g_idx = jnp.min(jnp.where(lg == g_max, lane, big), axis=-1, keepdims=True) - float(N_EXPERTS)
    g_w = 1.0 / jnp.sum(jnp.exp(lg - g_max), axis=-1, keepdims=True)
    in_group = (lane >= g_idx * EXPERTS_PER_GROUP) & (lane < (g_idx + 1) * EXPERTS_PER_GROUP)
    v = jnp.where(in_group, le, neg)
    top1 = jnp.max(v, axis=-1, keepdims=True)
    i1 = jnp.min(jnp.where(v == top1, lane, big), axis=-1, keepdims=True)
    v2 = jnp.where(lane == i1, neg, v)
    top2 = jnp.max(v2, axis=-1, keepdims=True)
    i2 = jnp.min(jnp.where(v2 == top2, lane, big), axis=-1, keepdims=True)
    e2 = jnp.exp(top2 - top1)
    inv = 1.0 / (1.0 + e2)
    comb_ref[...] = (jnp.where(lane == i1, inv * g_w, 0.0) + jnp.where(lane == i2, e2 * inv * g_w, 0.0))


def _merge(x2d, oa, ob, ys, zc, weights):
    T = x2d.shape[0]
    tm = MERGE_TILE

    def rows(width):
        return pl.BlockSpec((tm, width), lambda i: (i, 0))

    w_specs = [pl.BlockSpec(w.shape, lambda i, nd=w.ndim: (0,) * nd, pipeline_mode=pl.Buffered(1)) for w in weights]
    return pl.pallas_call(
        _merge_kernel,
        grid=(T // tm,),
        in_specs=[rows(D_MODEL), rows(A_WIDTH), rows(B_HEADS * LANES), rows(SSM_WIDTH), rows(SSM_WIDTH)] + w_specs,
        out_specs=[rows(D_MODEL), rows(D_MODEL), rows(LANES)],
        out_shape=[jax.ShapeDtypeStruct((T, D_MODEL), F32), jax.ShapeDtypeStruct((T, D_MODEL), BF16),
                   jax.ShapeDtypeStruct((T, LANES), F32)],
        compiler_params=_compiler_params(("parallel",)),
        name="merge_router",
    )(x2d, oa, ob, ys, zc, *weights)


def _moe_kernel(xn_ref, comb_ref, x1_ref, wg_ref, wu_ref, wd_ref, fw_ref, o_ref, acc_ref, *, final_norm):
    step = pl.program_id(1)

    @pl.when(step == 0)
    def _():
        acc_ref[...] = jnp.zeros(acc_ref.shape, F32)

    xn = xn_ref[...]
    comb = comb_ref[...]
    lane = lax.broadcasted_iota(jnp.int32, comb.shape, 1)
    hidden = []
    for j in range(EXPERTS_PER_STEP):
        h = jax.nn.silu(jnp.dot(xn, wg_ref[j], preferred_element_type=F32)) * jnp.dot(
            xn, wu_ref[j], preferred_element_type=F32)
        c = jnp.sum(jnp.where(lane == step * EXPERTS_PER_STEP + j, comb, 0.0), axis=-1, keepdims=True)
        hidden.append((c * h).astype(BF16))
    wd = wd_ref[...].reshape(EXPERTS_PER_STEP * D_FF_EXPERT, D_MODEL)
    acc_ref[...] += jnp.dot(jnp.concatenate(hidden, axis=1), wd, preferred_element_type=F32)

    @pl.when(step == N_EXPERTS // EXPERTS_PER_STEP - 1)
    def _():
        out = x1_ref[...] + acc_ref[...]
        if final_norm:
            out = _rms(out, fw_ref[...])
        o_ref[...] = out


def _moe(xn2, comb, x1, wg, wu, wd, fw, layer, final_norm):
    T = x1.shape[0]
    tm = EXPERT_TILE
    kernel = functools.partial(_moe_kernel, final_norm=final_norm)
    return pl.pallas_call(
        kernel,
        grid=(T // tm, N_EXPERTS // EXPERTS_PER_STEP),
        in_specs=[
            pl.BlockSpec((tm, D_MODEL), lambda i, e: (i, 0)),
            pl.BlockSpec((tm, LANES), lambda i, e: (i, 0)),
            pl.BlockSpec((tm, D_MODEL), lambda i, e: (i, 0)),
            pl.BlockSpec((None, EXPERTS_PER_STEP, D_MODEL, D_FF_EXPERT), lambda i, e: (layer, e, 0, 0)),
            pl.BlockSpec((None, EXPERTS_PER_STEP, D_MODEL, D_FF_EXPERT), lambda i, e: (layer, e, 0, 0)),
            pl.BlockSpec((None, EXPERTS_PER_STEP, D_FF_EXPERT, D_MODEL), lambda i, e: (layer, e, 0, 0)),
            _const_spec((1, D_MODEL)),
        ],
        out_specs=pl.BlockSpec((tm, D_MODEL), lambda i, e: (i, 0)),
        out_shape=jax.ShapeDtypeStruct((T, D_MODEL), F32),
        scratch_shapes=[pltpu.VMEM((tm, D_MODEL), F32)],
        compiler_params=_compiler_params(("parallel", "arbitrary")),
        name="experts",
    )(xn2, comb, x1, wg, wu, wd, fw)


def _pad_heads(w, n_heads, width):
    w = w.reshape(D_MODEL, n_heads, -1)
    return jnp.pad(w, ((0, 0), (0, 0), (0, width - w.shape[-1]))).reshape(D_MODEL, n_heads * width)


def _pad_vec(v, width=LANES):
    return jnp.pad(v.astype(F32), (0, width - v.shape[0]))[None, :]


def _layer_params(l, p):
    scale = HEAD_DIM ** -0.5
    w_in = p['w_in'][l]
    wa, wb, wc, wg = jnp.split(w_in, [COLS_A, COLS_A + COLS_B, COLS_A + COLS_B + COLS_C], axis=-1)

    wqa, wka, wva = jnp.split(wa, [A_QK_COLS, 2 * A_QK_COLS], axis=-1)
    wva = _pad_heads(wva, A_HEADS, 2 * LANES)
    w_a = jnp.concatenate([wqa * scale, wka, wva], axis=-1).astype(BF16)
    ones_a = jnp.zeros((A_HEADS, 2 * LANES), F32).at[:, A_VDIM].set(1.0).reshape(1, -1)
    e_a = jnp.concatenate([jnp.zeros((1, 2 * A_QK_COLS), F32), ones_a], axis=-1)

    wqb, wkb, wvb = jnp.split(wb, [B_HEADS * HEAD_DIM, (B_HEADS + B_KV_HEADS) * HEAD_DIM], axis=-1)
    w_b = jnp.concatenate([_pad_heads(wqb, B_HEADS, LANES), _pad_heads(wkb, B_KV_HEADS, LANES),
                           _pad_heads(wvb, B_KV_HEADS, LANES)], axis=-1).astype(BF16)
    e_b = jnp.zeros((B_KV_HEADS, LANES), F32).at[:, HEAD_DIM].set(1.0).reshape(1, -1)
    qw = p['q_norm_w'][l].astype(F32) * scale
    kw = p['k_norm_w'][l].astype(F32)
    shift_b = (HEAD_DIM * SHIFT_MARGIN) * jnp.max(jnp.abs(qw)) * jnp.max(jnp.abs(kw))
    e_q = jnp.zeros((1, LANES), F32).at[0, HEAD_DIM].set(1.0)
    e_k = jnp.zeros((1, LANES), F32).at[0, HEAD_DIM].set(-shift_b)

    wpb = p['w_proj_b'][l].reshape(B_HEADS, HEAD_DIM, D_MODEL)
    wpb = jnp.pad(wpb, ((0, 0), (0, LANES - HEAD_DIM), (0, 0))).reshape(B_HEADS * LANES, D_MODEL)

    w_r = jnp.pad(jnp.concatenate([p['w_router_expert'][l], p['w_router_group'][l]], axis=-1).astype(F32),
                  ((0, 0), (0, LANES - N_EXPERTS - MOE_GROUPS)))
    w_r_hi = w_r.astype(BF16)
    merge_w = [
        p['norm1_w'][l][None, :], wg.astype(BF16), p['w_proj_a'][l].astype(BF16), wpb.astype(BF16),
        p['w_proj_c'][l].astype(BF16), p['w_glu'][l].astype(BF16), p['b_glu'][l][None, :], p['ssm_D'][l][None, :],
        p['w_out'][l].astype(BF16), p['norm2_w'][l][None, :],
        w_r_hi, (w_r - w_r_hi.astype(F32)).astype(BF16),
        _pad_vec(jnp.concatenate([p['b_router_expert'][l], p['b_router_group'][l]])),
    ]
    lam_init = 0.8 - 0.6 * math.exp(-0.3 * l)
    lamv = jnp.stack([_pad_vec(v[l])[0] for v in (p['lam_q1'], p['lam_k1'], p['lam_q2'], p['lam_k2'])])
    return dict(
        norm1=p['norm1_w'][l][None, :], w_a=w_a, e_a=e_a, w_b=w_b, e_b=e_b, w_c=wc.astype(BF16),
        qw=_pad_vec(qw), kw=_pad_vec(kw), e_q=e_q, e_k=e_k, shift_b=shift_b.reshape(1),
        lamv=lamv, lam_init=lam_init, subw=p['diff_subln_w'][l][None, :],
        ssm=_ssm_operators(p['ssm_A_re'][l], p['ssm_A_im'][l], p['ssm_log_dt'][l], p['ssm_B_re'][l],
                           p['ssm_B_im'][l], p['ssm_C_re'][l], p['ssm_C_im'][l]),
        merge_w=merge_w,
        wg=p['w_exp_gate'].astype(BF16), wu=p['w_exp_up'].astype(BF16), wd=p['w_exp_down'].astype(BF16),
    )


def _rope_tables(L):
    half = HEAD_DIM // 2
    inv = ROPE_BASE ** (-jnp.arange(0, half, 2, dtype=F32) / half)
    t = jnp.arange(L)
    row = (t // GRID_W).astype(F32)
    col = (t % GRID_W).astype(F32)
    ang = jnp.concatenate([row[:, None] * inv[None, :]] * 2 + [col[:, None] * inv[None, :]] * 2, axis=-1)
    cos, sin = jnp.cos(ang), jnp.sin(ang)
    first = (jnp.arange(HEAD_DIM) % half) < (half // 2)
    pad = ((0, 0), (0, LANES - HEAD_DIM))
    return (jnp.pad(cos, pad), jnp.pad(jnp.where(first, -sin, 0.0), pad), jnp.pad(jnp.where(first, 0.0, sin), pad))


def _alibi_slopes():
    s = 2.0 ** (-8.0 * jnp.arange(1, A_HEADS + 1, dtype=F32) / A_HEADS)
    return jnp.broadcast_to(s[:, None, None], (A_HEADS, 1, LANES))


def _trunk(x, layers, final_norm_w):
    Bsz, L, _ = x.shape
    T = Bsz * L
    tm = ROW_TILE
    n_chunks = L // SSM_CHUNK
    pos_blocks = L // tm
    cos, sa, sb = _rope_tables(L)
    slopes = _alibi_slopes()
    fw = final_norm_w[None, :]
    x2d = x.reshape(T, D_MODEL)

    for l, lp in enumerate(layers):
        qkv_a, qkv_b, zc32, zc16 = _project(x2d, lp, (cos, sa, sb), pos_blocks)
        out_a = _diff_attention(qkv_a.reshape(Bsz, L, -1), slopes, lp['lamv'], lp['subw'], lp['lam_init'], Bsz, L)
        out_b = _gqa_attention(qkv_b.reshape(Bsz, L, -1), lp['shift_b'], Bsz, L)

        u_g = jnp.transpose(zc16.reshape(Bsz, n_chunks, SSM_CHUNK, SSM_GROUPS, SSM_P), (3, 0, 1, 2, 4))
        u_g = u_g.reshape(SSM_GROUPS, Bsz * n_chunks, SSM_FLAT)
        y_g = _ssm_apply(u_g, *lp['ssm'], Bsz, n_chunks)
        y_s = jnp.transpose(y_g.reshape(SSM_GROUPS, Bsz, n_chunks, SSM_CHUNK, SSM_P), (1, 2, 3, 0, 4))
        y_s = y_s.reshape(T, SSM_WIDTH)

        x1, xn2, comb = _merge(x2d, out_a.reshape(T, A_WIDTH), out_b.reshape(T, B_HEADS * LANES), y_s, zc32,
                               lp['merge_w'])
        x2d = _moe(xn2, comb, x1, lp['wg'], lp['wu'], lp['wd'], fw, layer=l, final_norm=(l == len(layers) - 1))
    return x2d.reshape(Bsz, L, D_MODEL)


def kernel(x_prompt, x_sample, norm1_w, w_in, lam_q1, lam_k1, lam_q2, lam_k2, diff_subln_w, q_norm_w, k_norm_w,
           ssm_A_re, ssm_A_im, ssm_log_dt, ssm_B_re, ssm_B_im, ssm_C_re, ssm_C_im, ssm_D, w_glu, b_glu,
           w_proj_a, w_proj_b, w_proj_c, w_out, norm2_w, w_router_group, b_router_group, w_router_expert,
           b_router_expert, w_exp_gate, w_exp_up, w_exp_down, final_norm_w):
    p = dict(norm1_w=norm1_w, w_in=w_in, lam_q1=lam_q1, lam_k1=lam_k1, lam_q2=lam_q2, lam_k2=lam_k2,
             diff_subln_w=diff_subln_w, q_norm_w=q_norm_w, k_norm_w=k_norm_w, ssm_A_re=ssm_A_re, ssm_A_im=ssm_A_im,
             ssm_log_dt=ssm_log_dt, ssm_B_re=ssm_B_re, ssm_B_im=ssm_B_im, ssm_C_re=ssm_C_re, ssm_C_im=ssm_C_im,
             ssm_D=ssm_D, w_glu=w_glu, b_glu=b_glu, w_proj_a=w_proj_a, w_proj_b=w_proj_b, w_proj_c=w_proj_c,
             w_out=w_out, norm2_w=norm2_w, w_router_group=w_router_group, b_router_group=b_router_group,
             w_router_expert=w_router_expert, b_router_expert=b_router_expert, w_exp_gate=w_exp_gate,
             w_exp_up=w_exp_up, w_exp_down=w_exp_down)
    layers = [_layer_params(l, p) for l in range(DEPTH)]
    return (_trunk(x_prompt, layers, final_norm_w), _trunk(x_sample, layers, final_norm_w))
```

```python
import functools
import math

import jax
import jax.numpy as jnp
from jax import lax
from jax.experimental import pallas as pl
from jax.experimental.pallas import tpu as pltpu

F32 = jnp.float32
BF16 = jnp.bfloat16

D_MODEL = 1024
DEPTH = 2
HEAD_DIM = 64
EPS = 1e-6
A_HEADS = 4
A_VDIM = 2 * HEAD_DIM
A_QK_COLS = A_HEADS * 2 * HEAD_DIM
A_WIDTH = A_HEADS * A_VDIM
COLS_A = 2 * A_QK_COLS + A_WIDTH
B_HEADS = 8
B_KV_HEADS = 2
B_GROUP = B_HEADS // B_KV_HEADS
B_WIDTH = B_HEADS * HEAD_DIM
COLS_B = B_HEADS * HEAD_DIM + 2 * B_KV_HEADS * HEAD_DIM
ROPE_BASE = 10000.0
GRID_W = 64
SSM_WIDTH = 512
SSM_P = 16
SSM_GROUPS = SSM_WIDTH // SSM_P
SSM_N = 64
COLS_C = SSM_WIDTH
N_BRANCH = 3
MOE_GROUPS = 4
EXPERTS_PER_GROUP = 4
N_EXPERTS = MOE_GROUPS * EXPERTS_PER_GROUP
D_FF_EXPERT = 512

LANES = 128
SUBLANES = 8
VMEM_LIMIT_BYTES = 60 * 1024 * 1024

SSM_CHUNK = 64
SSM_FLAT = SSM_CHUNK * SSM_P
SSM_STATE_COLS = 4 * LANES
SSM_SHIFTS = LANES // SSM_P
SSM_K2_WIDTH = (SSM_CHUNK - 1) // SSM_SHIFTS * LANES + SSM_FLAT

ROW_TILE = 512
MERGE_TILE = 512
EXPERT_TILE = 1024
EXPERTS_PER_STEP = 4
ATT_ROWS = 1024
ATT_KV_TILE = 4096
MAX_STATIC_SHIFT = 40.0
SHIFT_MARGIN = 1.02


def _compiler_params(semantics):
    return pltpu.CompilerParams(dimension_semantics=semantics, vmem_limit_bytes=VMEM_LIMIT_BYTES)


def _const_spec(shape):
    zeros = (0,) * len(shape)
    return pl.BlockSpec(shape, lambda *_: zeros)


def _rms(x, w):
    ms = jnp.mean(x * x, axis=-1, keepdims=True)
    return x * lax.rsqrt(ms + EPS) * w


def _proj_kernel(x_ref, nw_ref, wa_ref, wb_ref, wc_ref, ea_ref, qw_ref, kw_ref, eq_ref, ek_ref, ev_ref,
                 cos_ref, sa_ref, sb_ref, oa_ref, ob_ref, oc32_ref, oc16_ref):
    xn = _rms(x_ref[...], nw_ref[...]).astype(BF16)

    zb = jnp.dot(xn, wb_ref[...], preferred_element_type=F32)
    cos, sa, sb = cos_ref[...], sa_ref[...], sb_ref[...]

    def norm_rope(zh, w):
        ms = jnp.sum(zh * zh, axis=-1, keepdims=True) * (1.0 / HEAD_DIM)
        y = zh * lax.rsqrt(ms + EPS) * w
        return y * cos + pltpu.roll(y, LANES - 16, 1) * sa + pltpu.roll(y, 16, 1) * sb

    for h in range(B_HEADS + B_KV_HEADS):
        sl = slice(h * LANES, (h + 1) * LANES)
        w, e = (qw_ref, eq_ref) if h < B_HEADS else (kw_ref, ek_ref)
        ob_ref[:, sl] = (norm_rope(zb[:, sl], w[...]) + e[...]).astype(ob_ref.dtype)
    sl = slice((B_HEADS + B_KV_HEADS) * LANES, (B_HEADS + 2 * B_KV_HEADS) * LANES)
    ob_ref[:, sl] = (zb[:, sl] + ev_ref[...]).astype(ob_ref.dtype)

    oa_ref[...] = (jnp.dot(xn, wa_ref[...], preferred_element_type=F32) + ea_ref[...]).astype(oa_ref.dtype)

    zc = jnp.dot(xn, wc_ref[...], preferred_element_type=F32)
    oc32_ref[...] = zc
    oc16_ref[...] = zc.astype(BF16)


def _project(x2d, lp, tables, pos_blocks):
    T = x2d.shape[0]
    tm = ROW_TILE
    na, nb = lp['w_a'].shape[1], lp['w_b'].shape[1]

    def rows(width):
        return pl.BlockSpec((tm, width), lambda i: (i, 0))

    def whole(arr):
        return pl.BlockSpec(arr.shape, lambda i: (0, 0), pipeline_mode=pl.Buffered(1))

    consts = [lp['norm1'], lp['w_a'], lp['w_b'], lp['w_c'], lp['e_a'], lp['qw'], lp['kw'], lp['e_q'], lp['e_k'],
              lp['e_b']]
    table_spec = pl.BlockSpec((tm, LANES), lambda i: (i % pos_blocks, 0))
    return pl.pallas_call(
        _proj_kernel,
        grid=(T // tm,),
        in_specs=[rows(D_MODEL)] + [whole(c) for c in consts] + [table_spec] * 3,
        out_specs=[rows(na), rows(nb), rows(COLS_C), rows(COLS_C)],
        out_shape=[jax.ShapeDtypeStruct((T, na), BF16), jax.ShapeDtypeStruct((T, nb), BF16),
                   jax.ShapeDtypeStruct((T, COLS_C), F32), jax.ShapeDtypeStruct((T, COLS_C), BF16)],
        compiler_params=_compiler_params(("parallel",)),
        name="in_proj",
    )(x2d, *consts, *tables)


_NT = (((1,), (1,)), ((), ()))


def _online_softmax(qs, k_fn, v_ref, m_ref, acc_ref, tk, n_chunks, bias_fn):
    m_ref[...] = jnp.full(m_ref.shape, -jnp.inf, F32)
    acc_ref[...] = jnp.zeros(acc_ref.shape, F32)

    def body(c, carry):
        start = pl.multiple_of(c * tk, tk)
        s = lax.dot_general(qs, k_fn(start), _NT, preferred_element_type=F32)
        if bias_fn is not None:
            s = s + bias_fn(start)
        m_prev = m_ref[...]
        m_new = jnp.maximum(m_prev, jnp.max(s, axis=-1, keepdims=True))
        p = jnp.exp(s - m_new)
        alpha = jnp.exp(m_prev - m_new)
        pv = jnp.dot(p.astype(BF16), v_ref[pl.ds(start, tk), :], preferred_element_type=F32)
        acc_ref[...] = alpha * acc_ref[...] + pv
        m_ref[...] = m_new
        return carry

    lax.fori_loop(0, n_chunks, body, 0)


def _shifted_softmax(qs, k_fn, v_ref, acc_ref, tk, lo, hi, bias_fn=None):
    def body(c, carry):
        start = pl.multiple_of(c * tk, tk)
        s = lax.dot_general(qs, k_fn(start), _NT, preferred_element_type=F32)
        if bias_fn is not None:
            s = s + bias_fn(start)
        acc_ref[...] += jnp.dot(jnp.exp(s).astype(BF16), v_ref[pl.ds(start, tk), :], preferred_element_type=F32)
        return carry

    lax.fori_loop(lo, hi, body, 0)


def _half_norms(x, lane):
    sq = x.astype(F32)
    sq = sq * sq
    n1 = jnp.sum(jnp.where(lane < HEAD_DIM, sq, 0.0), axis=-1, keepdims=True)
    n2 = jnp.sum(jnp.where(lane >= HEAD_DIM, sq, 0.0), axis=-1, keepdims=True)
    return jnp.sqrt(n1), jnp.sqrt(n2)


def _diff_attn_kernel(q_ref, k_ref, v_ref, slope_ref, lam_ref, subw_ref, o_ref, kmax_ref, m_ref, acc_ref,
                      *, tq, tk, lam_init):
    i = pl.program_id(2)
    n_chunks = k_ref.shape[0] // tk
    slope = slope_ref[...][:, :1]

    @pl.when(i == 0)
    def _():
        lane_k = lax.broadcasted_iota(jnp.int32, (tk, LANES), 1)

        def body(c, carry):
            n1, n2 = _half_norms(k_ref[pl.ds(pl.multiple_of(c * tk, tk), tk), :], lane_k)
            return (jnp.maximum(carry[0], jnp.max(n1, axis=0, keepdims=True)),
                    jnp.maximum(carry[1], jnp.max(n2, axis=0, keepdims=True)))

        zero11 = jnp.zeros((1, 1), F32)
        k1, k2 = lax.fori_loop(0, n_chunks, body, (zero11, zero11))
        kmax_ref[0:1, :] = jnp.broadcast_to(k1, (1, LANES))
        kmax_ref[1:2, :] = jnp.broadcast_to(k2, (1, LANES))

    q = q_ref[...]
    lane = lax.broadcasted_iota(jnp.int32, q.shape, 1)
    zero = jnp.zeros_like(q)
    qz = jnp.concatenate([jnp.where(lane < HEAD_DIM, q, zero), jnp.where(lane >= HEAD_DIM, q, zero)], axis=0)

    qn1, qn2 = _half_norms(q, lane)
    shift = jnp.concatenate([qn1 * kmax_ref[0:1, 0:1], qn2 * kmax_ref[1:2, 0:1]], axis=0) * SHIFT_MARGIN
    static_ok = jnp.max(shift) <= MAX_STATIC_SHIFT

    row = lax.broadcasted_iota(jnp.int32, (2 * tq, 1), 0)
    q_slope_pos = (i * tq + jnp.where(row >= tq, row - tq, row)).astype(F32) * slope

    def k_fn(start):
        return k_ref[pl.ds(start, tk), :]

    def alibi(start):
        k_slope_pos = (start + lax.broadcasted_iota(jnp.int32, (1, tk), 1)).astype(F32) * slope
        return jnp.abs(q_slope_pos - k_slope_pos)

    @pl.when(static_ok)
    def _():
        acc_ref[...] = jnp.zeros(acc_ref.shape, F32)
        _shifted_softmax(qz, k_fn, v_ref, acc_ref, tk, 0, n_chunks, lambda start: -shift - alibi(start))

    @pl.when(jnp.logical_not(static_ok))
    def _():
        _online_softmax(qz, k_fn, v_ref, m_ref, acc_ref, tk, n_chunks, lambda start: -alibi(start))

    acc = acc_ref[...]
    o = acc[:, :A_VDIM] / acc[:, A_VDIM:A_VDIM + 1]
    lv = lam_ref[...]
    lam = (jnp.exp(jnp.sum(lv[0:1] * lv[1:2], axis=-1, keepdims=True))
           - jnp.exp(jnp.sum(lv[2:3] * lv[3:4], axis=-1, keepdims=True)) + lam_init)
    d = o[:tq] - lam * o[tq:]
    o_ref[...] = (_rms(d, subw_ref[...]) * (1.0 - lam_init)).astype(o_ref.dtype)


def _gqa_attn_kernel(shift_ref, q_ref, k_ref, v_ref, o_ref, m_ref, acc_ref, *, tq, tk):
    qs = jnp.concatenate([q_ref[:, h * LANES:(h + 1) * LANES] for h in range(B_GROUP)], axis=0)
    n_chunks = k_ref.shape[0] // tk
    static_ok = shift_ref[0] <= MAX_STATIC_SHIFT

    def k_fn(start):
        return k_ref[pl.ds(start, tk), :]

    @pl.when(static_ok)
    def _():
        acc_ref[...] = jnp.zeros(acc_ref.shape, F32)
        _shifted_softmax(qs, k_fn, v_ref, acc_ref, tk, 0, n_chunks)

    @pl.when(jnp.logical_not(static_ok))
    def _():
        _online_softmax(qs, k_fn, v_ref, m_ref, acc_ref, tk, n_chunks, None)

    acc = acc_ref[...]
    o = acc / acc[:, HEAD_DIM:HEAD_DIM + 1]
    lane = lax.broadcasted_iota(jnp.int32, (tq, LANES), 1)
    for pair in range(B_GROUP // 2):
        first = o[2 * pair * tq:(2 * pair + 1) * tq]
        second = o[(2 * pair + 1) * tq:(2 * pair + 2) * tq]
        packed = jnp.where(lane < HEAD_DIM, first, pltpu.roll(second, HEAD_DIM, 1))
        o_ref[:, pair * LANES:(pair + 1) * LANES] = packed.astype(o_ref.dtype)


def _attn_tiles(L, stacked):
    return min(ATT_ROWS // stacked, L), min(ATT_KV_TILE, L)


def _resident(block_shape, index_map):
    return pl.BlockSpec(block_shape, index_map, pipeline_mode=pl.Buffered(1))


def _diff_attention(qkv, slopes, lamv, subw, lam_init, Bsz, L):
    tq, tk = _attn_tiles(L, 2)
    nq = A_HEADS
    kernel = functools.partial(_diff_attn_kernel, tq=tq, tk=tk, lam_init=lam_init)
    return pl.pallas_call(
        kernel,
        grid=(Bsz, A_HEADS, L // tq),
        in_specs=[
            pl.BlockSpec((None, tq, LANES), lambda b, h, i: (b, i, h)),
            _resident((None, L, LANES), lambda b, h, i: (b, 0, nq + h)),
            _resident((None, L, 2 * LANES), lambda b, h, i: (b, 0, nq + h)),
            pl.BlockSpec((None, 1, LANES), lambda b, h, i: (h, 0, 0)),
            _const_spec((4, LANES)),
            _const_spec((1, A_VDIM)),
        ],
        out_specs=pl.BlockSpec((None, tq, LANES), lambda b, h, i: (b, i, h)),
        out_shape=jax.ShapeDtypeStruct((Bsz, L, A_WIDTH), BF16),
        scratch_shapes=[pltpu.VMEM((SUBLANES, LANES), F32),
                        pltpu.VMEM((2 * tq, 1), F32), pltpu.VMEM((2 * tq, 2 * LANES), F32)],
        compiler_params=_compiler_params(("parallel", "parallel", "arbitrary")),
        name="diff_attn",
    )(qkv, qkv, qkv, slopes, lamv, subw)


def _gqa_attention(qkv, shift, Bsz, L):
    tq, tk = _attn_tiles(L, B_GROUP)
    kernel = functools.partial(_gqa_attn_kernel, tq=tq, tk=tk)
    return pl.pallas_call(
        kernel,
        grid=(Bsz, B_KV_HEADS, L // tq),
        in_specs=[
            pl.BlockSpec(memory_space=pltpu.SMEM),
            pl.BlockSpec((None, tq, B_GROUP * LANES), lambda b, g, i: (b, i, g)),
            _resident((None, L, LANES), lambda b, g, i: (b, 0, B_HEADS + g)),
            _resident((None, L, LANES), lambda b, g, i: (b, 0, B_HEADS + B_KV_HEADS + g)),
        ],
        out_specs=pl.BlockSpec((None, tq, B_GROUP * HEAD_DIM), lambda b, g, i: (b, i, g)),
        out_shape=jax.ShapeDtypeStruct((Bsz, L, B_WIDTH), BF16),
        scratch_shapes=[pltpu.VMEM((B_GROUP * tq, 1), F32), pltpu.VMEM((B_GROUP * tq, LANES), F32)],
        compiler_params=_compiler_params(("parallel", "parallel", "arbitrary")),
        name="gqa_attn",
    )(shift, qkv, qkv, qkv)


def _ssm_kernel(u_ref, k2_ref, wst_ref, wout_ref, a_ref, y_ref, m_ref, s_ref, h_ref, *, n_batch, n_chunks):
    for s in range(SSM_CHUNK):
        first_lag = SSM_CHUNK - 1 - s
        j, a = first_lag % SSM_SHIFTS, first_lag // SSM_SHIFTS
        m_ref[s * SSM_P:(s + 1) * SSM_P, :] = k2_ref[j, :, a * LANES:a * LANES + SSM_FLAT]
    u = u_ref[...]
    s_ref[...] = jnp.dot(u, wst_ref[...], preferred_element_type=F32)
    a = a_ref[...]
    afr, afi = a[:, 0:LANES], a[:, LANES:2 * LANES]
    abr, abi = a[:, 2 * LANES:3 * LANES], a[:, 3 * LANES:4 * LANES]
    zero = jnp.zeros((1, LANES), F32)

    def step(cg, carry):
        new = []
        for b in range(n_batch):
            hr, hi, gr, gi = carry[4 * b:4 * b + 4]
            rf = pl.multiple_of(b * n_chunks + cg * SUBLANES, SUBLANES)
            rb = pl.multiple_of(b * n_chunks + n_chunks - SUBLANES - cg * SUBLANES, SUBLANES)
            sf = s_ref[pl.ds(rf, SUBLANES), 0:2 * LANES]
            sb = s_ref[pl.ds(rb, SUBLANES), 2 * LANES:4 * LANES]
            rows_f, rows_b = [], [None] * SUBLANES
            for j in range(SUBLANES):
                rows_f.append(jnp.concatenate([hr, hi], axis=1))
                hr, hi = (afr * hr - afi * hi + sf[j:j + 1, 0:LANES],
                          afr * hi + afi * hr + sf[j:j + 1, LANES:2 * LANES])
                jb = SUBLANES - 1 - j
                rows_b[jb] = jnp.concatenate([gr, gi], axis=1)
                gr, gi = (abr * gr - abi * gi + sb[jb:jb + 1, 0:LANES],
                          abr * gi + abi * gr + sb[jb:jb + 1, LANES:2 * LANES])
            h_ref[pl.ds(rf, SUBLANES), 0:2 * LANES] = jnp.concatenate(rows_f, axis=0)
            h_ref[pl.ds(rb, SUBLANES), 2 * LANES:4 * LANES] = jnp.concatenate(rows_b, axis=0)
            new += [hr, hi, gr, gi]
        return tuple(new)

    lax.fori_loop(0, n_chunks // SUBLANES, step, (zero,) * (4 * n_batch))
    y = (jnp.dot(u, m_ref[...], preferred_element_type=F32)
         + jnp.dot(h_ref[...].astype(BF16), wout_ref[...], preferred_element_type=F32))
    y_ref[...] = y.astype(y_ref.dtype)


def _ssm_apply(u_g, k2, wst, wout, a64, n_batch, n_chunks):
    nc = n_batch * n_chunks
    kernel = functools.partial(_ssm_kernel, n_batch=n_batch, n_chunks=n_chunks)
    return pl.pallas_call(
        kernel,
        grid=(SSM_GROUPS,),
        in_specs=[
            pl.BlockSpec((None, nc, SSM_FLAT), lambda g: (g, 0, 0)),
            pl.BlockSpec((None, SSM_SHIFTS, SSM_P, SSM_K2_WIDTH), lambda g: (g, 0, 0, 0)),
            pl.BlockSpec((None, SSM_FLAT, SSM_STATE_COLS), lambda g: (g, 0, 0)),
            pl.BlockSpec((None, SSM_STATE_COLS, SSM_FLAT), lambda g: (g, 0, 0)),
            pl.BlockSpec((None, 1, SSM_STATE_COLS), lambda g: (g, 0, 0)),
        ],
        out_specs=pl.BlockSpec((None, nc, SSM_FLAT), lambda g: (g, 0, 0)),
        out_shape=jax.ShapeDtypeStruct((SSM_GROUPS, nc, SSM_FLAT), BF16),
        scratch_shapes=[pltpu.VMEM((SSM_FLAT, SSM_FLAT), BF16),
                        pltpu.VMEM((nc, SSM_STATE_COLS), F32), pltpu.VMEM((nc, SSM_STATE_COLS), F32)],
        compiler_params=_compiler_params(("parallel",)),
        name="ssm_chunks",
    )(u_g, k2, wst, wout, a64)


def _ssm_operators(A_re, A_im, log_dt, B_re, B_im, C_re, C_im):
    hi = lax.Precision.HIGHEST
    G, N, P, Tc = SSM_GROUPS, SSM_N, SSM_P, SSM_CHUNK
    Br, Bi, Cr, Ci = B_re.astype(F32), B_im.astype(F32), C_re.astype(F32), C_im.astype(F32)
    steps = jnp.arange(Tc + 1, dtype=F32)

    per_dir = []
    for dirn in range(2):
        dt = jnp.exp(log_dt[dirn].astype(F32))[:, None]
        ar, ai = A_re[dirn].astype(F32), A_im[dirn].astype(F32)
        mag = jnp.exp(dt * ar)
        er, ei = mag * jnp.cos(dt * ai), mag * jnp.sin(dt * ai)
        den = ar * ar + ai * ai
        fr = ((er - 1.0) * ar + ei * ai) / den
        fi = (ei * ar - (er - 1.0) * ai) / den
        bbr = fr[..., None] * Br - fi[..., None] * Bi
        bbi = fr[..., None] * Bi + fi[..., None] * Br
        pm = jnp.exp(steps[None, :, None] * (dt * ar)[:, None, :])
        ang = steps[None, :, None] * (dt * ai)[:, None, :]
        pr, pi = pm * jnp.cos(ang), pm * jnp.sin(ang)
        car = Cr[:, None] * pr[:, :, None, :] - Ci[:, None] * pi[:, :, None, :]
        cai = Cr[:, None] * pi[:, :, None, :] + Ci[:, None] * pr[:, :, None, :]
        kern = (jnp.einsum('gkpn,gnq->gkpq', car[:, :Tc], bbr, precision=hi)
                - jnp.einsum('gkpn,gnq->gkpq', cai[:, :Tc], bbi, precision=hi))
        bbr_t, bbi_t = jnp.swapaxes(bbr, 1, 2)[:, None], jnp.swapaxes(bbi, 1, 2)[:, None]
        prk, pik = pr[:, :, None, :], pi[:, :, None, :]
        abr = prk * bbr_t - pik * bbi_t
        abi = prk * bbi_t + pik * bbr_t
        cr_t, ci_t = jnp.swapaxes(Cr, 1, 2)[:, :, None, :], jnp.swapaxes(Ci, 1, 2)[:, :, None, :]
        prn, pin = jnp.swapaxes(pr, 1, 2)[..., None], jnp.swapaxes(pi, 1, 2)[..., None]
        per_dir.append(dict(kern=kern, abr=abr, abi=abi, pr=pr, pi=pi,
                            car=cr_t * prn - ci_t * pin, cai=cr_t * pin + ci_t * prn))

    f, b = per_dir
    k2 = jnp.concatenate([b['kern'][:, :0:-1], f['kern'][:, :1] + b['kern'][:, :1], f['kern'][:, 1:]], axis=1)
    k2 = jnp.transpose(k2, (0, 3, 1, 2)).astype(BF16).reshape(G, P, (2 * Tc - 1) * P)
    k2 = jnp.stack([k2[:, :, j * P:j * P + SSM_K2_WIDTH] for j in range(SSM_SHIFTS)], axis=1)

    pad = jnp.zeros((G, Tc * P, LANES - N), BF16)

    def st_cols(xr):
        return jnp.concatenate([xr.astype(BF16).reshape(G, Tc * P, N), pad], axis=-1)

    wst = jnp.concatenate([st_cols(f['abr'][:, Tc - 1::-1][:, :Tc]), st_cols(f['abi'][:, Tc - 1::-1][:, :Tc]),
                           st_cols(b['abr'][:, :Tc]), st_cols(b['abi'][:, :Tc])], axis=-1)

    padr = jnp.zeros((G, LANES - N, Tc * P), BF16)

    def out_rows(x):
        return jnp.concatenate([x.astype(BF16).reshape(G, N, Tc * P), padr], axis=1)

    wout = jnp.concatenate([out_rows(f['car'][:, :, 1:Tc + 1]), out_rows(-f['cai'][:, :, 1:Tc + 1]),
                            out_rows(b['car'][:, :, Tc:0:-1]), out_rows(-b['cai'][:, :, Tc:0:-1])], axis=1)

    padc = jnp.zeros((G, LANES - N), F32)
    a64 = jnp.concatenate([f['pr'][:, Tc], padc, f['pi'][:, Tc], padc,
                           b['pr'][:, Tc], padc, b['pi'][:, Tc], padc], axis=-1)[:, None, :]
    return k2, wst, wout, a64


def _merge_kernel(x_ref, oa_ref, ob_ref, ys_ref, zc_ref, n1_ref, wg_ref, wpa_ref, wpb_ref, wpc_ref,
                  wglu_ref, bglu_ref, dskip_ref, wout_ref, n2_ref, wr_hi_ref, wr_lo_ref, br_ref,
                  x1_ref, xn2_ref, comb_ref):
    x = x_ref[...]
    xn = _rms(x, n1_ref[...]).astype(BF16)
    def gate(branch):
        w = wg_ref[:, branch * D_MODEL:(branch + 1) * D_MODEL]
        return jax.nn.sigmoid(jnp.dot(xn, w, preferred_element_type=F32))

    y = ys_ref[...].astype(F32) + dskip_ref[...] * zc_ref[...]
    g = jax.nn.gelu(y)
    glu = jnp.dot(g.astype(BF16), wglu_ref[...], preferred_element_type=F32) + bglu_ref[...]
    out_c = g * jax.nn.sigmoid(glu)

    merged = gate(0) * jnp.dot(oa_ref[...], wpa_ref[...], preferred_element_type=F32)
    merged += gate(1) * jnp.dot(ob_ref[...], wpb_ref[...], preferred_element_type=F32)
    merged += gate(2) * jnp.dot(out_c.astype(BF16), wpc_ref[...], preferred_element_type=F32)
    x1 = x + jnp.dot(merged.astype(BF16), wout_ref[...], preferred_element_type=F32)
    x1_ref[...] = x1

    xn2 = _rms(x1, n2_ref[...])
    xn2_ref[...] = xn2.astype(BF16)

    lane = lax.broadcasted_iota(jnp.int32, (x.shape[0], LANES), 1).astype(F32)
    neg = jnp.float32(-jnp.inf)
    big = jnp.float32(LANES)
    x_hi = xn2.astype(BF16)
    x_lo = (xn2 - x_hi.astype(F32)).astype(BF16)
    le = (jnp.dot(x_hi, wr_hi_ref[...], preferred_element_type=F32)
          + (jnp.dot(x_hi, wr_lo_ref[...], preferred_element_type=F32)
             + jnp.dot(x_lo, wr_hi_ref[...], preferred_element_type=F32))) + br_ref[...]
    lg = jnp.where((lane >= N_EXPERTS) & (lane < N_EXPERTS + MOE_GROUPS), le, neg)
    g_max = jnp.max(lg, axis=-1, keepdims=True)
    g_idx = jnp.min(jnp.where(lg == g_max, lane, big), axis=-1, keepdims=True) - float(N_EXPERTS)
    g_w = 1.0 / jnp.sum(jnp.exp(lg - g_max), axis=-1, keepdims=True)
    in_group = (lane >= g_idx * EXPERTS_PER_GROUP) & (lane < (g_idx + 1) * EXPERTS_PER_GROUP)
    v = jnp.where(in_group, le, neg)
    top1 = jnp.max(v, axis=-1, keepdims=True)
    i1 = jnp.min(jnp.where(v == top1, lane, big), axis=-1, keepdims=True)
    v2 = jnp.where(lane == i1, neg, v)
    top2 = jnp.max(v2, axis=-1, keepdims=True)
    i2 = jnp.min(jnp.where(v2 == top2, lane, big), axis=-1, keepdims=True)
    e2 = jnp.exp(top2 - top1)
    inv = 1.0 / (1.0 + e2)
    comb_ref[...] = (jnp.where(lane == i1, inv * g_w, 0.0) + jnp.where(lane == i2, e2 * inv * g_w, 0.0))


def _merge(x2d, oa, ob, ys, zc, weights):
    T = x2d.shape[0]
    tm = MERGE_TILE

    def rows(width):
        return pl.BlockSpec((tm, width), lambda i: (i, 0))

    w_specs = [pl.BlockSpec(w.shape, lambda i, nd=w.ndim: (0,) * nd, pipeline_mode=pl.Buffered(1)) for w in weights]
    return pl.pallas_call(
        _merge_kernel,
        grid=(T // tm,),
        in_specs=[rows(D_MODEL), rows(A_WIDTH), rows(B_WIDTH), rows(SSM_WIDTH), rows(SSM_WIDTH)] + w_specs,
        out_specs=[rows(D_MODEL), rows(D_MODEL), rows(LANES)],
        out_shape=[jax.ShapeDtypeStruct((T, D_MODEL), F32), jax.ShapeDtypeStruct((T, D_MODEL), BF16),
                   jax.ShapeDtypeStruct((T, LANES), F32)],
        compiler_params=_compiler_params(("parallel",)),
        name="merge_router",
    )(x2d, oa, ob, ys, zc, *weights)


def _moe_kernel(xn_ref, comb_ref, x1_ref, wg_ref, wu_ref, wd_ref, fw_ref, o_ref, acc_ref, *, final_norm):
    step = pl.program_id(1)

    @pl.when(step == 0)
    def _():
        acc_ref[...] = jnp.zeros(acc_ref.shape, F32)

    xn = xn_ref[...]
    comb = comb_ref[...]
    lane = lax.broadcasted_iota(jnp.int32, comb.shape, 1)
    hidden = []
    for j in range(EXPERTS_PER_STEP):
        h = jax.nn.silu(jnp.dot(xn, wg_ref[j], preferred_element_type=F32)) * jnp.dot(
            xn, wu_ref[j], preferred_element_type=F32)
        c = jnp.sum(jnp.where(lane == step * EXPERTS_PER_STEP + j, comb, 0.0), axis=-1, keepdims=True)
        hidden.append((c * h).astype(BF16))
    wd = wd_ref[...].reshape(EXPERTS_PER_STEP * D_FF_EXPERT, D_MODEL)
    acc_ref[...] += jnp.dot(jnp.concatenate(hidden, axis=1), wd, preferred_element_type=F32)

    @pl.when(step == N_EXPERTS // EXPERTS_PER_STEP - 1)
    def _():
        out = x1_ref[...] + acc_ref[...]
        if final_norm:
            out = _rms(out, fw_ref[...])
        o_ref[...] = out


def _moe(xn2, comb, x1, wg, wu, wd, fw, layer, final_norm):
    T = x1.shape[0]
    tm = EXPERT_TILE
    kernel = functools.partial(_moe_kernel, final_norm=final_norm)
    return pl.pallas_call(
        kernel,
        grid=(T // tm, N_EXPERTS // EXPERTS_PER_STEP),
        in_specs=[
            pl.BlockSpec((tm, D_MODEL), lambda i, e: (i, 0)),
            pl.BlockSpec((tm, LANES), lambda i, e: (i, 0)),
            pl.BlockSpec((tm, D_MODEL), lambda i, e: (i, 0)),
            pl.BlockSpec((None, EXPERTS_PER_STEP, D_MODEL, D_FF_EXPERT), lambda i, e: (layer, e, 0, 0)),
            pl.BlockSpec((None, EXPERTS_PER_STEP, D_MODEL, D_FF_EXPERT), lambda i, e: (layer, e, 0, 0)),
            pl.BlockSpec((None, EXPERTS_PER_STEP, D_FF_EXPERT, D_MODEL), lambda i, e: (layer, e, 0, 0)),
            _const_spec((1, D_MODEL)),
        ],
        out_specs=pl.BlockSpec((tm, D_MODEL), lambda i, e: (i, 0)),
        out_shape=jax.ShapeDtypeStruct((T, D_MODEL), F32),
        scratch_shapes=[pltpu.VMEM((tm, D_MODEL), F32)],
        compiler_params=_compiler_params(("parallel", "arbitrary")),
        name="experts",
    )(xn2, comb, x1, wg, wu, wd, fw)


def _pad_heads(w, n_heads, width):
    w = w.reshape(D_MODEL, n_heads, -1)
    return jnp.pad(w, ((0, 0), (0, 0), (0, width - w.shape[-1]))).reshape(D_MODEL, n_heads * width)


def _pad_vec(v, width=LANES):
    return jnp.pad(v.astype(F32), (0, width - v.shape[0]))[None, :]


def _layer_params(l, p):
    scale = HEAD_DIM ** -0.5
    w_in = p['w_in'][l]
    wa, wb, wc, wg = jnp.split(w_in, [COLS_A, COLS_A + COLS_B, COLS_A + COLS_B + COLS_C], axis=-1)

    wqa, wka, wva = jnp.split(wa, [A_QK_COLS, 2 * A_QK_COLS], axis=-1)
    wva = _pad_heads(wva, A_HEADS, 2 * LANES)
    w_a = jnp.concatenate([wqa * scale, wka, wva], axis=-1).astype(BF16)
    ones_a = jnp.zeros((A_HEADS, 2 * LANES), F32).at[:, A_VDIM].set(1.0).reshape(1, -1)
    e_a = jnp.concatenate([jnp.zeros((1, 2 * A_QK_COLS), F32), ones_a], axis=-1)

    wqb, wkb, wvb = jnp.split(wb, [B_HEADS * HEAD_DIM, (B_HEADS + B_KV_HEADS) * HEAD_DIM], axis=-1)
    w_b = jnp.concatenate([_pad_heads(wqb, B_HEADS, LANES), _pad_heads(wkb, B_KV_HEADS, LANES),
                           _pad_heads(wvb, B_KV_HEADS, LANES)], axis=-1).astype(BF16)
    e_b = jnp.zeros((B_KV_HEADS, LANES), F32).at[:, HEAD_DIM].set(1.0).reshape(1, -1)
    qw = p['q_norm_w'][l].astype(F32) * scale
    kw = p['k_norm_w'][l].astype(F32)
    shift_b = (HEAD_DIM * SHIFT_MARGIN) * jnp.max(jnp.abs(qw)) * jnp.max(jnp.abs(kw))
    e_q = jnp.zeros((1, LANES), F32).at[0, HEAD_DIM].set(1.0)
    e_k = jnp.zeros((1, LANES), F32).at[0, HEAD_DIM].set(-shift_b)

    w_r = jnp.pad(jnp.concatenate([p['w_router_expert'][l], p['w_router_group'][l]], axis=-1).astype(F32),
                  ((0, 0), (0, LANES - N_EXPERTS - MOE_GROUPS)))
    w_r_hi = w_r.astype(BF16)
    merge_w = [
        p['norm1_w'][l][None, :], wg.astype(BF16), p['w_proj_a'][l].astype(BF16),
        p['w_proj_b'][l].astype(BF16),
        p['w_proj_c'][l].astype(BF16), p['w_glu'][l].astype(BF16), p['b_glu'][l][None, :], p['ssm_D'][l][None, :],
        p['w_out'][l].astype(BF16), p['norm2_w'][l][None, :],
        w_r_hi, (w_r - w_r_hi.astype(F32)).astype(BF16),
        _pad_vec(jnp.concatenate([p['b_router_expert'][l], p['b_router_group'][l]])),
    ]
    lam_init = 0.8 - 0.6 * math.exp(-0.3 * l)
    lamv = jnp.stack([_pad_vec(v[l])[0] for v in (p['lam_q1'], p['lam_k1'], p['lam_q2'], p['lam_k2'])])
    return dict(
        norm1=p['norm1_w'][l][None, :], w_a=w_a, e_a=e_a, w_b=w_b, e_b=e_b, w_c=wc.astype(BF16),
        qw=_pad_vec(qw), kw=_pad_vec(kw), e_q=e_q, e_k=e_k, shift_b=shift_b.reshape(1),
        lamv=lamv, lam_init=lam_init, subw=p['diff_subln_w'][l][None, :],
        ssm=_ssm_operators(p['ssm_A_re'][l], p['ssm_A_im'][l], p['ssm_log_dt'][l], p['ssm_B_re'][l],
                           p['ssm_B_im'][l], p['ssm_C_re'][l], p['ssm_C_im'][l]),
        merge_w=merge_w,
        wg=p['w_exp_gate'].astype(BF16), wu=p['w_exp_up'].astype(BF16), wd=p['w_exp_down'].astype(BF16),
    )


def _rope_tables(L):
    half = HEAD_DIM // 2
    inv = ROPE_BASE ** (-jnp.arange(0, half, 2, dtype=F32) / half)
    t = jnp.arange(L)
    row = (t // GRID_W).astype(F32)
    col = (t % GRID_W).astype(F32)
    ang = jnp.concatenate([row[:, None] * inv[None, :]] * 2 + [col[:, None] * inv[None, :]] * 2, axis=-1)
    cos, sin = jnp.cos(ang), jnp.sin(ang)
    first = (jnp.arange(HEAD_DIM) % half) < (half // 2)
    pad = ((0, 0), (0, LANES - HEAD_DIM))
    return (jnp.pad(cos, pad), jnp.pad(jnp.where(first, -sin, 0.0), pad), jnp.pad(jnp.where(first, 0.0, sin), pad))


def _alibi_slopes():
    s = 2.0 ** (-8.0 * jnp.arange(1, A_HEADS + 1, dtype=F32) / A_HEADS)
    return jnp.broadcast_to(s[:, None, None], (A_HEADS, 1, LANES))


def _trunk(x, layers, final_norm_w):
    Bsz, L, _ = x.shape
    T = Bsz * L
    tm = ROW_TILE
    n_chunks = L // SSM_CHUNK
    pos_blocks = L // tm
    cos, sa, sb = _rope_tables(L)
    slopes = _alibi_slopes()
    fw = final_norm_w[None, :]
    x2d = x.reshape(T, D_MODEL)

    for l, lp in enumerate(layers):
        qkv_a, qkv_b, zc32, zc16 = _project(x2d, lp, (cos, sa, sb), pos_blocks)
        out_a = _diff_attention(qkv_a.reshape(Bsz, L, -1), slopes, lp['lamv'], lp['subw'], lp['lam_init'], Bsz, L)
        out_b = _gqa_attention(qkv_b.reshape(Bsz, L, -1), lp['shift_b'], Bsz, L)

        u_g = jnp.transpose(zc16.reshape(Bsz, n_chunks, SSM_CHUNK, SSM_GROUPS, SSM_P), (3, 0, 1, 2, 4))
        u_g = u_g.reshape(SSM_GROUPS, Bsz * n_chunks, SSM_FLAT)
        y_g = _ssm_apply(u_g, *lp['ssm'], Bsz, n_chunks)
        y_s = jnp.transpose(y_g.reshape(SSM_GROUPS, Bsz, n_chunks, SSM_CHUNK, SSM_P), (1, 2, 3, 0, 4))
        y_s = y_s.reshape(T, SSM_WIDTH)

        x1, xn2, comb = _merge(x2d, out_a.reshape(T, A_WIDTH), out_b.reshape(T, B_WIDTH), y_s, zc32,
                               lp['merge_w'])
        x2d = _moe(xn2, comb, x1, lp['wg'], lp['wu'], lp['wd'], fw, layer=l, final_norm=(l == len(layers) - 1))
    return x2d.reshape(Bsz, L, D_MODEL)


def kernel(x_prompt, x_sample, norm1_w, w_in, lam_q1, lam_k1, lam_q2, lam_k2, diff_subln_w, q_norm_w, k_norm_w,
           ssm_A_re, ssm_A_im, ssm_log_dt, ssm_B_re, ssm_B_im, ssm_C_re, ssm_C_im, ssm_D, w_glu, b_glu,
           w_proj_a, w_proj_b, w_proj_c, w_out, norm2_w, w_router_group, b_router_group, w_router_expert,
           b_router_expert, w_exp_gate, w_exp_up, w_exp_down, final_norm_w):
    p = dict(norm1_w=norm1_w, w_in=w_in, lam_q1=lam_q1, lam_k1=lam_k1, lam_q2=lam_q2, lam_k2=lam_k2,
             diff_subln_w=diff_subln_w, q_norm_w=q_norm_w, k_norm_w=k_norm_w, ssm_A_re=ssm_A_re, ssm_A_im=ssm_A_im,
             ssm_log_dt=ssm_log_dt, ssm_B_re=ssm_B_re, ssm_B_im=ssm_B_im, ssm_C_re=ssm_C_re, ssm_C_im=ssm_C_im,
             ssm_D=ssm_D, w_glu=w_glu, b_glu=b_glu, w_proj_a=w_proj_a, w_proj_b=w_proj_b, w_proj_c=w_proj_c,
             w_out=w_out, norm2_w=norm2_w, w_router_group=w_router_group, b_router_group=b_router_group,
             w_router_expert=w_router_expert, b_router_expert=b_router_expert, w_exp_gate=w_exp_gate,
             w_exp_up=w_exp_up, w_exp_down=w_exp_down)
    layers = [_layer_params(l, p) for l in range(DEPTH)]
    return (_trunk(x_prompt, layers, final_norm_w), _trunk(x_sample, layers, final_norm_w))
```

```python
import functools
import math

import jax
import jax.numpy as jnp
from jax import lax
from jax.experimental import pallas as pl
from jax.experimental.pallas import tpu as pltpu

F32 = jnp.float32
BF16 = jnp.bfloat16

D_MODEL = 1024
DEPTH = 2
HEAD_DIM = 64
EPS = 1e-6
A_HEADS = 4
A_VDIM = 2 * HEAD_DIM
A_QK_COLS = A_HEADS * 2 * HEAD_DIM
A_WIDTH = A_HEADS * A_VDIM
COLS_A = 2 * A_QK_COLS + A_WIDTH
B_HEADS = 8
B_KV_HEADS = 2
B_GROUP = B_HEADS // B_KV_HEADS
B_WIDTH = B_HEADS * HEAD_DIM
COLS_B = B_HEADS * HEAD_DIM + 2 * B_KV_HEADS * HEAD_DIM
ROPE_BASE = 10000.0
GRID_W = 64
SSM_WIDTH = 512
SSM_P = 16
SSM_GROUPS = SSM_WIDTH // SSM_P
SSM_N = 64
COLS_C = SSM_WIDTH
MOE_GROUPS = 4
EXPERTS_PER_GROUP = 4
N_EXPERTS = MOE_GROUPS * EXPERTS_PER_GROUP
D_FF_EXPERT = 512

LANES = 128
SUBLANES = 8
VMEM_LIMIT_BYTES = 60 * 1024 * 1024

SSM_CHUNK = 64
SSM_FLAT = SSM_CHUNK * SSM_P
SSM_STATE_COLS = 4 * LANES
SSM_SHIFTS = LANES // SSM_P
SSM_K2_WIDTH = (SSM_CHUNK - 1) // SSM_SHIFTS * LANES + SSM_FLAT

ROW_TILE = 512
MERGE_TILE = 512
EXPERT_TILE = 1024
EXPERTS_PER_STEP = 4
ATT_ROWS = 1024
ATT_KV_TILE = 4096
MAX_STATIC_SHIFT = 40.0
SHIFT_MARGIN = 1.02


def _compiler_params(semantics):
    return pltpu.CompilerParams(dimension_semantics=semantics, vmem_limit_bytes=VMEM_LIMIT_BYTES)


def _const_spec(shape):
    zeros = (0,) * len(shape)
    return pl.BlockSpec(shape, lambda *_: zeros)


def _rms(x, w):
    ms = jnp.mean(x * x, axis=-1, keepdims=True)
    return x * lax.rsqrt(ms + EPS) * w


def _proj_kernel(x_ref, nw_ref, wa_ref, wb_ref, wc_ref, ea_ref, qw_ref, kw_ref, eq_ref, ek_ref, ev_ref,
                 cos_ref, sa_ref, sb_ref, oa_ref, ob_ref, oc32_ref, oc16_ref):
    xn = _rms(x_ref[...], nw_ref[...]).astype(BF16)

    zb = jnp.dot(xn, wb_ref[...], preferred_element_type=F32)
    cos, sa, sb = cos_ref[...], sa_ref[...], sb_ref[...]

    def norm_rope(zh, w):
        ms = jnp.sum(zh * zh, axis=-1, keepdims=True) * (1.0 / HEAD_DIM)
        y = zh * lax.rsqrt(ms + EPS) * w
        return y * cos + pltpu.roll(y, LANES - 16, 1) * sa + pltpu.roll(y, 16, 1) * sb

    for h in range(B_HEADS + B_KV_HEADS):
        sl = slice(h * LANES, (h + 1) * LANES)
        w, e = (qw_ref, eq_ref) if h < B_HEADS else (kw_ref, ek_ref)
        ob_ref[:, sl] = (norm_rope(zb[:, sl], w[...]) + e[...]).astype(ob_ref.dtype)
    sl = slice((B_HEADS + B_KV_HEADS) * LANES, (B_HEADS + 2 * B_KV_HEADS) * LANES)
    ob_ref[:, sl] = (zb[:, sl] + ev_ref[...]).astype(ob_ref.dtype)

    oa_ref[...] = (jnp.dot(xn, wa_ref[...], preferred_element_type=F32) + ea_ref[...]).astype(oa_ref.dtype)

    zc = jnp.dot(xn, wc_ref[...], preferred_element_type=F32)
    oc32_ref[...] = zc
    oc16_ref[...] = zc.astype(BF16)


def _project(x2d, lp, tables, pos_blocks):
    T = x2d.shape[0]
    tm = ROW_TILE
    na, nb = lp['w_a'].shape[1], lp['w_b'].shape[1]

    def rows(width):
        return pl.BlockSpec((tm, width), lambda i: (i, 0))

    def whole(arr):
        return pl.BlockSpec(arr.shape, lambda i: (0, 0), pipeline_mode=pl.Buffered(1))

    consts = [lp['norm1'], lp['w_a'], lp['w_b'], lp['w_c'], lp['e_a'], lp['qw'], lp['kw'], lp['e_q'], lp['e_k'],
              lp['e_b']]
    table_spec = pl.BlockSpec((tm, LANES), lambda i: (i % pos_blocks, 0))
    return pl.pallas_call(
        _proj_kernel,
        grid=(T // tm,),
        in_specs=[rows(D_MODEL)] + [whole(c) for c in consts] + [table_spec] * 3,
        out_specs=[rows(na), rows(nb), rows(COLS_C), rows(COLS_C)],
        out_shape=[jax.ShapeDtypeStruct((T, na), BF16), jax.ShapeDtypeStruct((T, nb), BF16),
                   jax.ShapeDtypeStruct((T, COLS_C), F32), jax.ShapeDtypeStruct((T, COLS_C), BF16)],
        compiler_params=_compiler_params(("parallel",)),
        name="in_proj",
    )(x2d, *consts, *tables)


_NT = (((1,), (1,)), ((), ()))


def _online_softmax(qs, k_fn, v_ref, m_ref, acc_ref, tk, n_chunks, bias_fn):
    m_ref[...] = jnp.full(m_ref.shape, -jnp.inf, F32)
    acc_ref[...] = jnp.zeros(acc_ref.shape, F32)

    def body(c, carry):
        start = pl.multiple_of(c * tk, tk)
        s = lax.dot_general(qs, k_fn(start), _NT, preferred_element_type=F32)
        if bias_fn is not None:
            s = s + bias_fn(start)
        m_prev = m_ref[...]
        m_new = jnp.maximum(m_prev, jnp.max(s, axis=-1, keepdims=True))
        p = jnp.exp(s - m_new)
        alpha = jnp.exp(m_prev - m_new)
        pv = jnp.dot(p.astype(BF16), v_ref[pl.ds(start, tk), :], preferred_element_type=F32)
        acc_ref[...] = alpha * acc_ref[...] + pv
        m_ref[...] = m_new
        return carry

    lax.fori_loop(0, n_chunks, body, 0)


def _shifted_softmax(qs, k_fn, v_ref, acc_ref, tk, lo, hi, bias_fn=None):
    def body(c, carry):
        start = pl.multiple_of(c * tk, tk)
        s = lax.dot_general(qs, k_fn(start), _NT, preferred_element_type=F32)
        if bias_fn is not None:
            s = s + bias_fn(start)
        acc_ref[...] += jnp.dot(jnp.exp(s).astype(BF16), v_ref[pl.ds(start, tk), :], preferred_element_type=F32)
        return carry

    lax.fori_loop(lo, hi, body, 0)


def _half_norms(x, lane):
    sq = x.astype(F32)
    sq = sq * sq
    n1 = jnp.sum(jnp.where(lane < HEAD_DIM, sq, 0.0), axis=-1, keepdims=True)
    n2 = jnp.sum(jnp.where(lane >= HEAD_DIM, sq, 0.0), axis=-1, keepdims=True)
    return jnp.sqrt(n1), jnp.sqrt(n2)


def _diff_attn_kernel(q_ref, k_ref, v_ref, slope_ref, lam_ref, subw_ref, o_ref, kmax_ref, m_ref, acc_ref,
                      *, tq, tk, lam_init):
    i = pl.program_id(2)
    n_chunks = k_ref.shape[0] // tk
    slope = slope_ref[...][:, :1]

    @pl.when(i == 0)
    def _():
        lane_k = lax.broadcasted_iota(jnp.int32, (tk, LANES), 1)

        def body(c, carry):
            n1, n2 = _half_norms(k_ref[pl.ds(pl.multiple_of(c * tk, tk), tk), :], lane_k)
            return (jnp.maximum(carry[0], jnp.max(n1, axis=0, keepdims=True)),
                    jnp.maximum(carry[1], jnp.max(n2, axis=0, keepdims=True)))

        zero11 = jnp.zeros((1, 1), F32)
        k1, k2 = lax.fori_loop(0, n_chunks, body, (zero11, zero11))
        kmax_ref[0:1, :] = jnp.broadcast_to(k1, (1, LANES))
        kmax_ref[1:2, :] = jnp.broadcast_to(k2, (1, LANES))

    q = q_ref[...]
    lane = lax.broadcasted_iota(jnp.int32, q.shape, 1)
    zero = jnp.zeros_like(q)
    qz = jnp.concatenate([jnp.where(lane < HEAD_DIM, q, zero), jnp.where(lane >= HEAD_DIM, q, zero)], axis=0)

    qn1, qn2 = _half_norms(q, lane)
    shift = jnp.concatenate([qn1 * kmax_ref[0:1, 0:1], qn2 * kmax_ref[1:2, 0:1]], axis=0) * SHIFT_MARGIN
    static_ok = jnp.max(shift) <= MAX_STATIC_SHIFT

    row = lax.broadcasted_iota(jnp.int32, (2 * tq, 1), 0)
    q_slope_pos = (i * tq + jnp.where(row >= tq, row - tq, row)).astype(F32) * slope

    def k_fn(start):
        return k_ref[pl.ds(start, tk), :]

    def alibi(start):
        k_slope_pos = (start + lax.broadcasted_iota(jnp.int32, (1, tk), 1)).astype(F32) * slope
        return jnp.abs(q_slope_pos - k_slope_pos)

    @pl.when(static_ok)
    def _():
        acc_ref[...] = jnp.zeros(acc_ref.shape, F32)
        _shifted_softmax(qz, k_fn, v_ref, acc_ref, tk, 0, n_chunks, lambda start: -shift - alibi(start))

    @pl.when(jnp.logical_not(static_ok))
    def _():
        _online_softmax(qz, k_fn, v_ref, m_ref, acc_ref, tk, n_chunks, lambda start: -alibi(start))

    acc = acc_ref[...]
    o = acc[:, :A_VDIM] / acc[:, A_VDIM:A_VDIM + 1]
    lv = lam_ref[...]
    lam = (jnp.exp(jnp.sum(lv[0:1] * lv[1:2], axis=-1, keepdims=True))
           - jnp.exp(jnp.sum(lv[2:3] * lv[3:4], axis=-1, keepdims=True)) + lam_init)
    d = o[:tq] - lam * o[tq:]
    o_ref[...] = (_rms(d, subw_ref[...]) * (1.0 - lam_init)).astype(o_ref.dtype)


def _gqa_attn_kernel(shift_ref, q_ref, k_ref, v_ref, o_ref, m_ref, acc_ref, *, tq, tk):
    qs = jnp.concatenate([q_ref[:, h * LANES:(h + 1) * LANES] for h in range(B_GROUP)], axis=0)
    n_chunks = k_ref.shape[0] // tk
    static_ok = shift_ref[0] <= MAX_STATIC_SHIFT

    def k_fn(start):
        return k_ref[pl.ds(start, tk), :]

    @pl.when(static_ok)
    def _():
        acc_ref[...] = jnp.zeros(acc_ref.shape, F32)
        _shifted_softmax(qs, k_fn, v_ref, acc_ref, tk, 0, n_chunks)

    @pl.when(jnp.logical_not(static_ok))
    def _():
        _online_softmax(qs, k_fn, v_ref, m_ref, acc_ref, tk, n_chunks, None)

    acc = acc_ref[...]
    o = acc / acc[:, HEAD_DIM:HEAD_DIM + 1]
    lane = lax.broadcasted_iota(jnp.int32, (tq, LANES), 1)
    for pair in range(B_GROUP // 2):
        first = o[2 * pair * tq:(2 * pair + 1) * tq]
        second = o[(2 * pair + 1) * tq:(2 * pair + 2) * tq]
        packed = jnp.where(lane < HEAD_DIM, first, pltpu.roll(second, HEAD_DIM, 1))
        o_ref[:, pair * LANES:(pair + 1) * LANES] = packed.astype(o_ref.dtype)


def _attn_tiles(L, stacked):
    return min(ATT_ROWS // stacked, L), min(ATT_KV_TILE, L)


def _resident(block_shape, index_map):
    return pl.BlockSpec(block_shape, index_map, pipeline_mode=pl.Buffered(1))


def _diff_attention(qkv, slopes, lamv, subw, lam_init, Bsz, L):
    tq, tk = _attn_tiles(L, 2)
    nq = A_HEADS
    kernel = functools.partial(_diff_attn_kernel, tq=tq, tk=tk, lam_init=lam_init)
    return pl.pallas_call(
        kernel,
        grid=(Bsz, A_HEADS, L // tq),
        in_specs=[
            pl.BlockSpec((None, tq, LANES), lambda b, h, i: (b, i, h)),
            _resident((None, L, LANES), lambda b, h, i: (b, 0, nq + h)),
            _resident((None, L, 2 * LANES), lambda b, h, i: (b, 0, nq + h)),
            pl.BlockSpec((None, 1, LANES), lambda b, h, i: (h, 0, 0)),
            _const_spec((4, LANES)),
            _const_spec((1, A_VDIM)),
        ],
        out_specs=pl.BlockSpec((None, tq, LANES), lambda b, h, i: (b, i, h)),
        out_shape=jax.ShapeDtypeStruct((Bsz, L, A_WIDTH), BF16),
        scratch_shapes=[pltpu.VMEM((SUBLANES, LANES), F32),
                        pltpu.VMEM((2 * tq, 1), F32), pltpu.VMEM((2 * tq, 2 * LANES), F32)],
        compiler_params=_compiler_params(("parallel", "parallel", "arbitrary")),
        name="diff_attn",
    )(qkv, qkv, qkv, slopes, lamv, subw)


def _gqa_attention(qkv, shift, Bsz, L):
    tq, tk = _attn_tiles(L, B_GROUP)
    kernel = functools.partial(_gqa_attn_kernel, tq=tq, tk=tk)
    return pl.pallas_call(
        kernel,
        grid=(Bsz, B_KV_HEADS, L // tq),
        in_specs=[
            pl.BlockSpec(memory_space=pltpu.SMEM),
            pl.BlockSpec((None, tq, B_GROUP * LANES), lambda b, g, i: (b, i, g)),
            _resident((None, L, LANES), lambda b, g, i: (b, 0, B_HEADS + g)),
            _resident((None, L, LANES), lambda b, g, i: (b, 0, B_HEADS + B_KV_HEADS + g)),
        ],
        out_specs=pl.BlockSpec((None, tq, B_GROUP * HEAD_DIM), lambda b, g, i: (b, i, g)),
        out_shape=jax.ShapeDtypeStruct((Bsz, L, B_WIDTH), BF16),
        scratch_shapes=[pltpu.VMEM((B_GROUP * tq, 1), F32), pltpu.VMEM((B_GROUP * tq, LANES), F32)],
        compiler_params=_compiler_params(("parallel", "parallel", "arbitrary")),
        name="gqa_attn",
    )(shift, qkv, qkv, qkv)


def _ssm_kernel(u_ref, k2_ref, wst_ref, wout_ref, a_ref, y_ref, m_ref, s_ref, h_ref, *, n_batch, n_chunks):
    for s in range(SSM_CHUNK):
        first_lag = SSM_CHUNK - 1 - s
        j, a = first_lag % SSM_SHIFTS, first_lag // SSM_SHIFTS
        m_ref[s * SSM_P:(s + 1) * SSM_P, :] = k2_ref[j, :, a * LANES:a * LANES + SSM_FLAT]
    u = u_ref[...]
    s_ref[...] = jnp.dot(u, wst_ref[...], preferred_element_type=F32)
    a = a_ref[...]
    afr, afi = a[:, 0:LANES], a[:, LANES:2 * LANES]
    abr, abi = a[:, 2 * LANES:3 * LANES], a[:, 3 * LANES:4 * LANES]
    zero = jnp.zeros((1, LANES), F32)

    def step(cg, carry):
        new = []
        for b in range(n_batch):
            hr, hi, gr, gi = carry[4 * b:4 * b + 4]
            rf = pl.multiple_of(b * n_chunks + cg * SUBLANES, SUBLANES)
            rb = pl.multiple_of(b * n_chunks + n_chunks - SUBLANES - cg * SUBLANES, SUBLANES)
            sf = s_ref[pl.ds(rf, SUBLANES), 0:2 * LANES]
            sb = s_ref[pl.ds(rb, SUBLANES), 2 * LANES:4 * LANES]
            rows_f, rows_b = [], [None] * SUBLANES
            for j in range(SUBLANES):
                rows_f.append(jnp.concatenate([hr, hi], axis=1))
                hr, hi = (afr * hr - afi * hi + sf[j:j + 1, 0:LANES],
                          afr * hi + afi * hr + sf[j:j + 1, LANES:2 * LANES])
                jb = SUBLANES - 1 - j
                rows_b[jb] = jnp.concatenate([gr, gi], axis=1)
                gr, gi = (abr * gr - abi * gi + sb[jb:jb + 1, 0:LANES],
                          abr * gi + abi * gr + sb[jb:jb + 1, LANES:2 * LANES])
            h_ref[pl.ds(rf, SUBLANES), 0:2 * LANES] = jnp.concatenate(rows_f, axis=0)
            h_ref[pl.ds(rb, SUBLANES), 2 * LANES:4 * LANES] = jnp.concatenate(rows_b, axis=0)
            new += [hr, hi, gr, gi]
        return tuple(new)

    lax.fori_loop(0, n_chunks // SUBLANES, step, (zero,) * (4 * n_batch))
    y = (jnp.dot(u, m_ref[...], preferred_element_type=F32)
         + jnp.dot(h_ref[...].astype(BF16), wout_ref[...], preferred_element_type=F32))
    y_ref[...] = y.astype(y_ref.dtype)


def _ssm_apply(u_g, k2, wst, wout, a64, n_batch, n_chunks):
    nc = n_batch * n_chunks
    kernel = functools.partial(_ssm_kernel, n_batch=n_batch, n_chunks=n_chunks)
    return pl.pallas_call(
        kernel,
        grid=(SSM_GROUPS,),
        in_specs=[
            pl.BlockSpec((None, nc, SSM_FLAT), lambda g: (g, 0, 0)),
            pl.BlockSpec((None, SSM_SHIFTS, SSM_P, SSM_K2_WIDTH), lambda g: (g, 0, 0, 0)),
            pl.BlockSpec((None, SSM_FLAT, SSM_STATE_COLS), lambda g: (g, 0, 0)),
            pl.BlockSpec((None, SSM_STATE_COLS, SSM_FLAT), lambda g: (g, 0, 0)),
            pl.BlockSpec((None, 1, SSM_STATE_COLS), lambda g: (g, 0, 0)),
        ],
        out_specs=pl.BlockSpec((None, nc, SSM_FLAT), lambda g: (g, 0, 0)),
        out_shape=jax.ShapeDtypeStruct((SSM_GROUPS, nc, SSM_FLAT), BF16),
        scratch_shapes=[pltpu.VMEM((SSM_FLAT, SSM_FLAT), BF16),
                        pltpu.VMEM((nc, SSM_STATE_COLS), F32), pltpu.VMEM((nc, SSM_STATE_COLS), F32)],
        compiler_params=_compiler_params(("parallel",)),
        name="ssm_chunks",
    )(u_g, k2, wst, wout, a64)


def _ssm_operators(A_re, A_im, log_dt, B_re, B_im, C_re, C_im):
    hi = lax.Precision.HIGHEST
    G, N, P, Tc = SSM_GROUPS, SSM_N, SSM_P, SSM_CHUNK
    Br, Bi, Cr, Ci = B_re.astype(F32), B_im.astype(F32), C_re.astype(F32), C_im.astype(F32)
    steps = jnp.arange(Tc + 1, dtype=F32)

    per_dir = []
    for dirn in range(2):
        dt = jnp.exp(log_dt[dirn].astype(F32))[:, None]
        ar, ai = A_re[dirn].astype(F32), A_im[dirn].astype(F32)
        mag = jnp.exp(dt * ar)
        er, ei = mag * jnp.cos(dt * ai), mag * jnp.sin(dt * ai)
        den = ar * ar + ai * ai
        fr = ((er - 1.0) * ar + ei * ai) / den
        fi = (ei * ar - (er - 1.0) * ai) / den
        bbr = fr[..., None] * Br - fi[..., None] * Bi
        bbi = fr[..., None] * Bi + fi[..., None] * Br
        pm = jnp.exp(steps[None, :, None] * (dt * ar)[:, None, :])
        ang = steps[None, :, None] * (dt * ai)[:, None, :]
        pr, pi = pm * jnp.cos(ang), pm * jnp.sin(ang)
        car = Cr[:, None] * pr[:, :, None, :] - Ci[:, None] * pi[:, :, None, :]
        cai = Cr[:, None] * pi[:, :, None, :] + Ci[:, None] * pr[:, :, None, :]
        kern = (jnp.einsum('gkpn,gnq->gkpq', car[:, :Tc], bbr, precision=hi)
                - jnp.einsum('gkpn,gnq->gkpq', cai[:, :Tc], bbi, precision=hi))
        bbr_t, bbi_t = jnp.swapaxes(bbr, 1, 2)[:, None], jnp.swapaxes(bbi, 1, 2)[:, None]
        prk, pik = pr[:, :, None, :], pi[:, :, None, :]
        abr = prk * bbr_t - pik * bbi_t
        abi = prk * bbi_t + pik * bbr_t
        cr_t, ci_t = jnp.swapaxes(Cr, 1, 2)[:, :, None, :], jnp.swapaxes(Ci, 1, 2)[:, :, None, :]
        prn, pin = jnp.swapaxes(pr, 1, 2)[..., None], jnp.swapaxes(pi, 1, 2)[..., None]
        per_dir.append(dict(kern=kern, abr=abr, abi=abi, pr=pr, pi=pi,
                            car=cr_t * prn - ci_t * pin, cai=cr_t * pin + ci_t * prn))

    f, b = per_dir
    k2 = jnp.concatenate([b['kern'][:, :0:-1], f['kern'][:, :1] + b['kern'][:, :1], f['kern'][:, 1:]], axis=1)
    k2 = jnp.transpose(k2, (0, 3, 1, 2)).astype(BF16).reshape(G, P, (2 * Tc - 1) * P)
    k2 = jnp.stack([k2[:, :, j * P:j * P + SSM_K2_WIDTH] for j in range(SSM_SHIFTS)], axis=1)

    pad = jnp.zeros((G, Tc * P, LANES - N), BF16)

    def st_cols(xr):
        return jnp.concatenate([xr.astype(BF16).reshape(G, Tc * P, N), pad], axis=-1)

    wst = jnp.concatenate([st_cols(f['abr'][:, Tc - 1::-1][:, :Tc]), st_cols(f['abi'][:, Tc - 1::-1][:, :Tc]),
                           st_cols(b['abr'][:, :Tc]), st_cols(b['abi'][:, :Tc])], axis=-1)

    padr = jnp.zeros((G, LANES - N, Tc * P), BF16)

    def out_rows(x):
        return jnp.concatenate([x.astype(BF16).reshape(G, N, Tc * P), padr], axis=1)

    wout = jnp.concatenate([out_rows(f['car'][:, :, 1:Tc + 1]), out_rows(-f['cai'][:, :, 1:Tc + 1]),
                            out_rows(b['car'][:, :, Tc:0:-1]), out_rows(-b['cai'][:, :, Tc:0:-1])], axis=1)

    padc = jnp.zeros((G, LANES - N), F32)
    a64 = jnp.concatenate([f['pr'][:, Tc], padc, f['pi'][:, Tc], padc,
                           b['pr'][:, Tc], padc, b['pi'][:, Tc], padc], axis=-1)[:, None, :]
    return k2, wst, wout, a64


def _merge_kernel(x_ref, oa_ref, ob_ref, ys_ref, zc_ref, n1_ref, wg_ref, wpa_ref, wpb_ref, wpc_ref,
                  wglu_ref, bglu_ref, dskip_ref, wout_ref, n2_ref, wr_hi_ref, wr_lo_ref, br_ref,
                  x1_ref, xn2_ref, comb_ref):
    x = x_ref[...]
    xn = _rms(x, n1_ref[...]).astype(BF16)
    def gate(branch):
        w = wg_ref[:, branch * D_MODEL:(branch + 1) * D_MODEL]
        return jax.nn.sigmoid(jnp.dot(xn, w, preferred_element_type=F32))

    y = ys_ref[...].astype(F32) + dskip_ref[...] * zc_ref[...]
    g = jax.nn.gelu(y)
    glu = jnp.dot(g.astype(BF16), wglu_ref[...], preferred_element_type=F32) + bglu_ref[...]
    out_c = g * jax.nn.sigmoid(glu)

    merged = gate(0) * jnp.dot(oa_ref[...], wpa_ref[...], preferred_element_type=F32)
    merged += gate(1) * jnp.dot(ob_ref[...], wpb_ref[...], preferred_element_type=F32)
    merged += gate(2) * jnp.dot(out_c.astype(BF16), wpc_ref[...], preferred_element_type=F32)
    x1 = x + jnp.dot(merged.astype(BF16), wout_ref[...], preferred_element_type=F32)
    x1_ref[...] = x1

    xn2 = _rms(x1, n2_ref[...])
    xn2_ref[...] = xn2.astype(BF16)

    lane = lax.broadcasted_iota(jnp.int32, (x.shape[0], LANES), 1).astype(F32)
    neg = jnp.float32(-jnp.inf)
    big = jnp.float32(LANES)
    x_hi = xn2.astype(BF16)
    x_lo = (xn2 - x_hi.astype(F32)).astype(BF16)
    le = (jnp.dot(x_hi, wr_hi_ref[...], preferred_element_type=F32)
          + (jnp.dot(x_hi, wr_lo_ref[...], preferred_element_type=F32)
             + jnp.dot(x_lo, wr_hi_ref[...], preferred_element_type=F32))) + br_ref[...]
    lg = jnp.where((lane >= N_EXPERTS) & (lane < N_EXPERTS + MOE_GROUPS), le, neg)
    g_max = jnp.max(lg, axis=-1, keepdims=True)
    g_idx = jnp.min(jnp.where(lg == g_max, lane, big), axis=-1, keepdims=True) - float(N_EXPERTS)
    g_w = 1.0 / jnp.sum(jnp.exp(lg - g_max), axis=-1, keepdims=True)
    in_group = (lane >= g_idx * EXPERTS_PER_GROUP) & (lane < (g_idx + 1) * EXPERTS_PER_GROUP)
    v = jnp.where(in_group, le, neg)
    top1 = jnp.max(v, axis=-1, keepdims=True)
    i1 = jnp.min(jnp.where(v == top1, lane, big), axis=-1, keepdims=True)
    v2 = jnp.where(lane == i1, neg, v)
    top2 = jnp.max(v2, axis=-1, keepdims=True)
    i2 = jnp.min(jnp.where(v2 == top2, lane, big), axis=-1, keepdims=True)
    e2 = jnp.exp(top2 - top1)
    inv = 1.0 / (1.0 + e2)
    comb_ref[...] = (jnp.where(lane == i1, inv * g_w, 0.0) + jnp.where(lane == i2, e2 * inv * g_w, 0.0))


def _merge(x2d, oa, ob, ys, zc, weights):
    T = x2d.shape[0]
    tm = MERGE_TILE

    def rows(width):
        return pl.BlockSpec((tm, width), lambda i: (i, 0))

    w_specs = [pl.BlockSpec(w.shape, lambda i, nd=w.ndim: (0,) * nd, pipeline_mode=pl.Buffered(1)) for w in weights]
    return pl.pallas_call(
        _merge_kernel,
        grid=(T // tm,),
        in_specs=[rows(D_MODEL), rows(A_WIDTH), rows(B_WIDTH), rows(SSM_WIDTH), rows(SSM_WIDTH)] + w_specs,
        out_specs=[rows(D_MODEL), rows(D_MODEL), rows(LANES)],
        out_shape=[jax.ShapeDtypeStruct((T, D_MODEL), F32), jax.ShapeDtypeStruct((T, D_MODEL), BF16),
                   jax.ShapeDtypeStruct((T, LANES), F32)],
        compiler_params=_compiler_params(("parallel",)),
        name="merge_router",
    )(x2d, oa, ob, ys, zc, *weights)


def _moe_kernel(xn_ref, comb_ref, x1_ref, wg_ref, wu_ref, wd_ref, fw_ref, o_ref, acc_ref, *, final_norm):
    step = pl.program_id(1)

    @pl.when(step == 0)
    def _():
        acc_ref[...] = jnp.zeros(acc_ref.shape, F32)

    xn = xn_ref[...]
    comb = comb_ref[...]
    lane = lax.broadcasted_iota(jnp.int32, comb.shape, 1)
    hidden = []
    for j in range(EXPERTS_PER_STEP):
        h = jax.nn.silu(jnp.dot(xn, wg_ref[j], preferred_element_type=F32)) * jnp.dot(
            xn, wu_ref[j], preferred_element_type=F32)
        c = jnp.sum(jnp.where(lane == step * EXPERTS_PER_STEP + j, comb, 0.0), axis=-1, keepdims=True)
        hidden.append((c * h).astype(BF16))
    wd = wd_ref[...].reshape(EXPERTS_PER_STEP * D_FF_EXPERT, D_MODEL)
    acc_ref[...] += jnp.dot(jnp.concatenate(hidden, axis=1), wd, preferred_element_type=F32)

    @pl.when(step == N_EXPERTS // EXPERTS_PER_STEP - 1)
    def _():
        out = x1_ref[...] + acc_ref[...]
        if final_norm:
            out = _rms(out, fw_ref[...])
        o_ref[...] = out


def _moe(xn2, comb, x1, wg, wu, wd, fw, layer, final_norm):
    T = x1.shape[0]
    tm = EXPERT_TILE
    kernel = functools.partial(_moe_kernel, final_norm=final_norm)
    return pl.pallas_call(
        kernel,
        grid=(T // tm, N_EXPERTS // EXPERTS_PER_STEP),
        in_specs=[
            pl.BlockSpec((tm, D_MODEL), lambda i, e: (i, 0)),
            pl.BlockSpec((tm, LANES), lambda i, e: (i, 0)),
            pl.BlockSpec((tm, D_MODEL), lambda i, e: (i, 0)),
            pl.BlockSpec((None, EXPERTS_PER_STEP, D_MODEL, D_FF_EXPERT), lambda i, e: (layer, e, 0, 0)),
            pl.BlockSpec((None, EXPERTS_PER_STEP, D_MODEL, D_FF_EXPERT), lambda i, e: (layer, e, 0, 0)),
            pl.BlockSpec((None, EXPERTS_PER_STEP, D_FF_EXPERT, D_MODEL), lambda i, e: (layer, e, 0, 0)),
            _const_spec((1, D_MODEL)),
        ],
        out_specs=pl.BlockSpec((tm, D_MODEL), lambda i, e: (i, 0)),
        out_shape=jax.ShapeDtypeStruct((T, D_MODEL), F32),
        scratch_shapes=[pltpu.VMEM((tm, D_MODEL), F32)],
        compiler_params=_compiler_params(("parallel", "arbitrary")),
        name="experts",
    )(xn2, comb, x1, wg, wu, wd, fw)


def _pad_heads(w, n_heads, width):
    w = w.reshape(D_MODEL, n_heads, -1)
    return jnp.pad(w, ((0, 0), (0, 0), (0, width - w.shape[-1]))).reshape(D_MODEL, n_heads * width)


def _pad_vec(v, width=LANES):
    return jnp.pad(v.astype(F32), (0, width - v.shape[0]))[None, :]


def _layer_params(l, p):
    scale = HEAD_DIM ** -0.5
    w_in = p['w_in'][l].astype(BF16)
    wa, wb, wc, wg = jnp.split(w_in, [COLS_A, COLS_A + COLS_B, COLS_A + COLS_B + COLS_C], axis=-1)

    wqa, wka, wva = jnp.split(wa, [A_QK_COLS, 2 * A_QK_COLS], axis=-1)
    wva = _pad_heads(wva, A_HEADS, 2 * LANES)
    w_a = jnp.concatenate([wqa * scale, wka, wva], axis=-1)
    ones_a = jnp.zeros((A_HEADS, 2 * LANES), F32).at[:, A_VDIM].set(1.0).reshape(1, -1)
    e_a = jnp.concatenate([jnp.zeros((1, 2 * A_QK_COLS), F32), ones_a], axis=-1)

    wqb, wkb, wvb = jnp.split(wb, [B_HEADS * HEAD_DIM, (B_HEADS + B_KV_HEADS) * HEAD_DIM], axis=-1)
    w_b = jnp.concatenate([_pad_heads(wqb, B_HEADS, LANES), _pad_heads(wkb, B_KV_HEADS, LANES),
                           _pad_heads(wvb, B_KV_HEADS, LANES)], axis=-1)
    e_b = jnp.zeros((B_KV_HEADS, LANES), F32).at[:, HEAD_DIM].set(1.0).reshape(1, -1)
    qw = p['q_norm_w'][l].astype(F32) * scale
    kw = p['k_norm_w'][l].astype(F32)
    shift_b = (HEAD_DIM * SHIFT_MARGIN) * jnp.max(jnp.abs(qw)) * jnp.max(jnp.abs(kw))
    e_q = jnp.zeros((1, LANES), F32).at[0, HEAD_DIM].set(1.0)
    e_k = jnp.zeros((1, LANES), F32).at[0, HEAD_DIM].set(-shift_b)

    w_r = jnp.pad(jnp.concatenate([p['w_router_expert'][l], p['w_router_group'][l]], axis=-1).astype(F32),
                  ((0, 0), (0, LANES - N_EXPERTS - MOE_GROUPS)))
    w_r_hi = w_r.astype(BF16)
    merge_w = [
        p['norm1_w'][l][None, :], wg, p['w_proj_a'][l].astype(BF16),
        p['w_proj_b'][l].astype(BF16),
        p['w_proj_c'][l].astype(BF16), p['w_glu'][l].astype(BF16), p['b_glu'][l][None, :], p['ssm_D'][l][None, :],
        p['w_out'][l].astype(BF16), p['norm2_w'][l][None, :],
        w_r_hi, (w_r - w_r_hi.astype(F32)).astype(BF16),
        _pad_vec(jnp.concatenate([p['b_router_expert'][l], p['b_router_group'][l]])),
    ]
    lam_init = 0.8 - 0.6 * math.exp(-0.3 * l)
    lamv = jnp.stack([_pad_vec(v[l])[0] for v in (p['lam_q1'], p['lam_k1'], p['lam_q2'], p['lam_k2'])])
    return dict(
        norm1=p['norm1_w'][l][None, :], w_a=w_a, e_a=e_a, w_b=w_b, e_b=e_b, w_c=wc,
        qw=_pad_vec(qw), kw=_pad_vec(kw), e_q=e_q, e_k=e_k, shift_b=shift_b.reshape(1),
        lamv=lamv, lam_init=lam_init, subw=p['diff_subln_w'][l][None, :],
        ssm=_ssm_operators(p['ssm_A_re'][l], p['ssm_A_im'][l], p['ssm_log_dt'][l], p['ssm_B_re'][l],
                           p['ssm_B_im'][l], p['ssm_C_re'][l], p['ssm_C_im'][l]),
        merge_w=merge_w,
        wg=p['w_exp_gate'].astype(BF16), wu=p['w_exp_up'].astype(BF16), wd=p['w_exp_down'].astype(BF16),
    )


def _rope_tables(L):
    half = HEAD_DIM // 2
    inv = ROPE_BASE ** (-jnp.arange(0, half, 2, dtype=F32) / half)
    t = jnp.arange(L)
    row = (t // GRID_W).astype(F32)
    col = (t % GRID_W).astype(F32)
    ang = jnp.concatenate([row[:, None] * inv[None, :]] * 2 + [col[:, None] * inv[None, :]] * 2, axis=-1)
    cos, sin = jnp.cos(ang), jnp.sin(ang)
    first = (jnp.arange(HEAD_DIM) % half) < (half // 2)
    pad = ((0, 0), (0, LANES - HEAD_DIM))
    return (jnp.pad(cos, pad), jnp.pad(jnp.where(first, -sin, 0.0), pad), jnp.pad(jnp.where(first, 0.0, sin), pad))


def _alibi_slopes():
    s = 2.0 ** (-8.0 * jnp.arange(1, A_HEADS + 1, dtype=F32) / A_HEADS)
    return jnp.broadcast_to(s[:, None, None], (A_HEADS, 1, LANES))


def _trunk(x, layers, final_norm_w):
    Bsz, L, _ = x.shape
    T = Bsz * L
    tm = ROW_TILE
    n_chunks = L // SSM_CHUNK
    pos_blocks = L // tm
    cos, sa, sb = _rope_tables(L)
    slopes = _alibi_slopes()
    fw = final_norm_w[None, :]
    x2d = x.reshape(T, D_MODEL)

    for l, lp in enumerate(layers):
        qkv_a, qkv_b, zc32, zc16 = _project(x2d, lp, (cos, sa, sb), pos_blocks)
        out_a = _diff_attention(qkv_a.reshape(Bsz, L, -1), slopes, lp['lamv'], lp['subw'], lp['lam_init'], Bsz, L)
        out_b = _gqa_attention(qkv_b.reshape(Bsz, L, -1), lp['shift_b'], Bsz, L)

        u_g = jnp.transpose(zc16.reshape(Bsz, n_chunks, SSM_CHUNK, SSM_GROUPS, SSM_P), (3, 0, 1, 2, 4))
        u_g = u_g.reshape(SSM_GROUPS, Bsz * n_chunks, SSM_FLAT)
        y_g = _ssm_apply(u_g, *lp['ssm'], Bsz, n_chunks)
        y_s = jnp.transpose(y_g.reshape(SSM_GROUPS, Bsz, n_chunks, SSM_CHUNK, SSM_P), (1, 2, 3, 0, 4))
        y_s = y_s.reshape(T, SSM_WIDTH)

        x1, xn2, comb = _merge(x2d, out_a.reshape(T, A_WIDTH), out_b.reshape(T, B_WIDTH), y_s, zc32,
                               lp['merge_w'])
        x2d = _moe(xn2, comb, x1, lp['wg'], lp['wu'], lp['wd'], fw, layer=l, final_norm=(l == len(layers) - 1))
    return x2d.reshape(Bsz, L, D_MODEL)


def kernel(x_prompt, x_sample, norm1_w, w_in, lam_q1, lam_k1, lam_q2, lam_k2, diff_subln_w, q_norm_w, k_norm_w,
           ssm_A_re, ssm_A_im, ssm_log_dt, ssm_B_re, ssm_B_im, ssm_C_re, ssm_C_im, ssm_D, w_glu, b_glu,
           w_proj_a, w_proj_b, w_proj_c, w_out, norm2_w, w_router_group, b_router_group, w_router_expert,
           b_router_expert, w_exp_gate, w_exp_up, w_exp_down, final_norm_w):
    p = dict(norm1_w=norm1_w, w_in=w_in, lam_q1=lam_q1, lam_k1=lam_k1, lam_q2=lam_q2, lam_k2=lam_k2,
             diff_subln_w=diff_subln_w, q_norm_w=q_norm_w, k_norm_w=k_norm_w, ssm_A_re=ssm_A_re, ssm_A_im=ssm_A_im,
             ssm_log_dt=ssm_log_dt, ssm_B_re=ssm_B_re, ssm_B_im=ssm_B_im, ssm_C_re=ssm_C_re, ssm_C_im=ssm_C_im,
             ssm_D=ssm_D, w_glu=w_glu, b_glu=b_glu, w_proj_a=w_proj_a, w_proj_b=w_proj_b, w_proj_c=w_proj_c,
             w_out=w_out, norm2_w=norm2_w, w_router_group=w_router_group, b_router_group=b_router_group,
             w_router_expert=w_router_expert, b_router_expert=b_router_expert, w_exp_gate=w_exp_gate,
             w_exp_up=w_exp_up, w_exp_down=w_exp_down)
    layers = [_layer_params(l, p) for l in range(DEPTH)]
    return (_trunk(x_prompt, layers, final_norm_w), _trunk(x_sample, layers, final_norm_w))
```

```python
import functools
import math

import jax
import jax.numpy as jnp
from jax import lax
from jax.experimental import pallas as pl
from jax.experimental.pallas import tpu as pltpu

F32 = jnp.float32
BF16 = jnp.bfloat16

D_MODEL = 1024
DEPTH = 2
HEAD_DIM = 64
EPS = 1e-6
A_HEADS = 4
A_VDIM = 2 * HEAD_DIM
A_QK_COLS = A_HEADS * 2 * HEAD_DIM
A_WIDTH = A_HEADS * A_VDIM
COLS_A = 2 * A_QK_COLS + A_WIDTH
B_HEADS = 8
B_KV_HEADS = 2
B_GROUP = B_HEADS // B_KV_HEADS
B_WIDTH = B_HEADS * HEAD_DIM
COLS_B = B_HEADS * HEAD_DIM + 2 * B_KV_HEADS * HEAD_DIM
ROPE_BASE = 10000.0
GRID_W = 64
SSM_WIDTH = 512
SSM_P = 16
SSM_GROUPS = SSM_WIDTH // SSM_P
SSM_N = 64
COLS_C = SSM_WIDTH
N_BRANCH = 3
MOE_GROUPS = 4
EXPERTS_PER_GROUP = 4
N_EXPERTS = MOE_GROUPS * EXPERTS_PER_GROUP
D_FF_EXPERT = 512

LANES = 128
SUBLANES = 8
VMEM_LIMIT_BYTES = 60 * 1024 * 1024

SSM_CHUNK = 64
SSM_FLAT = SSM_CHUNK * SSM_P
SSM_STATE_COLS = 4 * LANES
SSM_SHIFTS = LANES // SSM_P
SSM_K2_WIDTH = (SSM_CHUNK - 1) // SSM_SHIFTS * LANES + SSM_FLAT

ROW_TILE = 512
MERGE_TILE = 512
EXPERT_TILE = 1024
EXPERTS_PER_STEP = 4
ATT_ROWS = 1024
ATT_KV_TILE = 4096
MAX_STATIC_SHIFT = 40.0
SHIFT_MARGIN = 1.02


def _compiler_params(semantics):
    return pltpu.CompilerParams(dimension_semantics=semantics, vmem_limit_bytes=VMEM_LIMIT_BYTES)


def _const_spec(shape):
    zeros = (0,) * len(shape)
    return pl.BlockSpec(shape, lambda *_: zeros)


def _rms(x, w):
    ms = jnp.mean(x * x, axis=-1, keepdims=True)
    return x * lax.rsqrt(ms + EPS) * w


def _proj_kernel(x_ref, nw_ref, wa_ref, wb_ref, wc_ref, ea_ref, qw_ref, kw_ref, eq_ref, ek_ref, ev_ref,
                 cos_ref, sa_ref, sb_ref, oa_ref, ob_ref, oc32_ref, oc16_ref):
    xn = _rms(x_ref[...], nw_ref[...]).astype(BF16)

    zb = jnp.dot(xn, wb_ref[...], preferred_element_type=F32)
    cos, sa, sb = cos_ref[...], sa_ref[...], sb_ref[...]

    def norm_rope(zh, w):
        ms = jnp.sum(zh * zh, axis=-1, keepdims=True) * (1.0 / HEAD_DIM)
        y = zh * lax.rsqrt(ms + EPS) * w
        return y * cos + pltpu.roll(y, LANES - 16, 1) * sa + pltpu.roll(y, 16, 1) * sb

    for h in range(B_HEADS + B_KV_HEADS):
        sl = slice(h * LANES, (h + 1) * LANES)
        w, e = (qw_ref, eq_ref) if h < B_HEADS else (kw_ref, ek_ref)
        ob_ref[:, sl] = (norm_rope(zb[:, sl], w[...]) + e[...]).astype(ob_ref.dtype)
    sl = slice((B_HEADS + B_KV_HEADS) * LANES, (B_HEADS + 2 * B_KV_HEADS) * LANES)
    ob_ref[:, sl] = (zb[:, sl] + ev_ref[...]).astype(ob_ref.dtype)

    oa_ref[...] = (jnp.dot(xn, wa_ref[...], preferred_element_type=F32) + ea_ref[...]).astype(oa_ref.dtype)

    zc = jnp.dot(xn, wc_ref[...], preferred_element_type=F32)
    oc32_ref[...] = zc
    oc16_ref[...] = zc.astype(BF16)


def _project(x2d, lp, tables, pos_blocks):
    T = x2d.shape[0]
    tm = ROW_TILE
    na, nb = lp['w_a'].shape[1], lp['w_b'].shape[1]

    def rows(width):
        return pl.BlockSpec((tm, width), lambda i: (i, 0))

    def whole(arr):
        return pl.BlockSpec(arr.shape, lambda i: (0, 0), pipeline_mode=pl.Buffered(1))

    consts = [lp['norm1'], lp['w_a'], lp['w_b'], lp['w_c'], lp['e_a'], lp['qw'], lp['kw'], lp['e_q'], lp['e_k'],
              lp['e_b']]
    table_spec = pl.BlockSpec((tm, LANES), lambda i: (i % pos_blocks, 0))
    return pl.pallas_call(
        _proj_kernel,
        grid=(T // tm,),
        in_specs=[rows(D_MODEL)] + [whole(c) for c in consts] + [table_spec] * 3,
        out_specs=[rows(na), rows(nb), rows(COLS_C), rows(COLS_C)],
        out_shape=[jax.ShapeDtypeStruct((T, na), BF16), jax.ShapeDtypeStruct((T, nb), BF16),
                   jax.ShapeDtypeStruct((T, COLS_C), F32), jax.ShapeDtypeStruct((T, COLS_C), BF16)],
        compiler_params=_compiler_params(("parallel",)),
        name="in_proj",
    )(x2d, *consts, *tables)


_NT = (((1,), (1,)), ((), ()))


def _online_softmax(qs, k_fn, v_ref, m_ref, acc_ref, tk, n_chunks, bias_fn):
    m_ref[...] = jnp.full(m_ref.shape, -jnp.inf, F32)
    acc_ref[...] = jnp.zeros(acc_ref.shape, F32)

    def body(c, carry):
        start = pl.multiple_of(c * tk, tk)
        s = lax.dot_general(qs, k_fn(start), _NT, preferred_element_type=F32)
        if bias_fn is not None:
            s = s + bias_fn(start)
        m_prev = m_ref[...]
        m_new = jnp.maximum(m_prev, jnp.max(s, axis=-1, keepdims=True))
        p = jnp.exp(s - m_new)
        alpha = jnp.exp(m_prev - m_new)
        pv = jnp.dot(p.astype(BF16), v_ref[pl.ds(start, tk), :], preferred_element_type=F32)
        acc_ref[...] = alpha * acc_ref[...] + pv
        m_ref[...] = m_new
        return carry

    lax.fori_loop(0, n_chunks, body, 0)


def _shifted_softmax(qs, k_fn, v_ref, acc_ref, tk, lo, hi, bias_fn=None):
    def body(c, carry):
        start = pl.multiple_of(c * tk, tk)
        s = lax.dot_general(qs, k_fn(start), _NT, preferred_element_type=F32)
        if bias_fn is not None:
            s = s + bias_fn(start)
        acc_ref[...] += jnp.dot(jnp.exp(s).astype(BF16), v_ref[pl.ds(start, tk), :], preferred_element_type=F32)
        return carry

    lax.fori_loop(lo, hi, body, 0)


def _half_norms(x, lane):
    sq = x.astype(F32)
    sq = sq * sq
    n1 = jnp.sum(jnp.where(lane < HEAD_DIM, sq, 0.0), axis=-1, keepdims=True)
    n2 = jnp.sum(jnp.where(lane >= HEAD_DIM, sq, 0.0), axis=-1, keepdims=True)
    return jnp.sqrt(n1), jnp.sqrt(n2)


def _diff_attn_kernel(q_ref, k_ref, v_ref, slope_ref, lam_ref, subw_ref, o_ref, kmax_ref, m_ref, acc_ref,
                      *, tq, tk, lam_init):
    i = pl.program_id(2)
    n_chunks = k_ref.shape[0] // tk
    slope = slope_ref[...][:, :1]

    @pl.when(i == 0)
    def _():
        lane_k = lax.broadcasted_iota(jnp.int32, (tk, LANES), 1)

        def body(c, carry):
            n1, n2 = _half_norms(k_ref[pl.ds(pl.multiple_of(c * tk, tk), tk), :], lane_k)
            return (jnp.maximum(carry[0], jnp.max(n1, axis=0, keepdims=True)),
                    jnp.maximum(carry[1], jnp.max(n2, axis=0, keepdims=True)))

        zero11 = jnp.zeros((1, 1), F32)
        k1, k2 = lax.fori_loop(0, n_chunks, body, (zero11, zero11))
        kmax_ref[0:1, :] = jnp.broadcast_to(k1, (1, LANES))
        kmax_ref[1:2, :] = jnp.broadcast_to(k2, (1, LANES))

    q = q_ref[...]
    lane = lax.broadcasted_iota(jnp.int32, q.shape, 1)
    zero = jnp.zeros_like(q)
    qz = jnp.concatenate([jnp.where(lane < HEAD_DIM, q, zero), jnp.where(lane >= HEAD_DIM, q, zero)], axis=0)

    qn1, qn2 = _half_norms(q, lane)
    shift = jnp.concatenate([qn1 * kmax_ref[0:1, 0:1], qn2 * kmax_ref[1:2, 0:1]], axis=0) * SHIFT_MARGIN
    static_ok = jnp.max(shift) <= MAX_STATIC_SHIFT

    row = lax.broadcasted_iota(jnp.int32, (2 * tq, 1), 0)
    q_slope_pos = (i * tq + jnp.where(row >= tq, row - tq, row)).astype(F32) * slope

    def k_fn(start):
        return k_ref[pl.ds(start, tk), :]

    def alibi(start):
        k_slope_pos = (start + lax.broadcasted_iota(jnp.int32, (1, tk), 1)).astype(F32) * slope
        return jnp.abs(q_slope_pos - k_slope_pos)

    @pl.when(static_ok)
    def _():
        acc_ref[...] = jnp.zeros(acc_ref.shape, F32)
        _shifted_softmax(qz, k_fn, v_ref, acc_ref, tk, 0, n_chunks, lambda start: -shift - alibi(start))

    @pl.when(jnp.logical_not(static_ok))
    def _():
        _online_softmax(qz, k_fn, v_ref, m_ref, acc_ref, tk, n_chunks, lambda start: -alibi(start))

    acc = acc_ref[...]
    o = acc[:, :A_VDIM] / acc[:, A_VDIM:A_VDIM + 1]
    lv = lam_ref[...]
    lam = (jnp.exp(jnp.sum(lv[0:1] * lv[1:2], axis=-1, keepdims=True))
           - jnp.exp(jnp.sum(lv[2:3] * lv[3:4], axis=-1, keepdims=True)) + lam_init)
    d = o[:tq] - lam * o[tq:]
    o_ref[...] = (_rms(d, subw_ref[...]) * (1.0 - lam_init)).astype(o_ref.dtype)


def _gqa_attn_kernel(shift_ref, q_ref, k_ref, v_ref, o_ref, m_ref, acc_ref, *, tq, tk):
    qs = jnp.concatenate([q_ref[:, h * LANES:(h + 1) * LANES] for h in range(B_GROUP)], axis=0)
    n_chunks = k_ref.shape[0] // tk
    static_ok = shift_ref[0] <= MAX_STATIC_SHIFT

    def k_fn(start):
        return k_ref[pl.ds(start, tk), :]

    @pl.when(static_ok)
    def _():
        acc_ref[...] = jnp.zeros(acc_ref.shape, F32)
        _shifted_softmax(qs, k_fn, v_ref, acc_ref, tk, 0, n_chunks)

    @pl.when(jnp.logical_not(static_ok))
    def _():
        _online_softmax(qs, k_fn, v_ref, m_ref, acc_ref, tk, n_chunks, None)

    acc = acc_ref[...]
    o = acc / acc[:, HEAD_DIM:HEAD_DIM + 1]
    lane = lax.broadcasted_iota(jnp.int32, (tq, LANES), 1)
    for pair in range(B_GROUP // 2):
        first = o[2 * pair * tq:(2 * pair + 1) * tq]
        second = o[(2 * pair + 1) * tq:(2 * pair + 2) * tq]
        packed = jnp.where(lane < HEAD_DIM, first, pltpu.roll(second, HEAD_DIM, 1))
        o_ref[:, pair * LANES:(pair + 1) * LANES] = packed.astype(o_ref.dtype)


def _attn_tiles(L, stacked):
    return min(ATT_ROWS // stacked, L), min(ATT_KV_TILE, L)


def _resident(block_shape, index_map):
    return pl.BlockSpec(block_shape, index_map, pipeline_mode=pl.Buffered(1))


def _diff_attention(qkv, slopes, lamv, subw, lam_init, Bsz, L):
    tq, tk = _attn_tiles(L, 2)
    nq = A_HEADS
    kernel = functools.partial(_diff_attn_kernel, tq=tq, tk=tk, lam_init=lam_init)
    return pl.pallas_call(
        kernel,
        grid=(Bsz, A_HEADS, L // tq),
        in_specs=[
            pl.BlockSpec((None, tq, LANES), lambda b, h, i: (b, i, h)),
            _resident((None, L, LANES), lambda b, h, i: (b, 0, nq + h)),
            _resident((None, L, 2 * LANES), lambda b, h, i: (b, 0, nq + h)),
            pl.BlockSpec((None, 1, LANES), lambda b, h, i: (h, 0, 0)),
            _const_spec((4, LANES)),
            _const_spec((1, A_VDIM)),
        ],
        out_specs=pl.BlockSpec((None, tq, LANES), lambda b, h, i: (b, i, h)),
        out_shape=jax.ShapeDtypeStruct((Bsz, L, A_WIDTH), BF16),
        scratch_shapes=[pltpu.VMEM((SUBLANES, LANES), F32),
                        pltpu.VMEM((2 * tq, 1), F32), pltpu.VMEM((2 * tq, 2 * LANES), F32)],
        compiler_params=_compiler_params(("parallel", "parallel", "arbitrary")),
        name="diff_attn",
    )(qkv, qkv, qkv, slopes, lamv, subw)


def _gqa_attention(qkv, shift, Bsz, L):
    tq, tk = _attn_tiles(L, B_GROUP)
    kernel = functools.partial(_gqa_attn_kernel, tq=tq, tk=tk)
    return pl.pallas_call(
        kernel,
        grid=(Bsz, B_KV_HEADS, L // tq),
        in_specs=[
            pl.BlockSpec(memory_space=pltpu.SMEM),
            pl.BlockSpec((None, tq, B_GROUP * LANES), lambda b, g, i: (b, i, g)),
            _resident((None, L, LANES), lambda b, g, i: (b, 0, B_HEADS + g)),
            _resident((None, L, LANES), lambda b, g, i: (b, 0, B_HEADS + B_KV_HEADS + g)),
        ],
        out_specs=pl.BlockSpec((None, tq, B_GROUP * HEAD_DIM), lambda b, g, i: (b, i, g)),
        out_shape=jax.ShapeDtypeStruct((Bsz, L, B_WIDTH), BF16),
        scratch_shapes=[pltpu.VMEM((B_GROUP * tq, 1), F32), pltpu.VMEM((B_GROUP * tq, LANES), F32)],
        compiler_params=_compiler_params(("parallel", "parallel", "arbitrary")),
        name="gqa_attn",
    )(shift, qkv, qkv, qkv)


def _ssm_kernel(u_ref, k2_ref, wst_ref, wout_ref, a_ref, y_ref, m_ref, s_ref, h_ref, *, n_batch, n_chunks):
    for s in range(SSM_CHUNK):
        first_lag = SSM_CHUNK - 1 - s
        j, a = first_lag % SSM_SHIFTS, first_lag // SSM_SHIFTS
        m_ref[s * SSM_P:(s + 1) * SSM_P, :] = k2_ref[j, :, a * LANES:a * LANES + SSM_FLAT]
    u = u_ref[...]
    s_ref[...] = jnp.dot(u, wst_ref[...], preferred_element_type=F32)
    a = a_ref[...]
    afr, afi = a[:, 0:LANES], a[:, LANES:2 * LANES]
    abr, abi = a[:, 2 * LANES:3 * LANES], a[:, 3 * LANES:4 * LANES]
    zero = jnp.zeros((1, LANES), F32)

    def step(cg, carry):
        new = []
        for b in range(n_batch):
            hr, hi, gr, gi = carry[4 * b:4 * b + 4]
            rf = pl.multiple_of(b * n_chunks + cg * SUBLANES, SUBLANES)
            rb = pl.multiple_of(b * n_chunks + n_chunks - SUBLANES - cg * SUBLANES, SUBLANES)
            sf = s_ref[pl.ds(rf, SUBLANES), 0:2 * LANES]
            sb = s_ref[pl.ds(rb, SUBLANES), 2 * LANES:4 * LANES]
            rows_f, rows_b = [], [None] * SUBLANES
            for j in range(SUBLANES):
                rows_f.append(jnp.concatenate([hr, hi], axis=1))
                hr, hi = (afr * hr - afi * hi + sf[j:j + 1, 0:LANES],
                          afr * hi + afi * hr + sf[j:j + 1, LANES:2 * LANES])
                jb = SUBLANES - 1 - j
                rows_b[jb] = jnp.concatenate([gr, gi], axis=1)
                gr, gi = (abr * gr - abi * gi + sb[jb:jb + 1, 0:LANES],
                          abr * gi + abi * gr + sb[jb:jb + 1, LANES:2 * LANES])
            h_ref[pl.ds(rf, SUBLANES), 0:2 * LANES] = jnp.concatenate(rows_f, axis=0)
            h_ref[pl.ds(rb, SUBLANES), 2 * LANES:4 * LANES] = jnp.concatenate(rows_b, axis=0)
            new += [hr, hi, gr, gi]
        return tuple(new)

    lax.fori_loop(0, n_chunks // SUBLANES, step, (zero,) * (4 * n_batch), unroll=True)
    y = (jnp.dot(u, m_ref[...], preferred_element_type=F32)
         + jnp.dot(h_ref[...].astype(BF16), wout_ref[...], preferred_element_type=F32))
    y_ref[...] = y.astype(y_ref.dtype)


def _ssm_apply(u_g, k2, wst, wout, a64, n_batch, n_chunks):
    nc = n_batch * n_chunks
    kernel = functools.partial(_ssm_kernel, n_batch=n_batch, n_chunks=n_chunks)
    return pl.pallas_call(
        kernel,
        grid=(SSM_GROUPS,),
        in_specs=[
            pl.BlockSpec((None, nc, SSM_FLAT), lambda g: (g, 0, 0)),
            pl.BlockSpec((None, SSM_SHIFTS, SSM_P, SSM_K2_WIDTH), lambda g: (g, 0, 0, 0)),
            pl.BlockSpec((None, SSM_FLAT, SSM_STATE_COLS), lambda g: (g, 0, 0)),
            pl.BlockSpec((None, SSM_STATE_COLS, SSM_FLAT), lambda g: (g, 0, 0)),
            pl.BlockSpec((None, 1, SSM_STATE_COLS), lambda g: (g, 0, 0)),
        ],
        out_specs=pl.BlockSpec((None, nc, SSM_FLAT), lambda g: (g, 0, 0)),
        out_shape=jax.ShapeDtypeStruct((SSM_GROUPS, nc, SSM_FLAT), BF16),
        scratch_shapes=[pltpu.VMEM((SSM_FLAT, SSM_FLAT), BF16),
                        pltpu.VMEM((nc, SSM_STATE_COLS), F32), pltpu.VMEM((nc, SSM_STATE_COLS), F32)],
        compiler_params=_compiler_params(("parallel",)),
        name="ssm_chunks",
    )(u_g, k2, wst, wout, a64)


def _ssm_operators(A_re, A_im, log_dt, B_re, B_im, C_re, C_im):
    hi = lax.Precision.HIGHEST
    G, N, P, Tc = SSM_GROUPS, SSM_N, SSM_P, SSM_CHUNK
    Br, Bi, Cr, Ci = B_re.astype(F32), B_im.astype(F32), C_re.astype(F32), C_im.astype(F32)
    steps = jnp.arange(Tc + 1, dtype=F32)

    per_dir = []
    for dirn in range(2):
        dt = jnp.exp(log_dt[dirn].astype(F32))[:, None]
        ar, ai = A_re[dirn].astype(F32), A_im[dirn].astype(F32)
        mag = jnp.exp(dt * ar)
        er, ei = mag * jnp.cos(dt * ai), mag * jnp.sin(dt * ai)
        den = ar * ar + ai * ai
        fr = ((er - 1.0) * ar + ei * ai) / den
        fi = (ei * ar - (er - 1.0) * ai) / den
        bbr = fr[..., None] * Br - fi[..., None] * Bi
        bbi = fr[..., None] * Bi + fi[..., None] * Br
        pm = jnp.exp(steps[None, :, None] * (dt * ar)[:, None, :])
        ang = steps[None, :, None] * (dt * ai)[:, None, :]
        pr, pi = pm * jnp.cos(ang), pm * jnp.sin(ang)
        car = Cr[:, None] * pr[:, :, None, :] - Ci[:, None] * pi[:, :, None, :]
        cai = Cr[:, None] * pi[:, :, None, :] + Ci[:, None] * pr[:, :, None, :]
        kern = (jnp.einsum('gkpn,gnq->gkpq', car[:, :Tc], bbr, precision=hi)
                - jnp.einsum('gkpn,gnq->gkpq', cai[:, :Tc], bbi, precision=hi))
        bbr_t, bbi_t = jnp.swapaxes(bbr, 1, 2)[:, None], jnp.swapaxes(bbi, 1, 2)[:, None]
        prk, pik = pr[:, :, None, :], pi[:, :, None, :]
        abr = prk * bbr_t - pik * bbi_t
        abi = prk * bbi_t + pik * bbr_t
        cr_t, ci_t = jnp.swapaxes(Cr, 1, 2)[:, :, None, :], jnp.swapaxes(Ci, 1, 2)[:, :, None, :]
        prn, pin = jnp.swapaxes(pr, 1, 2)[..., None], jnp.swapaxes(pi, 1, 2)[..., None]
        per_dir.append(dict(kern=kern, abr=abr, abi=abi, pr=pr, pi=pi,
                            car=cr_t * prn - ci_t * pin, cai=cr_t * pin + ci_t * prn))

    f, b = per_dir
    k2 = jnp.concatenate([b['kern'][:, :0:-1], f['kern'][:, :1] + b['kern'][:, :1], f['kern'][:, 1:]], axis=1)
    k2 = jnp.transpose(k2, (0, 3, 1, 2)).astype(BF16).reshape(G, P, (2 * Tc - 1) * P)
    k2 = jnp.stack([k2[:, :, j * P:j * P + SSM_K2_WIDTH] for j in range(SSM_SHIFTS)], axis=1)

    pad = jnp.zeros((G, Tc * P, LANES - N), BF16)

    def st_cols(xr):
        return jnp.concatenate([xr.astype(BF16).reshape(G, Tc * P, N), pad], axis=-1)

    wst = jnp.concatenate([st_cols(f['abr'][:, Tc - 1::-1][:, :Tc]), st_cols(f['abi'][:, Tc - 1::-1][:, :Tc]),
                           st_cols(b['abr'][:, :Tc]), st_cols(b['abi'][:, :Tc])], axis=-1)

    padr = jnp.zeros((G, LANES - N, Tc * P), BF16)

    def out_rows(x):
        return jnp.concatenate([x.astype(BF16).reshape(G, N, Tc * P), padr], axis=1)

    wout = jnp.concatenate([out_rows(f['car'][:, :, 1:Tc + 1]), out_rows(-f['cai'][:, :, 1:Tc + 1]),
                            out_rows(b['car'][:, :, Tc:0:-1]), out_rows(-b['cai'][:, :, Tc:0:-1])], axis=1)

    padc = jnp.zeros((G, LANES - N), F32)
    a64 = jnp.concatenate([f['pr'][:, Tc], padc, f['pi'][:, Tc], padc,
                           b['pr'][:, Tc], padc, b['pi'][:, Tc], padc], axis=-1)[:, None, :]
    return k2, wst, wout, a64


def _merge_kernel(x_ref, oa_ref, ob_ref, ys_ref, zc_ref, n1_ref, wg_ref, wpa_ref, wpb_ref, wpc_ref,
                  wglu_ref, bglu_ref, dskip_ref, wout_ref, n2_ref, wr_hi_ref, wr_lo_ref, br_ref,
                  x1_ref, xn2_ref, comb_ref):
    x = x_ref[...]
    xn = _rms(x, n1_ref[...]).astype(BF16)
    def gate(branch):
        w = wg_ref[:, branch * D_MODEL:(branch + 1) * D_MODEL]
        return jax.nn.sigmoid(jnp.dot(xn, w, preferred_element_type=F32))

    y = ys_ref[...].astype(F32) + dskip_ref[...] * zc_ref[...]
    g = jax.nn.gelu(y)
    glu = jnp.dot(g.astype(BF16), wglu_ref[...], preferred_element_type=F32) + bglu_ref[...]
    out_c = g * jax.nn.sigmoid(glu)

    merged = gate(0) * jnp.dot(oa_ref[...], wpa_ref[...], preferred_element_type=F32)
    merged += gate(1) * jnp.dot(ob_ref[...], wpb_ref[...], preferred_element_type=F32)
    merged += gate(2) * jnp.dot(out_c.astype(BF16), wpc_ref[...], preferred_element_type=F32)
    x1 = x + jnp.dot(merged.astype(BF16), wout_ref[...], preferred_element_type=F32)
    x1_ref[...] = x1

    xn2 = _rms(x1, n2_ref[...])
    xn2_ref[...] = xn2.astype(BF16)

    lane = lax.broadcasted_iota(jnp.int32, (x.shape[0], LANES), 1).astype(F32)
    neg = jnp.float32(-jnp.inf)
    big = jnp.float32(LANES)
    x_hi = xn2.astype(BF16)
    x_lo = (xn2 - x_hi.astype(F32)).astype(BF16)
    le = (jnp.dot(x_hi, wr_hi_ref[...], preferred_element_type=F32)
          + (jnp.dot(x_hi, wr_lo_ref[...], preferred_element_type=F32)
             + jnp.dot(x_lo, wr_hi_ref[...], preferred_element_type=F32))) + br_ref[...]
    lg = jnp.where((lane >= N_EXPERTS) & (lane < N_EXPERTS + MOE_GROUPS), le, neg)
    g_max = jnp.max(lg, axis=-1, keepdims=True)
    g_idx = jnp.min(jnp.where(lg == g_max, lane, big), axis=-1, keepdims=True) - float(N_EXPERTS)
    g_w = 1.0 / jnp.sum(jnp.exp(lg - g_max), axis=-1, keepdims=True)
    in_group = (lane >= g_idx * EXPERTS_PER_GROUP) & (lane < (g_idx + 1) * EXPERTS_PER_GROUP)
    v = jnp.where(in_group, le, neg)
    top1 = jnp.max(v, axis=-1, keepdims=True)
    i1 = jnp.min(jnp.where(v == top1, lane, big), axis=-1, keepdims=True)
    v2 = jnp.where(lane == i1, neg, v)
    top2 = jnp.max(v2, axis=-1, keepdims=True)
    i2 = jnp.min(jnp.where(v2 == top2, lane, big), axis=-1, keepdims=True)
    e2 = jnp.exp(top2 - top1)
    inv = 1.0 / (1.0 + e2)
    comb_ref[...] = (jnp.where(lane == i1, inv * g_w, 0.0) + jnp.where(lane == i2, e2 * inv * g_w, 0.0))


def _merge(x2d, oa, ob, ys, zc, weights):
    T = x2d.shape[0]
    tm = MERGE_TILE

    def rows(width):
        return pl.BlockSpec((tm, width), lambda i: (i, 0))

    w_specs = [pl.BlockSpec(w.shape, lambda i, nd=w.ndim: (0,) * nd, pipeline_mode=pl.Buffered(1)) for w in weights]
    return pl.pallas_call(
        _merge_kernel,
        grid=(T // tm,),
        in_specs=[rows(D_MODEL), rows(A_WIDTH), rows(B_WIDTH), rows(SSM_WIDTH), rows(SSM_WIDTH)] + w_specs,
        out_specs=[rows(D_MODEL), rows(D_MODEL), rows(LANES)],
        out_shape=[jax.ShapeDtypeStruct((T, D_MODEL), F32), jax.ShapeDtypeStruct((T, D_MODEL), BF16),
                   jax.ShapeDtypeStruct((T, LANES), F32)],
        compiler_params=_compiler_params(("parallel",)),
        name="merge_router",
    )(x2d, oa, ob, ys, zc, *weights)


def _moe_kernel(xn_ref, comb_ref, x1_ref, wg_ref, wu_ref, wd_ref, fw_ref, o_ref, acc_ref, *, final_norm):
    step = pl.program_id(1)

    @pl.when(step == 0)
    def _():
        acc_ref[...] = jnp.zeros(acc_ref.shape, F32)

    xn = xn_ref[...]
    comb = comb_ref[...]
    lane = lax.broadcasted_iota(jnp.int32, comb.shape, 1)
    hidden = []
    for j in range(EXPERTS_PER_STEP):
        h = jax.nn.silu(jnp.dot(xn, wg_ref[j], preferred_element_type=F32)) * jnp.dot(
            xn, wu_ref[j], preferred_element_type=F32)
        c = jnp.sum(jnp.where(lane == step * EXPERTS_PER_STEP + j, comb, 0.0), axis=-1, keepdims=True)
        hidden.append((c * h).astype(BF16))
    wd = wd_ref[...].reshape(EXPERTS_PER_STEP * D_FF_EXPERT, D_MODEL)
    acc_ref[...] += jnp.dot(jnp.concatenate(hidden, axis=1), wd, preferred_element_type=F32)

    @pl.when(step == N_EXPERTS // EXPERTS_PER_STEP - 1)
    def _():
        out = x1_ref[...] + acc_ref[...]
        if final_norm:
            out = _rms(out, fw_ref[...])
        o_ref[...] = out


def _moe(xn2, comb, x1, wg, wu, wd, fw, layer, final_norm):
    T = x1.shape[0]
    tm = EXPERT_TILE
    kernel = functools.partial(_moe_kernel, final_norm=final_norm)
    return pl.pallas_call(
        kernel,
        grid=(T // tm, N_EXPERTS // EXPERTS_PER_STEP),
        in_specs=[
            pl.BlockSpec((tm, D_MODEL), lambda i, e: (i, 0)),
            pl.BlockSpec((tm, LANES), lambda i, e: (i, 0)),
            pl.BlockSpec((tm, D_MODEL), lambda i, e: (i, 0)),
            pl.BlockSpec((None, EXPERTS_PER_STEP, D_MODEL, D_FF_EXPERT), lambda i, e: (layer, e, 0, 0)),
            pl.BlockSpec((None, EXPERTS_PER_STEP, D_MODEL, D_FF_EXPERT), lambda i, e: (layer, e, 0, 0)),
            pl.BlockSpec((None, EXPERTS_PER_STEP, D_FF_EXPERT, D_MODEL), lambda i, e: (layer, e, 0, 0)),
            _const_spec((1, D_MODEL)),
        ],
        out_specs=pl.BlockSpec((tm, D_MODEL), lambda i, e: (i, 0)),
        out_shape=jax.ShapeDtypeStruct((T, D_MODEL), F32),
        scratch_shapes=[pltpu.VMEM((tm, D_MODEL), F32)],
        compiler_params=_compiler_params(("parallel", "arbitrary")),
        name="experts",
    )(xn2, comb, x1, wg, wu, wd, fw)


def _pad_heads(w, n_heads, width):
    w = w.reshape(D_MODEL, n_heads, -1)
    return jnp.pad(w, ((0, 0), (0, 0), (0, width - w.shape[-1]))).reshape(D_MODEL, n_heads * width)


def _pad_vec(v, width=LANES):
    return jnp.pad(v.astype(F32), (0, width - v.shape[0]))[None, :]


def _layer_params(l, p):
    scale = HEAD_DIM ** -0.5
    w_in = p['w_in'][l]
    wa, wb, wc, wg = jnp.split(w_in, [COLS_A, COLS_A + COLS_B, COLS_A + COLS_B + COLS_C], axis=-1)

    wqa, wka, wva = jnp.split(wa, [A_QK_COLS, 2 * A_QK_COLS], axis=-1)
    wva = _pad_heads(wva, A_HEADS, 2 * LANES)
    w_a = jnp.concatenate([wqa * scale, wka, wva], axis=-1).astype(BF16)
    ones_a = jnp.zeros((A_HEADS, 2 * LANES), F32).at[:, A_VDIM].set(1.0).reshape(1, -1)
    e_a = jnp.concatenate([jnp.zeros((1, 2 * A_QK_COLS), F32), ones_a], axis=-1)

    wqb, wkb, wvb = jnp.split(wb, [B_HEADS * HEAD_DIM, (B_HEADS + B_KV_HEADS) * HEAD_DIM], axis=-1)
    w_b = jnp.concatenate([_pad_heads(wqb, B_HEADS, LANES), _pad_heads(wkb, B_KV_HEADS, LANES),
                           _pad_heads(wvb, B_KV_HEADS, LANES)], axis=-1).astype(BF16)
    e_b = jnp.zeros((B_KV_HEADS, LANES), F32).at[:, HEAD_DIM].set(1.0).reshape(1, -1)
    qw = p['q_norm_w'][l].astype(F32) * scale
    kw = p['k_norm_w'][l].astype(F32)
    shift_b = (HEAD_DIM * SHIFT_MARGIN) * jnp.max(jnp.abs(qw)) * jnp.max(jnp.abs(kw))
    e_q = jnp.zeros((1, LANES), F32).at[0, HEAD_DIM].set(1.0)
    e_k = jnp.zeros((1, LANES), F32).at[0, HEAD_DIM].set(-shift_b)

    w_r = jnp.pad(jnp.concatenate([p['w_router_expert'][l], p['w_router_group'][l]], axis=-1).astype(F32),
                  ((0, 0), (0, LANES - N_EXPERTS - MOE_GROUPS)))
    w_r_hi = w_r.astype(BF16)
    merge_w = [
        p['norm1_w'][l][None, :], wg.astype(BF16), p['w_proj_a'][l].astype(BF16),
        p['w_proj_b'][l].astype(BF16),
        p['w_proj_c'][l].astype(BF16), p['w_glu'][l].astype(BF16), p['b_glu'][l][None, :], p['ssm_D'][l][None, :],
        p['w_out'][l].astype(BF16), p['norm2_w'][l][None, :],
        w_r_hi, (w_r - w_r_hi.astype(F32)).astype(BF16),
        _pad_vec(jnp.concatenate([p['b_router_expert'][l], p['b_router_group'][l]])),
    ]
    lam_init = 0.8 - 0.6 * math.exp(-0.3 * l)
    lamv = jnp.stack([_pad_vec(v[l])[0] for v in (p['lam_q1'], p['lam_k1'], p['lam_q2'], p['lam_k2'])])
    return dict(
        norm1=p['norm1_w'][l][None, :], w_a=w_a, e_a=e_a, w_b=w_b, e_b=e_b, w_c=wc.astype(BF16),
        qw=_pad_vec(qw), kw=_pad_vec(kw), e_q=e_q, e_k=e_k, shift_b=shift_b.reshape(1),
        lamv=lamv, lam_init=lam_init, subw=p['diff_subln_w'][l][None, :],
        ssm=_ssm_operators(p['ssm_A_re'][l], p['ssm_A_im'][l], p['ssm_log_dt'][l], p['ssm_B_re'][l],
                           p['ssm_B_im'][l], p['ssm_C_re'][l], p['ssm_C_im'][l]),
        merge_w=merge_w,
        wg=p['w_exp_gate'].astype(BF16), wu=p['w_exp_up'].astype(BF16), wd=p['w_exp_down'].astype(BF16),
    )


def _rope_tables(L):
    half = HEAD_DIM // 2
    inv = ROPE_BASE ** (-jnp.arange(0, half, 2, dtype=F32) / half)
    t = jnp.arange(L)
    row = (t // GRID_W).astype(F32)
    col = (t % GRID_W).astype(F32)
    ang = jnp.concatenate([row[:, None] * inv[None, :]] * 2 + [col[:, None] * inv[None, :]] * 2, axis=-1)
    cos, sin = jnp.cos(ang), jnp.sin(ang)
    first = (jnp.arange(HEAD_DIM) % half) < (half // 2)
    pad = ((0, 0), (0, LANES - HEAD_DIM))
    return (jnp.pad(cos, pad), jnp.pad(jnp.where(first, -sin, 0.0), pad), jnp.pad(jnp.where(first, 0.0, sin), pad))


def _alibi_slopes():
    s = 2.0 ** (-8.0 * jnp.arange(1, A_HEADS + 1, dtype=F32) / A_HEADS)
    return jnp.broadcast_to(s[:, None, None], (A_HEADS, 1, LANES))


def _trunk(x, layers, final_norm_w):
    Bsz, L, _ = x.shape
    T = Bsz * L
    tm = ROW_TILE
    n_chunks = L // SSM_CHUNK
    pos_blocks = L // tm
    cos, sa, sb = _rope_tables(L)
    slopes = _alibi_slopes()
    fw = final_norm_w[None, :]
    x2d = x.reshape(T, D_MODEL)

    for l, lp in enumerate(layers):
        qkv_a, qkv_b, zc32, zc16 = _project(x2d, lp, (cos, sa, sb), pos_blocks)
        out_a = _diff_attention(qkv_a.reshape(Bsz, L, -1), slopes, lp['lamv'], lp['subw'], lp['lam_init'], Bsz, L)
        out_b = _gqa_attention(qkv_b.reshape(Bsz, L, -1), lp['shift_b'], Bsz, L)

        u_g = jnp.transpose(zc16.reshape(Bsz, n_chunks, SSM_CHUNK, SSM_GROUPS, SSM_P), (3, 0, 1, 2, 4))
        u_g = u_g.reshape(SSM_GROUPS, Bsz * n_chunks, SSM_FLAT)
        y_g = _ssm_apply(u_g, *lp['ssm'], Bsz, n_chunks)
        y_s = jnp.transpose(y_g.reshape(SSM_GROUPS, Bsz, n_chunks, SSM_CHUNK, SSM_P), (1, 2, 3, 0, 4))
        y_s = y_s.reshape(T, SSM_WIDTH)

        x1, xn2, comb = _merge(x2d, out_a.reshape(T, A_WIDTH), out_b.reshape(T, B_WIDTH), y_s, zc32,
                               lp['merge_w'])
        x2d = _moe(xn2, comb, x1, lp['wg'], lp['wu'], lp['wd'], fw, layer=l, final_norm=(l == len(layers) - 1))
    return x2d.reshape(Bsz, L, D_MODEL)


def kernel(x_prompt, x_sample, norm1_w, w_in, lam_q1, lam_k1, lam_q2, lam_k2, diff_subln_w, q_norm_w, k_norm_w,
           ssm_A_re, ssm_A_im, ssm_log_dt, ssm_B_re, ssm_B_im, ssm_C_re, ssm_C_im, ssm_D, w_glu, b_glu,
           w_proj_a, w_proj_b, w_proj_c, w_out, norm2_w, w_router_group, b_router_group, w_router_expert,
           b_router_expert, w_exp_gate, w_exp_up, w_exp_down, final_norm_w):
    p = dict(norm1_w=norm1_w, w_in=w_in, lam_q1=lam_q1, lam_k1=lam_k1, lam_q2=lam_q2, lam_k2=lam_k2,
             diff_subln_w=diff_subln_w, q_norm_w=q_norm_w, k_norm_w=k_norm_w, ssm_A_re=ssm_A_re, ssm_A_im=ssm_A_im,
             ssm_log_dt=ssm_log_dt, ssm_B_re=ssm_B_re, ssm_B_im=ssm_B_im, ssm_C_re=ssm_C_re, ssm_C_im=ssm_C_im,
             ssm_D=ssm_D, w_glu=w_glu, b_glu=b_glu, w_proj_a=w_proj_a, w_proj_b=w_proj_b, w_proj_c=w_proj_c,
             w_out=w_out, norm2_w=norm2_w, w_router_group=w_router_group, b_router_group=b_router_group,
             w_router_expert=w_router_expert, b_router_expert=b_router_expert, w_exp_gate=w_exp_gate,
             w_exp_up=w_exp_up, w_exp_down=w_exp_down)
    layers = [_layer_params(l, p) for l in range(DEPTH)]
    return (_trunk(x_prompt, layers, final_norm_w), _trunk(x_sample, layers, final_norm_w))
```
